```python
import jax, jax.numpy as jnp
from jax import lax
import numpy as np

D_MODEL = 4096
BATCH = 2
SEQ = 4096
DEPTH = 2

FOX_HEADS = 16
FOX_HEAD_DIM = 128
FOX_WIDTH = FOX_HEADS * FOX_HEAD_DIM
FOX_BLOCK = 128
GLA_HEADS = 4
GLA_KEY_DIM = 256
GLA_VAL_DIM = 512
GLA_KEY_WIDTH = GLA_HEADS * GLA_KEY_DIM
GLA_WIDTH = GLA_HEADS * GLA_VAL_DIM
GLA_GATE_RANK = 16
GLA_GATE_NORMALIZER = 16.0
GLA_CHUNK = 64
MIX_WIDTH = FOX_WIDTH + GLA_WIDTH
IN_SIZES = (FOX_WIDTH, FOX_WIDTH, FOX_WIDTH, FOX_HEADS,
            GLA_KEY_WIDTH, GLA_KEY_WIDTH, GLA_WIDTH, GLA_WIDTH, GLA_GATE_RANK)
VALUE_SLOTS = (2, 6)
IN_COLS = sum(IN_SIZES)
D_FF = 256 * ((8 * D_MODEL + 3 * 256 - 1) // (3 * 256))
ALPHA = (2 * DEPTH) ** 0.25
BETA = (8 * DEPTH) ** -0.25
EPS = 1e-5

kernel_name = "fox_gla_parallel_heads_deepnorm"


def layer_norm(t, g, b):
    t32 = t.astype(jnp.float32)
    mu = jnp.mean(t32, axis=-1, keepdims=True)
    var = jnp.mean(jnp.square(t32 - mu), axis=-1, keepdims=True)
    return ((t32 - mu) * lax.rsqrt(var + EPS) * g + b).astype(t.dtype)


def rms_norm(t, g):
    t32 = t.astype(jnp.float32)
    return t32 * lax.rsqrt(jnp.mean(jnp.square(t32), axis=-1, keepdims=True) + EPS) * g


def fox_attention(q, k, v, c):
    B, H, S, D = q.shape
    n_blocks = S // FOX_BLOCK
    scale = D ** -0.5
    k_pos = jnp.arange(S)

    def block(i):
        start = i * FOX_BLOCK
        qb = lax.dynamic_slice_in_dim(q, start, FOX_BLOCK, axis=2)
        cb = lax.dynamic_slice_in_dim(c, start, FOX_BLOCK, axis=2)
        logits = jnp.einsum('bhqd,bhkd->bhqk', qb, k) * scale + cb[..., :, None] - c[..., None, :]
        q_pos = start + jnp.arange(FOX_BLOCK)
        logits = jnp.where(q_pos[:, None] >= k_pos[None, :], logits, -jnp.inf)
        p = jax.nn.softmax(logits, axis=-1)
        return jnp.einsum('bhqk,bhkd->bhqd', p, v)

    out = lax.map(block, jnp.arange(n_blocks))
    return out.transpose(1, 2, 0, 3, 4).reshape(B, H, S, D)


def gla_chunked(q, k, v, g):
    B, S, H, DK = q.shape
    DV = v.shape[-1]
    n_chunks = S // GLA_CHUNK

    def to_chunks(t):
        return t.reshape(B, n_chunks, GLA_CHUNK, H, t.shape[-1]).transpose(1, 0, 3, 2, 4)

    qc, kc, vc, gc = to_chunks(q), to_chunks(k), to_chunks(v), to_chunks(g)
    bc = jnp.cumsum(gc, axis=-2)
    causal = jnp.tril(jnp.ones((GLA_CHUNK, GLA_CHUNK), dtype=bool))

    def step(state, inp):
        q_, k_, v_, b_ = inp
        b_last = b_[..., -1, :]
        o_inter = jnp.einsum('bhid,bhde->bhie', q_ * jnp.exp(b_), state)
        rel = b_[:, :, :, None, :] - b_[:, :, None, :, :]
        decay = jnp.exp(jnp.where(causal[None, None, :, :, None], rel, -jnp.inf))
        scores = jnp.einsum('bhid,bhjd,bhijd->bhij', q_, k_, decay)
        o_intra = jnp.einsum('bhij,bhje->bhie', scores, v_)
        k_dec = k_ * jnp.exp(b_last[:, :, None, :] - b_)
        state = state * jnp.exp(b_last)[..., None] + jnp.einsum('bhjd,bhje->bhde', k_dec, v_)
        return state, o_inter + o_intra

    state0 = jnp.zeros((B, H, DK, DV), jnp.float32)
    _, out = lax.scan(step, state0, (qc, kc, vc, bc))
    return out.transpose(1, 0, 3, 2, 4).reshape(B, S, H, DV)


def hybrid_mixer(h, w_in, b_f, w_gk_up, b_gk, fox_norm_g, gla_norm_g, w_out):
    B, S, _ = h.shape
    f32 = jnp.float32
    split_at = []
    acc = 0
    for size in IN_SIZES[:-1]:
        acc += size
        split_at.append(acc)
    proj = h @ w_in
    fq, fk, fv, f_logit, gq, gk, gv, gg, g_low = jnp.split(proj, split_at, axis=-1)

    def heads(t):
        return t.reshape(B, S, FOX_HEADS, FOX_HEAD_DIM).transpose(0, 2, 1, 3).astype(f32)
    log_f = jax.nn.log_sigmoid((f_logit + b_f).astype(f32))
    c = jnp.cumsum(log_f, axis=1).transpose(0, 2, 1)
    o_fox = fox_attention(heads(fq), heads(fk), heads(fv), c)
    o_fox = rms_norm(o_fox, fox_norm_g[:, None, :])
    o_fox = o_fox.transpose(0, 2, 1, 3).reshape(B, S, FOX_WIDTH)

    log_a = jax.nn.log_sigmoid((g_low @ w_gk_up + b_gk).astype(f32)) / GLA_GATE_NORMALIZER
    q_g = gq.reshape(B, S, GLA_HEADS, GLA_KEY_DIM).astype(f32) * (GLA_KEY_DIM ** -0.5)
    k_g = gk.reshape(B, S, GLA_HEADS, GLA_KEY_DIM).astype(f32)
    v_g = gv.reshape(B, S, GLA_HEADS, GLA_VAL_DIM).astype(f32)
    a_g = log_a.reshape(B, S, GLA_HEADS, GLA_KEY_DIM)
    o_gla = gla_chunked(q_g, k_g, v_g, a_g)
    gate = jax.nn.silu(gg.reshape(B, S, GLA_HEADS, GLA_VAL_DIM).astype(f32))
    o_gla = (rms_norm(o_gla, gla_norm_g) * gate).reshape(B, S, GLA_WIDTH)

    mixed = jnp.concatenate([o_fox, o_gla], axis=-1).astype(h.dtype)
    return mixed @ w_out


def swiglu(h, w_gate, w_up, w_down):
    return (jax.nn.silu(h @ w_gate) * (h @ w_up)) @ w_down


def setup_inputs(seed: int = 0) -> dict:
    key = jax.random.key(seed)
    ks = jax.random.split(key, 17)
    f32 = jnp.float32

    def nrm(k, shape, scale):
        return jax.random.normal(k, shape, f32) * scale

    col_scale = jnp.concatenate([jnp.full((s,), BETA if i in VALUE_SLOTS else 1.0, f32)
                                 for i, s in enumerate(IN_SIZES)])
    return {
        "x": nrm(ks[0], (BATCH, SEQ, D_MODEL), 1.0),
        "ln_in_g": 1.0 + nrm(ks[1], (D_MODEL,), 0.01),
        "ln_in_b": nrm(ks[2], (D_MODEL,), 0.01),
        "w_in": nrm(ks[3], (DEPTH, D_MODEL, IN_COLS), D_MODEL ** -0.5) * col_scale,
        "b_f": jax.random.uniform(ks[4], (DEPTH, FOX_HEADS), f32, 3.0, 6.0),
        "w_gk_up": nrm(ks[5], (DEPTH, GLA_GATE_RANK, GLA_KEY_WIDTH), GLA_GATE_RANK ** -0.5),
        "b_gk": nrm(ks[6], (DEPTH, GLA_KEY_WIDTH), 0.01),
        "fox_norm_g": 1.0 + nrm(ks[7], (DEPTH, FOX_HEADS, FOX_HEAD_DIM), 0.01),
        "gla_norm_g": 1.0 + nrm(ks[8], (DEPTH, GLA_VAL_DIM), 0.01),
        "w_out": nrm(ks[9], (DEPTH, MIX_WIDTH, D_MODEL), BETA * MIX_WIDTH ** -0.5),
        "ln1_g": 1.0 + nrm(ks[10], (DEPTH, D_MODEL), 0.01),
        "ln1_b": nrm(ks[11], (DEPTH, D_MODEL), 0.01),
        "w_gate": nrm(ks[12], (DEPTH, D_MODEL, D_FF), BETA * D_MODEL ** -0.5),
        "w_up": nrm(ks[13], (DEPTH, D_MODEL, D_FF), BETA * D_MODEL ** -0.5),
        "w_down": nrm(ks[14], (DEPTH, D_FF, D_MODEL), BETA * D_FF ** -0.5),
        "ln2_g": 1.0 + nrm(ks[15], (DEPTH, D_MODEL), 0.01),
        "ln2_b": nrm(ks[16], (DEPTH, D_MODEL), 0.01),
    }


def reference(x, ln_in_g, ln_in_b, w_in, b_f, w_gk_up, b_gk, fox_norm_g, gla_norm_g,
              w_out, ln1_g, ln1_b, w_gate, w_up, w_down, ln2_g, ln2_b):
    x = layer_norm(x, ln_in_g, ln_in_b)
    for l in range(DEPTH):
        mix = hybrid_mixer(x, w_in[l], b_f[l], w_gk_up[l], b_gk[l],
                           fox_norm_g[l], gla_norm_g[l], w_out[l])
        x = layer_norm(ALPHA * x + mix, ln1_g[l], ln1_b[l])
        ffn = swiglu(x, w_gate[l], w_up[l], w_down[l])
        x = layer_norm(ALPHA * x + ffn, ln2_g[l], ln2_b[l])
    return x
```

```python
import functools

import numpy as np
import jax
import jax.numpy as jnp
from jax import lax
from jax.experimental import pallas as pl
from jax.experimental.pallas import tpu as pltpu

F32 = jnp.float32
BF16 = jnp.bfloat16

V7X_VMEM_BYTES = 64 * 2**20
V7X_LANES = 128
V7X_MXU_DIM = 256

FOX_HEADS = 16
FOX_HEAD_DIM = 128
FOX_WIDTH = FOX_HEADS * FOX_HEAD_DIM
GLA_HEADS = 4
GLA_KEY_DIM = 256
GLA_VAL_DIM = 512
GLA_KEY_WIDTH = GLA_HEADS * GLA_KEY_DIM
GLA_WIDTH = GLA_HEADS * GLA_VAL_DIM
GLA_GATE_RANK = 16
GLA_GATE_NORMALIZER = 16.0
MIX_WIDTH = FOX_WIDTH + GLA_WIDTH
EPS = 1e-5

LN_ROWS = 256
MM_TM = 1024
MM_TN = 1024
FFN_TN = 512
FFN_PAD = 1024
DOWN_TK = 2816
GATE_ROWS = 512
FOX_TQ = 512
FOX_TK = 512
GLA_ROWS = 512
GLA_CHUNK = 64
NEG_BIG = -1e30

_NT = (((1,), (1,)), ((), ()))
_TN = (((0,), (0,)), ((), ()))


def _vmem_limit(block_bytes, extra_bytes):
    need = 2 * block_bytes + extra_bytes + (4 << 20)
    return int(min(need, V7X_VMEM_BYTES - (6 << 20)))


def _nbytes(shape, dtype):
    return int(np.prod(shape)) * jnp.dtype(dtype).itemsize


def _log_sigmoid(x):
    return jnp.minimum(x, 0.0) - jnp.log1p(jnp.exp(-jnp.abs(x)))


def _silu(x):
    return x * (1.0 / (1.0 + jnp.exp(-x)))


def _ln_math(t, g, b):
    mu = jnp.mean(t, axis=-1, keepdims=True)
    tc = t - mu
    var = jnp.mean(tc * tc, axis=-1, keepdims=True)
    return tc * lax.rsqrt(var + EPS) * g + b


def _ln_kernel(*refs, alpha, has_res, want_f32, want_bf16):
    if has_res:
        x_ref, y_ref, g_ref, b_ref = refs[:4]
        outs = refs[4:]
        t = alpha * x_ref[...] + y_ref[...]
    else:
        x_ref, g_ref, b_ref = refs[:3]
        outs = refs[3:]
        t = x_ref[...]
    y = _ln_math(t, g_ref[...], b_ref[...])
    i = 0
    if want_f32:
        outs[i][...] = y
        i += 1
    if want_bf16:
        outs[i][...] = y.astype(BF16)


def _layer_norm(x, res, g, b, *, alpha=1.0, want_f32=True, want_bf16=True):
    rows, d = x.shape
    assert rows % LN_ROWS == 0
    row_spec = pl.BlockSpec((LN_ROWS, d), lambda i: (i, 0))
    vec_spec = pl.BlockSpec((1, d), lambda i: (0, 0))
    ins = [x] + ([res] if res is not None else []) + [g.reshape(1, d), b.reshape(1, d)]
    in_specs = [row_spec] * (len(ins) - 2) + [vec_spec, vec_spec]
    out_shape, out_specs = [], []
    if want_f32:
        out_shape.append(jax.ShapeDtypeStruct((rows, d), F32))
        out_specs.append(row_spec)
    if want_bf16:
        out_shape.append(jax.ShapeDtypeStruct((rows, d), BF16))
        out_specs.append(row_spec)
    blk = _nbytes((LN_ROWS, d), F32) * (len(ins) - 2 + len(out_shape))
    return pl.pallas_call(
        functools.partial(_ln_kernel, alpha=alpha, has_res=res is not None,
                          want_f32=want_f32, want_bf16=want_bf16),
        grid=(rows // LN_ROWS,),
        in_specs=in_specs, out_specs=out_specs, out_shape=out_shape,
        compiler_params=pltpu.CompilerParams(
            dimension_semantics=("arbitrary",),
            vmem_limit_bytes=_vmem_limit(blk, 4 * _nbytes((LN_ROWS, d), F32))),
        name="layer_norm",
    )(*ins)


def _mm_kernel(x_ref, w_ref, o_ref, *scratch, nk):
    part = jnp.dot(x_ref[...], w_ref[...], preferred_element_type=F32)
    if nk == 1:
        o_ref[...] = part.astype(o_ref.dtype)
        return
    acc_ref, = scratch
    k = pl.program_id(2)

    @pl.when(k == 0)
    def _():
        acc_ref[...] = part

    @pl.when(jnp.logical_and(k > 0, k < nk - 1))
    def _():
        acc_ref[...] += part

    @pl.when(k == nk - 1)
    def _():
        o_ref[...] = (acc_ref[...] + part).astype(o_ref.dtype)


def _matmul(x, w, out_dtype, *, tm=MM_TM, tn=MM_TN, tk=None):
    m, kdim = x.shape
    n = w.shape[1]
    tk = kdim if tk is None else tk
    assert m % tm == 0 and n % tn == 0 and kdim % tk == 0
    nk = kdim // tk
    blk = _nbytes((tm, tk), x.dtype) + _nbytes((tk, tn), w.dtype) + _nbytes((tm, tn), out_dtype)
    acc = _nbytes((tm, tn), F32)
    return pl.pallas_call(
        functools.partial(_mm_kernel, nk=nk),
        grid=(m // tm, n // tn, nk),
        in_specs=[pl.BlockSpec((tm, tk), lambda i, j, k: (i, k)),
                  pl.BlockSpec((tk, tn), lambda i, j, k: (k, j))],
        out_specs=pl.BlockSpec((tm, tn), lambda i, j, k: (i, j)),
        out_shape=jax.ShapeDtypeStruct((m, n), out_dtype),
        scratch_shapes=[pltpu.VMEM((tm, tn), F32)] if nk > 1 else [],
        compiler_params=pltpu.CompilerParams(
            dimension_semantics=("arbitrary", "arbitrary", "arbitrary"),
            vmem_limit_bytes=_vmem_limit(blk, 2 * acc)),
        name="matmul",
    )(x, w)


def _gate_up_kernel(x_ref, w_ref, o_ref, *, tn):
    z = jnp.dot(x_ref[...], w_ref[...], preferred_element_type=F32)
    o_ref[...] = (_silu(z[:, :tn]) * z[:, tn:]).astype(o_ref.dtype)


def _gate_up(x, w_gu, *, tm=MM_TM, tn=FFN_TN):
    m, kdim = x.shape
    n = w_gu.shape[1] // 2
    assert m % tm == 0 and n % tn == 0
    blk = _nbytes((tm, kdim), x.dtype) + _nbytes((kdim, 2 * tn), w_gu.dtype) + _nbytes((tm, tn), BF16)
    return pl.pallas_call(
        functools.partial(_gate_up_kernel, tn=tn),
        grid=(m // tm, n // tn),
        in_specs=[pl.BlockSpec((tm, kdim), lambda i, j: (i, 0)),
                  pl.BlockSpec((kdim, 2 * tn), lambda i, j: (0, j))],
        out_specs=pl.BlockSpec((tm, tn), lambda i, j: (i, j)),
        out_shape=jax.ShapeDtypeStruct((m, n), BF16),
        compiler_params=pltpu.CompilerParams(
            dimension_semantics=("arbitrary", "arbitrary"),
            vmem_limit_bytes=_vmem_limit(blk, 3 * _nbytes((tm, 2 * tn), F32))),
        name="gate_up",
    )(x, w_gu)


def _gates_kernel(x_ref, w_ref, bias_ref, tri_ref, o_ref, carry_ref, *, steps_per_seq):
    i = pl.program_id(0)

    @pl.when(i % steps_per_seq == 0)
    def _():
        carry_ref[...] = jnp.zeros_like(carry_ref)

    z = jnp.dot(x_ref[...], w_ref[...], preferred_element_type=F32)
    log_f = _log_sigmoid(z + bias_ref[...])
    c = jnp.dot(tri_ref[...], log_f, preferred_element_type=F32,
                precision=lax.Precision.HIGHEST) + carry_ref[0:1, :]
    rows = c.shape[0]
    carry_ref[...] = jnp.broadcast_to(c[rows - 1:rows, :], carry_ref.shape)
    lane = lax.broadcasted_iota(jnp.int32, c.shape, 1)
    o_ref[...] = jnp.where(lane < FOX_HEADS, c, z)


def _gates(x, w_small, bias, seq):
    m, kdim = x.shape
    tri = jnp.asarray(np.tril(np.ones((GATE_ROWS, GATE_ROWS), np.float32)))
    blk = (_nbytes((GATE_ROWS, kdim), x.dtype) + _nbytes((kdim, V7X_LANES), w_small.dtype)
           + _nbytes((GATE_ROWS, GATE_ROWS), F32) + _nbytes((GATE_ROWS, V7X_LANES), F32))
    return pl.pallas_call(
        functools.partial(_gates_kernel, steps_per_seq=seq // GATE_ROWS),
        grid=(m // GATE_ROWS,),
        in_specs=[pl.BlockSpec((GATE_ROWS, kdim), lambda i: (i, 0)),
                  pl.BlockSpec((kdim, V7X_LANES), lambda i: (0, 0)),
                  pl.BlockSpec((1, V7X_LANES), lambda i: (0, 0)),
                  pl.BlockSpec((GATE_ROWS, GATE_ROWS), lambda i: (0, 0))],
        out_specs=pl.BlockSpec((GATE_ROWS, V7X_LANES), lambda i: (i, 0)),
        out_shape=jax.ShapeDtypeStruct((m, V7X_LANES), F32),
        scratch_shapes=[pltpu.VMEM((8, V7X_LANES), F32)],
        compiler_params=pltpu.CompilerParams(
            dimension_semantics=("arbitrary",),
            vmem_limit_bytes=_vmem_limit(blk, 8 * _nbytes((GATE_ROWS, V7X_LANES), F32))),
        name="gates",
    )(x, w_small, bias, tri)


def _fox_kernel(q_ref, k_ref, v_ref, ccol_ref, crow_ref, g_ref, o_ref, m_ref, l_ref, acc_ref,
                *, tq, tk, scale):
    h = pl.program_id(1)
    qi = pl.program_id(2)
    q = q_ref[0]
    sel = (lax.broadcasted_iota(jnp.int32, (V7X_LANES, V7X_LANES), 0) == h).astype(F32)
    c_t = jnp.dot(ccol_ref[0], sel, preferred_element_type=F32,
                  precision=lax.Precision.HIGHEST)

    m_ref[...] = jnp.full_like(m_ref, NEG_BIG)
    l_ref[...] = jnp.zeros_like(l_ref)
    acc_ref[...] = jnp.zeros_like(acc_ref)
    reps = tk // V7X_LANES

    def step(j, masked):
        start = pl.multiple_of(j * tk, tk)
        kj = k_ref[0, pl.ds(start, tk), :]
        vj = v_ref[0, pl.ds(start, tk), :]
        a = lax.dot_general(q, kj, _NT, preferred_element_type=F32) * scale
        a = a - crow_ref[0, 0, :, pl.ds(start, tk)]
        if masked:
            row = lax.broadcasted_iota(jnp.int32, (tq, tk), 0)
            col = lax.broadcasted_iota(jnp.int32, (tq, tk), 1)
            a = jnp.where(row >= col, a, NEG_BIG)
        m_prev = m_ref[...]
        m_new = jnp.maximum(m_prev, jnp.max(a, axis=-1, keepdims=True) + c_t)
        p = jnp.exp(a - pltpu.repeat(m_new - c_t, reps, axis=1))
        alpha = jnp.exp(m_prev - m_new)
        l_ref[...] = alpha * l_ref[...] + jnp.sum(p, axis=-1, keepdims=True)
        acc_ref[...] = alpha * acc_ref[...] + jnp.dot(p.astype(BF16), vj, preferred_element_type=F32)
        m_ref[...] = m_new

    def body(j, carry):
        step(j, False)
        return carry

    lax.fori_loop(0, qi, body, 0)
    step(qi, True)

    o = acc_ref[...] / l_ref[...]
    o = o * lax.rsqrt(jnp.mean(o * o, axis=-1, keepdims=True) + EPS) * g_ref[0]
    o_ref[0] = o.astype(o_ref.dtype)


def _fox(proj, gates, c_rows, norm_g, *, tq=FOX_TQ, tk=FOX_TK):
    bsz, seq, _ = proj.shape
    d = FOX_HEAD_DIM
    assert tq == tk and seq % tq == 0
    blk = (_nbytes((tq, d), BF16) * 2 + 2 * _nbytes((seq, d), BF16) + _nbytes((tq, V7X_LANES), F32)
           + _nbytes((8, seq), F32) + _nbytes((8, d), F32))
    return pl.pallas_call(
        functools.partial(_fox_kernel, tq=tq, tk=tk, scale=d ** -0.5),
        grid=(bsz, FOX_HEADS, seq // tq),
        in_specs=[pl.BlockSpec((1, tq, d), lambda b, h, i: (b, i, h)),
                  pl.BlockSpec((1, seq, d), lambda b, h, i: (b, 0, FOX_HEADS + h)),
                  pl.BlockSpec((1, seq, d), lambda b, h, i: (b, 0, 2 * FOX_HEADS + h)),
                  pl.BlockSpec((1, tq, V7X_LANES), lambda b, h, i: (b, i, 0)),
                  pl.BlockSpec((1, 1, 1, seq), lambda b, h, i: (b, h, 0, 0)),
                  pl.BlockSpec((1, 1, d), lambda b, h, i: (h, 0, 0))],
        out_specs=pl.BlockSpec((1, tq, d), lambda b, h, i: (b, i, h)),
        out_shape=jax.ShapeDtypeStruct((bsz, seq, MIX_WIDTH), BF16),
        scratch_shapes=[pltpu.VMEM((tq, V7X_LANES), F32), pltpu.VMEM((tq, V7X_LANES), F32),
                        pltpu.VMEM((tq, d), F32)],
        compiler_params=pltpu.CompilerParams(
            dimension_semantics=("arbitrary", "arbitrary", "arbitrary"),
            vmem_limit_bytes=_vmem_limit(blk, 8 * _nbytes((tq, tk), F32))),
        name="fox_attention",
    )(proj, proj, proj, gates, c_rows, norm_g)


def _gla_tables(chunk):
    idx = np.arange(chunk)
    mats = [np.tril(np.ones((chunk, chunk), np.float32))]
    masks = []
    half = chunk // 2
    while half >= 1:
        blk = idx // (2 * half)
        ref = blk * 2 * half + half - 1
        mats.append((idx[None, :] <= ref[:, None]).astype(np.float32))
        second = (idx // half) % 2 == 1
        masks.append(((blk[:, None] == blk[None, :]) & second[:, None] & ~second[None, :]).astype(np.float32))
        half //= 2
    return np.concatenate(mats, axis=0), np.stack(masks, axis=0)


def _gla_kernel(q_ref, k_ref, v_ref, gg_ref, gates_ref, wup_ref, bgk_ref, gn_ref, seltri_ref, mask_ref,
                mixed_hbm_ref, o_ref, state_ref, la_ref, *, rows, chunk, scale):
    del mixed_hbm_ref
    t = pl.program_id(2)

    @pl.when(t == 0)
    def _():
        state_ref[...] = jnp.zeros_like(state_ref)

    pre = jnp.dot(gates_ref[0], wup_ref[...], preferred_element_type=F32,
                  precision=lax.Precision.HIGHEST) + bgk_ref[...]
    la_ref[...] = _log_sigmoid(pre) * (1.0 / GLA_GATE_NORMALIZER)

    n_levels = mask_ref.shape[0]
    ones_cols = jnp.ones((chunk, V7X_LANES), F32)
    eye = (lax.broadcasted_iota(jnp.int32, (chunk, chunk), 0)
           == lax.broadcasted_iota(jnp.int32, (chunk, chunk), 1))
    dv_reps = GLA_VAL_DIM // V7X_LANES

    def chunk_body(c, carry):
        r0 = pl.multiple_of(c * chunk, chunk)
        g = la_ref[pl.ds(r0, chunk), :]
        b_all = jnp.dot(seltri_ref[...], g, preferred_element_type=F32,
                        precision=lax.Precision.HIGHEST)
        b = b_all[0:chunk]
        q = q_ref[0, pl.ds(r0, chunk), :].astype(F32) * scale
        k = k_ref[0, pl.ds(r0, chunk), :].astype(F32)
        v = v_ref[0, pl.ds(r0, chunk), :]

        state = state_ref[...]
        o = jnp.dot((q * jnp.exp(b)).astype(BF16), state.astype(BF16), preferred_element_type=F32)

        scores = jnp.zeros((chunk, chunk), F32)
        for lvl in range(n_levels):
            ref = b_all[(lvl + 1) * chunk:(lvl + 2) * chunk]
            qt = (q * jnp.exp(jnp.minimum(b - ref, 0.0))).astype(BF16)
            kt = (k * jnp.exp(jnp.minimum(ref - b, 0.0))).astype(BF16)
            scores = scores + mask_ref[lvl] * lax.dot_general(qt, kt, _NT, preferred_element_type=F32)
        scores = jnp.where(eye, jnp.sum(q * k, axis=-1, keepdims=True), scores)
        o = o + jnp.dot(scores.astype(BF16), v, preferred_element_type=F32)

        b_last = b[chunk - 1:chunk]
        k_dec = (k * jnp.exp(b_last - b)).astype(BF16)
        dec_col = jnp.exp(lax.dot_general(g, ones_cols, _TN, preferred_element_type=F32,
                                          precision=lax.Precision.HIGHEST))
        state_ref[...] = (state * pltpu.repeat(dec_col, dv_reps, axis=1)
                          + lax.dot_general(k_dec, v, _TN, preferred_element_type=F32))

        o = o * lax.rsqrt(jnp.mean(o * o, axis=-1, keepdims=True) + EPS) * gn_ref[...]
        gate = _silu(gg_ref[0, pl.ds(r0, chunk), :].astype(F32))
        o_ref[0, pl.ds(r0, chunk), :] = (o * gate).astype(o_ref.dtype)
        return carry

    lax.fori_loop(0, rows // chunk, chunk_body, 0)


def _gla(proj, gates, w_up_pad, b_gk, norm_g, mixed, *, rows=GLA_ROWS, chunk=GLA_CHUNK):
    bsz, seq, _ = proj.shape
    dk, dv = GLA_KEY_DIM, GLA_VAL_DIM
    assert seq % rows == 0 and rows % chunk == 0
    q_blk0 = 3 * FOX_WIDTH // dk
    k_blk0 = q_blk0 + GLA_HEADS
    v_blk0 = (3 * FOX_WIDTH + 2 * GLA_KEY_WIDTH) // dv
    g_blk0 = v_blk0 + GLA_HEADS
    o_blk0 = FOX_WIDTH // dv
    sel_tri, masks = _gla_tables(chunk)
    blk = (2 * _nbytes((rows, dk), BF16) + 3 * _nbytes((rows, dv), BF16) + _nbytes((rows, V7X_LANES), F32)
           + _nbytes((V7X_LANES, dk), F32) + _nbytes(sel_tri.shape, F32) + _nbytes(masks.shape, F32))
    scratch = _nbytes((dk, dv), F32) + _nbytes((rows, dk), F32)
    return pl.pallas_call(
        functools.partial(_gla_kernel, rows=rows, chunk=chunk, scale=dk ** -0.5),
        grid=(bsz, GLA_HEADS, seq // rows),
        in_specs=[pl.BlockSpec((1, rows, dk), lambda b, h, t: (b, t, q_blk0 + h)),
                  pl.BlockSpec((1, rows, dk), lambda b, h, t: (b, t, k_blk0 + h)),
                  pl.BlockSpec((1, rows, dv), lambda b, h, t: (b, t, v_blk0 + h)),
                  pl.BlockSpec((1, rows, dv), lambda b, h, t: (b, t, g_blk0 + h)),
                  pl.BlockSpec((1, rows, V7X_LANES), lambda b, h, t: (b, t, 0)),
                  pl.BlockSpec((V7X_LANES, dk), lambda b, h, t: (0, h)),
                  pl.BlockSpec((1, dk), lambda b, h, t: (0, h)),
                  pl.BlockSpec((1, dv), lambda b, h, t: (0, 0)),
                  pl.BlockSpec(sel_tri.shape, lambda b, h, t: (0, 0)),
                  pl.BlockSpec(masks.shape, lambda b, h, t: (0, 0, 0)),
                  pl.BlockSpec(memory_space=pl.ANY)],
        out_specs=pl.BlockSpec((1, rows, dv), lambda b, h, t: (b, t, o_blk0 + h)),
        out_shape=jax.ShapeDtypeStruct(mixed.shape, mixed.dtype),
        scratch_shapes=[pltpu.VMEM((dk, dv), F32), pltpu.VMEM((rows, dk), F32)],
        input_output_aliases={10: 0},
        compiler_params=pltpu.CompilerParams(
            dimension_semantics=("arbitrary", "arbitrary", "arbitrary"),
            vmem_limit_bytes=_vmem_limit(blk, scratch + (8 << 20))),
        name="gla",
    )(proj, proj, proj, proj, gates, w_up_pad, b_gk.reshape(1, -1), norm_g.reshape(1, -1),
      jnp.asarray(sel_tri), jnp.asarray(masks), mixed)


def _prep_layer(w_in, b_f, w_gk_up, w_out, w_gate, w_up, w_down):
    d_model = w_in.shape[0]
    f_lo = 3 * FOX_WIDTH
    g_lo = f_lo + FOX_HEADS
    low_lo = g_lo + 2 * GLA_KEY_WIDTH + 2 * GLA_WIDTH
    w_main = jnp.concatenate([w_in[:, :f_lo], w_in[:, g_lo:low_lo]], axis=1).astype(BF16)
    pad = V7X_LANES - FOX_HEADS - GLA_GATE_RANK
    w_small = jnp.concatenate([w_in[:, f_lo:g_lo], w_in[:, low_lo:], jnp.zeros((d_model, pad), F32)],
                              axis=1).astype(BF16)
    gate_bias = jnp.concatenate([b_f, jnp.zeros((V7X_LANES - FOX_HEADS,), F32)]).reshape(1, V7X_LANES)
    w_up_pad = jnp.zeros((V7X_LANES, GLA_KEY_WIDTH), F32).at[FOX_HEADS:FOX_HEADS + GLA_GATE_RANK].set(w_gk_up)

    d_ff = w_gate.shape[1]
    ff_pad = -d_ff % FFN_PAD
    n_tiles = (d_ff + ff_pad) // FFN_TN
    wg = jnp.pad(w_gate, ((0, 0), (0, ff_pad))).astype(BF16).reshape(d_model, n_tiles, 1, FFN_TN)
    wu = jnp.pad(w_up, ((0, 0), (0, ff_pad))).astype(BF16).reshape(d_model, n_tiles, 1, FFN_TN)
    w_gu = jnp.concatenate([wg, wu], axis=2).reshape(d_model, 2 * (d_ff + ff_pad))
    w_dn = jnp.pad(w_down, ((0, ff_pad), (0, 0))).astype(BF16)
    return w_main, w_small, gate_bias, w_up_pad, w_out.astype(BF16), w_gu, w_dn


def kernel(x, ln_in_g, ln_in_b, w_in, b_f, w_gk_up, b_gk, fox_norm_g, gla_norm_g, w_out, ln1_g, ln1_b,
           w_gate, w_up, w_down, ln2_g, ln2_b):
    bsz, seq, d_model = x.shape
    depth = w_in.shape[0]
    alpha = (2 * depth) ** 0.25
    rows = bsz * seq

    x32, x16 = _layer_norm(x.reshape(rows, d_model), None, ln_in_g, ln_in_b)
    for l in range(depth):
        w_main, w_small, gate_bias, w_up_pad, w_o, w_gu, w_dn = _prep_layer(
            w_in[l], b_f[l], w_gk_up[l], w_out[l], w_gate[l], w_up[l], w_down[l])

        proj = _matmul(x16, w_main, BF16).reshape(bsz, seq, -1)
        gates = _gates(x16, w_small, gate_bias, seq).reshape(bsz, seq, V7X_LANES)
        c_rows = gates[:, :, :FOX_HEADS].transpose(0, 2, 1).reshape(bsz, FOX_HEADS, 1, seq)

        mixed = _fox(proj, gates, c_rows, fox_norm_g[l].reshape(FOX_HEADS, 1, FOX_HEAD_DIM))
        mixed = _gla(proj, gates, w_up_pad, b_gk[l], gla_norm_g[l], mixed)

        mix = _matmul(mixed.reshape(rows, MIX_WIDTH), w_o, F32)
        x32, x16 = _layer_norm(x32, mix, ln1_g[l], ln1_b[l], alpha=alpha)

        hidden = _gate_up(x16, w_gu)
        ffn = _matmul(hidden, w_dn, F32, tk=DOWN_TK)
        last = l == depth - 1
        outs = _layer_norm(x32, ffn, ln2_g[l], ln2_b[l], alpha=alpha, want_bf16=not last)
        if last:
            x32, = outs
        else:
            x32, x16 = outs
    return x32.reshape(bsz, seq, d_model)
```

```python
import functools

import numpy as np
import jax
import jax.numpy as jnp
from jax import lax
from jax.experimental import pallas as pl
from jax.experimental.pallas import tpu as pltpu

F32 = jnp.float32
BF16 = jnp.bfloat16

V7X_VMEM_BYTES = 64 * 2**20
V7X_LANES = 128
V7X_MXU_DIM = 256

FOX_HEADS = 16
FOX_HEAD_DIM = 128
FOX_WIDTH = FOX_HEADS * FOX_HEAD_DIM
GLA_HEADS = 4
GLA_KEY_DIM = 256
GLA_VAL_DIM = 512
GLA_KEY_WIDTH = GLA_HEADS * GLA_KEY_DIM
GLA_WIDTH = GLA_HEADS * GLA_VAL_DIM
GLA_GATE_RANK = 16
GLA_GATE_NORMALIZER = 16.0
MIX_WIDTH = FOX_WIDTH + GLA_WIDTH
EPS = 1e-5

LN_ROWS = 256
MM_TM = 1024
MM_TN = 512
FFN_TN = 256
DOWN_TN = 1024
DOWN_TK = 2816
GATE_ROWS = 512
FOX_TQ = 512
FOX_TK = 512
GLA_ROWS = 512
GLA_CHUNK = 64
NEG_BIG = -1e30

_NT = (((1,), (1,)), ((), ()))
_TN = (((0,), (0,)), ((), ()))


def _vmem_limit(block_bytes, extra_bytes):
    need = 2 * block_bytes + extra_bytes + (4 << 20)
    return int(min(need, V7X_VMEM_BYTES - (6 << 20)))


def _nbytes(shape, dtype):
    return int(np.prod(shape)) * jnp.dtype(dtype).itemsize


def _log_sigmoid(x):
    return jnp.minimum(x, 0.0) - jnp.log1p(jnp.exp(-jnp.abs(x)))


def _silu(x):
    return x * (1.0 / (1.0 + jnp.exp(-x)))


def _ln_math(t, g, b):
    mu = jnp.mean(t, axis=-1, keepdims=True)
    tc = t - mu
    var = jnp.mean(tc * tc, axis=-1, keepdims=True)
    return tc * lax.rsqrt(var + EPS) * g + b


def _ln_kernel(*refs, alpha, has_res, want_f32, want_bf16):
    if has_res:
        x_ref, y_ref, g_ref, b_ref = refs[:4]
        outs = refs[4:]
        t = alpha * x_ref[...] + y_ref[...]
    else:
        x_ref, g_ref, b_ref = refs[:3]
        outs = refs[3:]
        t = x_ref[...]
    y = _ln_math(t, g_ref[...], b_ref[...])
    i = 0
    if want_f32:
        outs[i][...] = y
        i += 1
    if want_bf16:
        outs[i][...] = y.astype(BF16)


def _layer_norm(x, res, g, b, *, alpha=1.0, want_f32=True, want_bf16=True):
    rows, d = x.shape
    assert rows % LN_ROWS == 0
    row_spec = pl.BlockSpec((LN_ROWS, d), lambda i: (i, 0))
    vec_spec = pl.BlockSpec((1, d), lambda i: (0, 0))
    ins = [x] + ([res] if res is not None else []) + [g.reshape(1, d), b.reshape(1, d)]
    in_specs = [row_spec] * (len(ins) - 2) + [vec_spec, vec_spec]
    out_shape, out_specs = [], []
    if want_f32:
        out_shape.append(jax.ShapeDtypeStruct((rows, d), F32))
        out_specs.append(row_spec)
    if want_bf16:
        out_shape.append(jax.ShapeDtypeStruct((rows, d), BF16))
        out_specs.append(row_spec)
    blk = _nbytes((LN_ROWS, d), F32) * (len(ins) - 2 + len(out_shape))
    return pl.pallas_call(
        functools.partial(_ln_kernel, alpha=alpha, has_res=res is not None,
                          want_f32=want_f32, want_bf16=want_bf16),
        grid=(rows // LN_ROWS,),
        in_specs=in_specs, out_specs=out_specs, out_shape=out_shape,
        compiler_params=pltpu.CompilerParams(
            dimension_semantics=("arbitrary",),
            vmem_limit_bytes=_vmem_limit(blk, 4 * _nbytes((LN_ROWS, d), F32))),
        name="layer_norm",
    )(*ins)


def _mm_kernel(x_ref, w_ref, o_ref, *scratch, cast):
    if cast:
        wb_ref, = scratch

        @pl.when(pl.program_id(1) == 0)
        def _():
            wb_ref[...] = w_ref[...].astype(BF16)
        w = wb_ref[...]
    else:
        w = w_ref[...]
    o_ref[...] = jnp.dot(x_ref[...], w, preferred_element_type=F32).astype(o_ref.dtype)


def _matmul(x, w, layer, n_cols, out_dtype, *, tm=MM_TM, tn=MM_TN):
    m, kdim = x.shape
    assert m % tm == 0 and n_cols % tn == 0 and w.shape[1] == kdim
    cast = w.dtype != BF16
    blk = _nbytes((tm, kdim), x.dtype) + _nbytes((kdim, tn), w.dtype) + _nbytes((tm, tn), out_dtype)
    extra = _nbytes((kdim, tn), BF16) + 2 * _nbytes((tm, tn), F32)
    return pl.pallas_call(
        functools.partial(_mm_kernel, cast=cast),
        grid=(n_cols // tn, m // tm),
        in_specs=[pl.BlockSpec((tm, kdim), lambda j, i: (i, 0)),
                  pl.BlockSpec((None, kdim, tn), lambda j, i: (layer, 0, j))],
        out_specs=pl.BlockSpec((tm, tn), lambda j, i: (i, j)),
        out_shape=jax.ShapeDtypeStruct((m, n_cols), out_dtype),
        scratch_shapes=[pltpu.VMEM((kdim, tn), BF16)] if cast else [],
        compiler_params=pltpu.CompilerParams(
            dimension_semantics=("arbitrary", "arbitrary"),
            vmem_limit_bytes=_vmem_limit(blk, extra)),
        name="matmul",
    )(x, w)


def _gate_up_kernel(x_ref, wg_ref, wu_ref, o_ref, wb_ref, *, tn, d_ff):
    @pl.when(pl.program_id(1) == 0)
    def _():
        wb_ref[:, :tn] = wg_ref[...].astype(BF16)
        wb_ref[:, tn:] = wu_ref[...].astype(BF16)

    z = jnp.dot(x_ref[...], wb_ref[...], preferred_element_type=F32)
    h = _silu(z[:, :tn]) * z[:, tn:]
    col = pl.program_id(0) * tn + lax.broadcasted_iota(jnp.int32, h.shape, 1)
    o_ref[...] = jnp.where(col < d_ff, h, 0.0).astype(o_ref.dtype)


def _gate_up(x, w_gate, w_up, layer, *, tm=MM_TM, tn=FFN_TN):
    m, kdim = x.shape
    d_ff = w_gate.shape[2]
    n_out = pl.cdiv(d_ff, DOWN_TK) * DOWN_TK
    assert m % tm == 0 and n_out % tn == 0
    last_w_blk = pl.cdiv(d_ff, tn) - 1
    w_spec = pl.BlockSpec((None, kdim, tn), lambda j, i: (layer, 0, jnp.minimum(j, last_w_blk)))
    blk = _nbytes((tm, kdim), x.dtype) + 2 * _nbytes((kdim, tn), F32) + _nbytes((tm, tn), BF16)
    extra = _nbytes((kdim, 2 * tn), BF16) + 3 * _nbytes((tm, 2 * tn), F32)
    return pl.pallas_call(
        functools.partial(_gate_up_kernel, tn=tn, d_ff=d_ff),
        grid=(n_out // tn, m // tm),
        in_specs=[pl.BlockSpec((tm, kdim), lambda j, i: (i, 0)), w_spec, w_spec],
        out_specs=pl.BlockSpec((tm, tn), lambda j, i: (i, j)),
        out_shape=jax.ShapeDtypeStruct((m, n_out), BF16),
        scratch_shapes=[pltpu.VMEM((kdim, 2 * tn), BF16)],
        compiler_params=pltpu.CompilerParams(
            dimension_semantics=("arbitrary", "arbitrary"),
            vmem_limit_bytes=_vmem_limit(blk, extra)),
        name="gate_up",
    )(x, w_gate, w_up)


def _down_kernel(h_ref, w_ref, o_ref, acc_ref, *, nk):
    k = pl.program_id(2)
    part = jnp.dot(h_ref[...], w_ref[...], preferred_element_type=F32)
    if nk == 1:
        o_ref[...] = part
        return

    @pl.when(k == 0)
    def _():
        acc_ref[...] = part

    @pl.when(jnp.logical_and(k > 0, k < nk - 1))
    def _():
        acc_ref[...] += part

    @pl.when(k == nk - 1)
    def _():
        o_ref[...] = acc_ref[...] + part


def _down(h, w_down, *, tm=MM_TM, tn=DOWN_TN, tk=DOWN_TK):
    m, kp = h.shape
    n = w_down.shape[1]
    assert m % tm == 0 and n % tn == 0 and kp % tk == 0 and w_down.shape[0] == kp
    nk = kp // tk
    blk = _nbytes((tm, tk), h.dtype) + _nbytes((tk, tn), w_down.dtype) + _nbytes((tm, tn), F32)
    extra = 3 * _nbytes((tm, tn), F32)
    return pl.pallas_call(
        functools.partial(_down_kernel, nk=nk),
        grid=(m // tm, n // tn, nk),
        in_specs=[pl.BlockSpec((tm, tk), lambda i, j, k: (i, k)),
                  pl.BlockSpec((tk, tn), lambda i, j, k: (k, j))],
        out_specs=pl.BlockSpec((tm, tn), lambda i, j, k: (i, j)),
        out_shape=jax.ShapeDtypeStruct((m, n), F32),
        scratch_shapes=[pltpu.VMEM((tm, tn), F32)],
        compiler_params=pltpu.CompilerParams(
            dimension_semantics=("arbitrary", "arbitrary", "arbitrary"),
            vmem_limit_bytes=_vmem_limit(blk, extra)),
        name="down_proj",
    )(h, w_down)


def _gates_kernel(x_ref, w_ref, bias_ref, tri_ref, o_ref, carry_ref, *, steps_per_seq):
    i = pl.program_id(0)

    @pl.when(i % steps_per_seq == 0)
    def _():
        carry_ref[...] = jnp.zeros_like(carry_ref)

    z = jnp.dot(x_ref[...], w_ref[...], preferred_element_type=F32)
    log_f = _log_sigmoid(z + bias_ref[...])
    c = jnp.dot(tri_ref[...], log_f, preferred_element_type=F32,
                precision=lax.Precision.HIGHEST) + carry_ref[0:1, :]
    rows = c.shape[0]
    carry_ref[...] = jnp.broadcast_to(c[rows - 1:rows, :], carry_ref.shape)
    lane = lax.broadcasted_iota(jnp.int32, c.shape, 1)
    o_ref[...] = jnp.where(lane < FOX_HEADS, c, z)


def _gates(x, w_small, bias, seq):
    m, kdim = x.shape
    tri = jnp.asarray(np.tril(np.ones((GATE_ROWS, GATE_ROWS), np.float32)))
    blk = (_nbytes((GATE_ROWS, kdim), x.dtype) + _nbytes((kdim, V7X_LANES), w_small.dtype)
           + _nbytes((GATE_ROWS, GATE_ROWS), F32) + _nbytes((GATE_ROWS, V7X_LANES), F32))
    return pl.pallas_call(
        functools.partial(_gates_kernel, steps_per_seq=seq // GATE_ROWS),
        grid=(m // GATE_ROWS,),
        in_specs=[pl.BlockSpec((GATE_ROWS, kdim), lambda i: (i, 0)),
                  pl.BlockSpec((kdim, V7X_LANES), lambda i: (0, 0)),
                  pl.BlockSpec((1, V7X_LANES), lambda i: (0, 0)),
                  pl.BlockSpec((GATE_ROWS, GATE_ROWS), lambda i: (0, 0))],
        out_specs=pl.BlockSpec((GATE_ROWS, V7X_LANES), lambda i: (i, 0)),
        out_shape=jax.ShapeDtypeStruct((m, V7X_LANES), F32),
        scratch_shapes=[pltpu.VMEM((8, V7X_LANES), F32)],
        compiler_params=pltpu.CompilerParams(
            dimension_semantics=("arbitrary",),
            vmem_limit_bytes=_vmem_limit(blk, 8 * _nbytes((GATE_ROWS, V7X_LANES), F32))),
        name="gates",
    )(x, w_small, bias, tri)


def _fox_kernel(q_ref, k_ref, v_ref, ccol_ref, crow_ref, g_ref, o_ref, m_ref, l_ref, acc_ref,
                *, tq, tk, scale):
    h = pl.program_id(1)
    qi = pl.program_id(2)
    q = q_ref[0]
    sel = (lax.broadcasted_iota(jnp.int32, (V7X_LANES, V7X_LANES), 0) == h).astype(F32)
    c_t = jnp.dot(ccol_ref[0], sel, preferred_element_type=F32,
                  precision=lax.Precision.HIGHEST)

    m_ref[...] = jnp.full_like(m_ref, NEG_BIG)
    l_ref[...] = jnp.zeros_like(l_ref)
    acc_ref[...] = jnp.zeros_like(acc_ref)
    reps = tk // V7X_LANES

    def step(j, masked):
        start = pl.multiple_of(j * tk, tk)
        kj = k_ref[0, pl.ds(start, tk), :]
        vj = v_ref[0, pl.ds(start, tk), :]
        a = lax.dot_general(q, kj, _NT, preferred_element_type=F32) * scale
        a = a - crow_ref[0, 0, :, pl.ds(start, tk)]
        if masked:
            row = lax.broadcasted_iota(jnp.int32, (tq, tk), 0)
            col = lax.broadcasted_iota(jnp.int32, (tq, tk), 1)
            a = jnp.where(row >= col, a, NEG_BIG)
        m_prev = m_ref[...]
        m_new = jnp.maximum(m_prev, jnp.max(a, axis=-1, keepdims=True) + c_t)
        p = jnp.exp(a - pltpu.repeat(m_new - c_t, reps, axis=1))
        alpha = jnp.exp(m_prev - m_new)
        l_ref[...] = alpha * l_ref[...] + jnp.sum(p, axis=-1, keepdims=True)
        acc_ref[...] = alpha * acc_ref[...] + jnp.dot(p.astype(BF16), vj, preferred_element_type=F32)
        m_ref[...] = m_new

    def body(j, carry):
        step(j, False)
        return carry

    lax.fori_loop(0, qi, body, 0)
    step(qi, True)

    o = acc_ref[...] / l_ref[...]
    o = o * lax.rsqrt(jnp.mean(o * o, axis=-1, keepdims=True) + EPS) * g_ref[0]
    o_ref[0] = o.astype(o_ref.dtype)


def _fox(proj, gates, c_rows, norm_g, *, tq=FOX_TQ, tk=FOX_TK):
    bsz, seq, _ = proj.shape
    d = FOX_HEAD_DIM
    assert tq == tk and seq % tq == 0
    blk = (_nbytes((tq, d), BF16) * 2 + 2 * _nbytes((seq, d), BF16) + _nbytes((tq, V7X_LANES), F32)
           + _nbytes((8, seq), F32) + _nbytes((8, d), F32))
    return pl.pallas_call(
        functools.partial(_fox_kernel, tq=tq, tk=tk, scale=d ** -0.5),
        grid=(bsz, FOX_HEADS, seq // tq),
        in_specs=[pl.BlockSpec((1, tq, d), lambda b, h, i: (b, i, h)),
                  pl.BlockSpec((1, seq, d), lambda b, h, i: (b, 0, FOX_HEADS + h)),
                  pl.BlockSpec((1, seq, d), lambda b, h, i: (b, 0, 2 * FOX_HEADS + h)),
                  pl.BlockSpec((1, tq, V7X_LANES), lambda b, h, i: (b, i, 0)),
                  pl.BlockSpec((1, 1, 1, seq), lambda b, h, i: (b, h, 0, 0)),
                  pl.BlockSpec((1, 1, d), lambda b, h, i: (h, 0, 0))],
        out_specs=pl.BlockSpec((1, tq, d), lambda b, h, i: (b, i, h)),
        out_shape=jax.ShapeDtypeStruct((bsz, seq, MIX_WIDTH), BF16),
        scratch_shapes=[pltpu.VMEM((tq, V7X_LANES), F32), pltpu.VMEM((tq, V7X_LANES), F32),
                        pltpu.VMEM((tq, d), F32)],
        compiler_params=pltpu.CompilerParams(
            dimension_semantics=("arbitrary", "arbitrary", "arbitrary"),
            vmem_limit_bytes=_vmem_limit(blk, 8 * _nbytes((tq, tk), F32))),
        name="fox_attention",
    )(proj, proj, proj, gates, c_rows, norm_g)


def _gla_tables(chunk):
    idx = np.arange(chunk)
    mats = [np.tril(np.ones((chunk, chunk), np.float32))]
    masks = []
    half = chunk // 2
    while half >= 1:
        blk = idx // (2 * half)
        ref = blk * 2 * half + half - 1
        mats.append((idx[None, :] <= ref[:, None]).astype(np.float32))
        second = (idx // half) % 2 == 1
        masks.append(((blk[:, None] == blk[None, :]) & second[:, None] & ~second[None, :]).astype(np.float32))
        half //= 2
    return np.concatenate(mats, axis=0), np.stack(masks, axis=0)


def _gla_kernel(q_ref, k_ref, v_ref, gg_ref, gates_ref, wup_ref, bgk_ref, gn_ref, seltri_ref, mask_ref,
                mixed_hbm_ref, o_ref, state_ref, la_ref, *, rows, chunk, scale):
    del mixed_hbm_ref
    t = pl.program_id(2)

    @pl.when(t == 0)
    def _():
        state_ref[...] = jnp.zeros_like(state_ref)

    pre = jnp.dot(gates_ref[0], wup_ref[...], preferred_element_type=F32,
                  precision=lax.Precision.HIGHEST) + bgk_ref[...]
    la_ref[...] = _log_sigmoid(pre) * (1.0 / GLA_GATE_NORMALIZER)

    n_levels = mask_ref.shape[0]
    ones_cols = jnp.ones((chunk, V7X_LANES), F32)
    eye = (lax.broadcasted_iota(jnp.int32, (chunk, chunk), 0)
           == lax.broadcasted_iota(jnp.int32, (chunk, chunk), 1))
    dv_reps = GLA_VAL_DIM // V7X_LANES

    def chunk_body(c, carry):
        r0 = pl.multiple_of(c * chunk, chunk)
        g = la_ref[pl.ds(r0, chunk), :]
        b_all = jnp.dot(seltri_ref[...], g, preferred_element_type=F32,
                        precision=lax.Precision.HIGHEST)
        b = b_all[0:chunk]
        q = q_ref[0, pl.ds(r0, chunk), :].astype(F32) * scale
        k = k_ref[0, pl.ds(r0, chunk), :].astype(F32)
        v = v_ref[0, pl.ds(r0, chunk), :]

        state = state_ref[...]
        o = jnp.dot((q * jnp.exp(b)).astype(BF16), state.astype(BF16), preferred_element_type=F32)

        scores = jnp.zeros((chunk, chunk), F32)
        for lvl in range(n_levels):
            ref = b_all[(lvl + 1) * chunk:(lvl + 2) * chunk]
            qt = (q * jnp.exp(jnp.minimum(b - ref, 0.0))).astype(BF16)
            kt = (k * jnp.exp(jnp.minimum(ref - b, 0.0))).astype(BF16)
            scores = scores + mask_ref[lvl] * lax.dot_general(qt, kt, _NT, preferred_element_type=F32)
        scores = jnp.where(eye, jnp.sum(q * k, axis=-1, keepdims=True), scores)
        o = o + jnp.dot(scores.astype(BF16), v, preferred_element_type=F32)

        b_last = b[chunk - 1:chunk]
        k_dec = (k * jnp.exp(b_last - b)).astype(BF16)
        dec_col = jnp.exp(lax.dot_general(g, ones_cols, _TN, preferred_element_type=F32,
                                          precision=lax.Precision.HIGHEST))
        state_ref[...] = (state * pltpu.repeat(dec_col, dv_reps, axis=1)
                          + lax.dot_general(k_dec, v, _TN, preferred_element_type=F32))

        o = o * lax.rsqrt(jnp.mean(o * o, axis=-1, keepdims=True) + EPS) * gn_ref[...]
        gate = _silu(gg_ref[0, pl.ds(r0, chunk), :].astype(F32))
        o_ref[0, pl.ds(r0, chunk), :] = (o * gate).astype(o_ref.dtype)
        return carry

    lax.fori_loop(0, rows // chunk, chunk_body, 0)


def _gla(proj, gates, w_up_pad, b_gk, norm_g, mixed, *, rows=GLA_ROWS, chunk=GLA_CHUNK):
    bsz, seq, _ = proj.shape
    dk, dv = GLA_KEY_DIM, GLA_VAL_DIM
    assert seq % rows == 0 and rows % chunk == 0
    q_blk0 = 0
    k_blk0 = q_blk0 + GLA_HEADS
    v_blk0 = 2 * GLA_KEY_WIDTH // dv
    g_blk0 = v_blk0 + GLA_HEADS
    o_blk0 = FOX_WIDTH // dv
    sel_tri, masks = _gla_tables(chunk)
    blk = (2 * _nbytes((rows, dk), BF16) + 3 * _nbytes((rows, dv), BF16) + _nbytes((rows, V7X_LANES), F32)
           + _nbytes((V7X_LANES, dk), F32) + _nbytes(sel_tri.shape, F32) + _nbytes(masks.shape, F32))
    scratch = _nbytes((dk, dv), F32) + _nbytes((rows, dk), F32)
    return pl.pallas_call(
        functools.partial(_gla_kernel, rows=rows, chunk=chunk, scale=dk ** -0.5),
        grid=(bsz, GLA_HEADS, seq // rows),
        in_specs=[pl.BlockSpec((1, rows, dk), lambda b, h, t: (b, t, q_blk0 + h)),
                  pl.BlockSpec((1, rows, dk), lambda b, h, t: (b, t, k_blk0 + h)),
                  pl.BlockSpec((1, rows, dv), lambda b, h, t: (b, t, v_blk0 + h)),
                  pl.BlockSpec((1, rows, dv), lambda b, h, t: (b, t, g_blk0 + h)),
                  pl.BlockSpec((1, rows, V7X_LANES), lambda b, h, t: (b, t, 0)),
                  pl.BlockSpec((V7X_LANES, dk), lambda b, h, t: (0, h)),
                  pl.BlockSpec((1, dk), lambda b, h, t: (0, h)),
                  pl.BlockSpec((1, dv), lambda b, h, t: (0, 0)),
                  pl.BlockSpec(sel_tri.shape, lambda b, h, t: (0, 0)),
                  pl.BlockSpec(masks.shape, lambda b, h, t: (0, 0, 0)),
                  pl.BlockSpec(memory_space=pl.ANY)],
        out_specs=pl.BlockSpec((1, rows, dv), lambda b, h, t: (b, t, o_blk0 + h)),
        out_shape=jax.ShapeDtypeStruct(mixed.shape, mixed.dtype),
        scratch_shapes=[pltpu.VMEM((dk, dv), F32), pltpu.VMEM((rows, dk), F32)],
        input_output_aliases={10: 0},
        compiler_params=pltpu.CompilerParams(
            dimension_semantics=("arbitrary", "arbitrary", "arbitrary"),
            vmem_limit_bytes=_vmem_limit(blk, scratch + (8 << 20))),
        name="gla",
    )(proj, proj, proj, proj, gates, w_up_pad, b_gk.reshape(1, -1), norm_g.reshape(1, -1),
      jnp.asarray(sel_tri), jnp.asarray(masks), mixed)


def _prep_layer(w_in, b_f, w_gk_up):
    d_model = w_in.shape[0]
    f_lo = 3 * FOX_WIDTH
    g_lo = f_lo + FOX_HEADS
    low_lo = g_lo + 2 * GLA_KEY_WIDTH + 2 * GLA_WIDTH
    w_gla = w_in[:, g_lo:low_lo].astype(BF16)[None]
    pad = V7X_LANES - FOX_HEADS - GLA_GATE_RANK
    w_small = jnp.concatenate([w_in[:, f_lo:g_lo], w_in[:, low_lo:], jnp.zeros((d_model, pad), F32)],
                              axis=1).astype(BF16)
    gate_bias = jnp.concatenate([b_f, jnp.zeros((V7X_LANES - FOX_HEADS,), F32)]).reshape(1, V7X_LANES)
    w_up_pad = jnp.zeros((V7X_LANES, GLA_KEY_WIDTH), F32).at[FOX_HEADS:FOX_HEADS + GLA_GATE_RANK].set(w_gk_up)
    return w_gla, w_small, gate_bias, w_up_pad


def kernel(x, ln_in_g, ln_in_b, w_in, b_f, w_gk_up, b_gk, fox_norm_g, gla_norm_g, w_out, ln1_g, ln1_b,
           w_gate, w_up, w_down, ln2_g, ln2_b):
    bsz, seq, d_model = x.shape
    depth = w_in.shape[0]
    alpha = (2 * depth) ** 0.25
    rows = bsz * seq

    x32, x16 = _layer_norm(x.reshape(rows, d_model), None, ln_in_g, ln_in_b)
    for l in range(depth):
        w_gla, w_small, gate_bias, w_up_pad = _prep_layer(w_in[l], b_f[l], w_gk_up[l])

        proj_fox = _matmul(x16, w_in, l, 3 * FOX_WIDTH, BF16).reshape(bsz, seq, -1)
        proj_gla = _matmul(x16, w_gla, 0, w_gla.shape[2], BF16).reshape(bsz, seq, -1)
        gates = _gates(x16, w_small, gate_bias, seq).reshape(bsz, seq, V7X_LANES)
        c_rows = gates[:, :, :FOX_HEADS].transpose(0, 2, 1).reshape(bsz, FOX_HEADS, 1, seq)

        mixed = _fox(proj_fox, gates, c_rows, fox_norm_g[l].reshape(FOX_HEADS, 1, FOX_HEAD_DIM))
        mixed = _gla(proj_gla, gates, w_up_pad, b_gk[l], gla_norm_g[l], mixed)

        mix = _matmul(mixed.reshape(rows, MIX_WIDTH), w_out, l, d_model, F32)
        x32, x16 = _layer_norm(x32, mix, ln1_g[l], ln1_b[l], alpha=alpha)

        hidden = _gate_up(x16, w_gate, w_up, l)
        w_dn = jnp.pad(w_down[l], ((0, hidden.shape[1] - w_down.shape[1]), (0, 0))).astype(BF16)
        ffn = _down(hidden, w_dn)
        last = l == depth - 1
        outs = _layer_norm(x32, ffn, ln2_g[l], ln2_b[l], alpha=alpha, want_bf16=not last)
        if last:
            x32, = outs
        else:
            x32, x16 = outs
    return x32.reshape(bsz, seq, d_model)
```

```python
import functools

import numpy as np
import jax
import jax.numpy as jnp
from jax import lax
from jax.experimental import pallas as pl
from jax.experimental.pallas import tpu as pltpu

F32 = jnp.float32
BF16 = jnp.bfloat16

V7X_VMEM_BYTES = 64 * 2**20
V7X_LANES = 128
V7X_MXU_DIM = 256

FOX_HEADS = 16
FOX_HEAD_DIM = 128
FOX_WIDTH = FOX_HEADS * FOX_HEAD_DIM
GLA_HEADS = 4
GLA_KEY_DIM = 256
GLA_VAL_DIM = 512
GLA_KEY_WIDTH = GLA_HEADS * GLA_KEY_DIM
GLA_WIDTH = GLA_HEADS * GLA_VAL_DIM
GLA_GATE_RANK = 16
GLA_GATE_NORMALIZER = 16.0
MIX_WIDTH = FOX_WIDTH + GLA_WIDTH
EPS = 1e-5

LN_ROWS = 256
MM_TM = 512
MM_TN = 1024
FFN_TN = 512
PREP_ROWS = 1024
PREP_COLS = 1024
DOWN_TM = 1024
DOWN_TN = 1024
DOWN_TK = 2816
GATE_ROWS = 512
FOX_TQ = 512
FOX_TK = 512
FOX_GROUP = 4
GLA_ROWS = 512
GLA_CHUNK = 64
NEG_BIG = -1e30
LOG2_E = 1.4426950408889634

_NT = (((1,), (1,)), ((), ()))
_TN = (((0,), (0,)), ((), ()))


def _vmem_limit(block_bytes, extra_bytes):
    need = 2 * block_bytes + extra_bytes + (4 << 20)
    return int(min(need, V7X_VMEM_BYTES - (6 << 20)))


def _nbytes(shape, dtype):
    return int(np.prod(shape)) * jnp.dtype(dtype).itemsize


def _log_sigmoid(x):
    return jnp.minimum(x, 0.0) - jnp.log1p(jnp.exp(-jnp.abs(x)))


def _silu(x):
    return x * (1.0 / (1.0 + jnp.exp(-x)))


def _ln_math(t, g, b):
    mu = jnp.mean(t, axis=-1, keepdims=True)
    tc = t - mu
    var = jnp.mean(tc * tc, axis=-1, keepdims=True)
    return tc * lax.rsqrt(var + EPS) * g + b


def _ln_kernel(*refs, alpha, has_res, want_f32, want_bf16):
    if has_res:
        x_ref, y_ref, g_ref, b_ref = refs[:4]
        outs = refs[4:]
        t = alpha * x_ref[...] + y_ref[...]
    else:
        x_ref, g_ref, b_ref = refs[:3]
        outs = refs[3:]
        t = x_ref[...]
    y = _ln_math(t, g_ref[...], b_ref[...])
    i = 0
    if want_f32:
        outs[i][...] = y
        i += 1
    if want_bf16:
        outs[i][...] = y.astype(BF16)


def _layer_norm(x, res, g, b, *, alpha=1.0, want_f32=True, want_bf16=True):
    rows, d = x.shape
    assert rows % LN_ROWS == 0
    row_spec = pl.BlockSpec((LN_ROWS, d), lambda i: (i, 0))
    vec_spec = pl.BlockSpec((1, d), lambda i: (0, 0))
    ins = [x] + ([res] if res is not None else []) + [g.reshape(1, d), b.reshape(1, d)]
    in_specs = [row_spec] * (len(ins) - 2) + [vec_spec, vec_spec]
    out_shape, out_specs = [], []
    if want_f32:
        out_shape.append(jax.ShapeDtypeStruct((rows, d), F32))
        out_specs.append(row_spec)
    if want_bf16:
        out_shape.append(jax.ShapeDtypeStruct((rows, d), BF16))
        out_specs.append(row_spec)
    blk = _nbytes((LN_ROWS, d), F32) * (len(ins) - 2 + len(out_shape))
    return pl.pallas_call(
        functools.partial(_ln_kernel, alpha=alpha, has_res=res is not None,
                          want_f32=want_f32, want_bf16=want_bf16),
        grid=(rows // LN_ROWS,),
        in_specs=in_specs, out_specs=out_specs, out_shape=out_shape,
        compiler_params=pltpu.CompilerParams(
            dimension_semantics=("arbitrary",),
            vmem_limit_bytes=_vmem_limit(blk, 4 * _nbytes((LN_ROWS, d), F32))),
        name="layer_norm",
    )(*ins)


def _mm_kernel(x_ref, w_ref, o_ref, *scratch, cast, scaled_tiles, scale):
    if cast:
        wb_ref, = scratch

        @pl.when(pl.program_id(1) == 0)
        def _():
            wb_ref[...] = w_ref[...].astype(BF16)
        w = wb_ref[...]
    else:
        w = w_ref[...]
    acc = jnp.dot(x_ref[...], w, preferred_element_type=F32)
    if scaled_tiles:
        acc = acc * jnp.where(pl.program_id(0) < scaled_tiles, scale, 1.0)
    o_ref[...] = acc.astype(o_ref.dtype)


def _matmul(x, w, layer, n_cols, out_dtype, *, tm=MM_TM, tn=MM_TN, scaled_cols=0, scale=1.0):
    m, kdim = x.shape
    assert m % tm == 0 and n_cols % tn == 0 and w.shape[1] == kdim and scaled_cols % tn == 0
    cast = w.dtype != BF16
    blk = _nbytes((tm, kdim), x.dtype) + _nbytes((kdim, tn), w.dtype) + _nbytes((tm, tn), out_dtype)
    extra = _nbytes((kdim, tn), BF16) + 2 * _nbytes((tm, tn), F32)
    return pl.pallas_call(
        functools.partial(_mm_kernel, cast=cast, scaled_tiles=scaled_cols // tn, scale=scale),
        grid=(n_cols // tn, m // tm),
        in_specs=[pl.BlockSpec((tm, kdim), lambda j, i: (i, 0)),
                  pl.BlockSpec((None, kdim, tn), lambda j, i: (layer, 0, j))],
        out_specs=pl.BlockSpec((tm, tn), lambda j, i: (i, j)),
        out_shape=jax.ShapeDtypeStruct((m, n_cols), out_dtype),
        scratch_shapes=[pltpu.VMEM((kdim, tn), BF16)] if cast else [],
        compiler_params=pltpu.CompilerParams(
            dimension_semantics=("arbitrary", "arbitrary"),
            vmem_limit_bytes=_vmem_limit(blk, extra)),
        name="matmul",
    )(x, w)


def _shift_cast_kernel(a_ref, b_ref, o_ref, *, shift):
    o_ref[...] = jnp.concatenate([a_ref[:, shift:], b_ref[:, :shift]], axis=1).astype(o_ref.dtype)


def _shift_cast(w, layer, col0, n_cols, *, tr=PREP_ROWS, tc=PREP_COLS):
    kdim = w.shape[1]
    shift = col0 % V7X_LANES
    base = col0 - shift
    assert shift and base % tc == 0 and n_cols % tc == 0 and kdim % tr == 0
    tail_blk = tc // V7X_LANES
    blk = _nbytes((tr, tc + V7X_LANES), w.dtype) + _nbytes((tr, tc), BF16)
    return pl.pallas_call(
        functools.partial(_shift_cast_kernel, shift=shift),
        grid=(kdim // tr, n_cols // tc),
        in_specs=[pl.BlockSpec((None, tr, tc), lambda r, c: (layer, r, base // tc + c)),
                  pl.BlockSpec((None, tr, V7X_LANES),
                               lambda r, c: (layer, r, base // V7X_LANES + (c + 1) * tail_blk))],
        out_specs=pl.BlockSpec((tr, tc), lambda r, c: (r, c)),
        out_shape=jax.ShapeDtypeStruct((kdim, n_cols), BF16),
        compiler_params=pltpu.CompilerParams(
            dimension_semantics=("arbitrary", "arbitrary"),
            vmem_limit_bytes=_vmem_limit(blk, 3 * _nbytes((tr, tc), F32))),
        name="shift_cast",
    )(w, w)


def _gate_up_kernel(x_ref, wg_ref, wu_ref, o_ref, wb_ref, *, tn, d_ff):
    @pl.when(pl.program_id(1) == 0)
    def _():
        wb_ref[:, :tn] = wg_ref[...].astype(BF16)
        wb_ref[:, tn:] = wu_ref[...].astype(BF16)

    z = jnp.dot(x_ref[...], wb_ref[...], preferred_element_type=F32)
    h = _silu(z[:, :tn]) * z[:, tn:]
    col = pl.program_id(0) * tn + lax.broadcasted_iota(jnp.int32, h.shape, 1)
    o_ref[...] = jnp.where(col < d_ff, h, 0.0).astype(o_ref.dtype)


def _gate_up(x, w_gate, w_up, layer, *, tm=MM_TM, tn=FFN_TN):
    m, kdim = x.shape
    d_ff = w_gate.shape[2]
    n_out = pl.cdiv(d_ff, DOWN_TK) * DOWN_TK
    assert m % tm == 0 and n_out % tn == 0
    last_w_blk = pl.cdiv(d_ff, tn) - 1
    w_spec = pl.BlockSpec((None, kdim, tn), lambda j, i: (layer, 0, jnp.minimum(j, last_w_blk)))
    blk = _nbytes((tm, kdim), x.dtype) + 2 * _nbytes((kdim, tn), F32) + _nbytes((tm, tn), BF16)
    extra = _nbytes((kdim, 2 * tn), BF16) + 3 * _nbytes((tm, 2 * tn), F32)
    return pl.pallas_call(
        functools.partial(_gate_up_kernel, tn=tn, d_ff=d_ff),
        grid=(n_out // tn, m // tm),
        in_specs=[pl.BlockSpec((tm, kdim), lambda j, i: (i, 0)), w_spec, w_spec],
        out_specs=pl.BlockSpec((tm, tn), lambda j, i: (i, j)),
        out_shape=jax.ShapeDtypeStruct((m, n_out), BF16),
        scratch_shapes=[pltpu.VMEM((kdim, 2 * tn), BF16)],
        compiler_params=pltpu.CompilerParams(
            dimension_semantics=("arbitrary", "arbitrary"),
            vmem_limit_bytes=_vmem_limit(blk, extra)),
        name="gate_up",
    )(x, w_gate, w_up)


def _down_kernel(h_ref, w_ref, o_ref, acc_ref, *, nk):
    k = pl.program_id(2)
    part = jnp.dot(h_ref[...], w_ref[...], preferred_element_type=F32)
    if nk == 1:
        o_ref[...] = part
        return

    @pl.when(k == 0)
    def _():
        acc_ref[...] = part

    @pl.when(jnp.logical_and(k > 0, k < nk - 1))
    def _():
        acc_ref[...] += part

    @pl.when(k == nk - 1)
    def _():
        o_ref[...] = acc_ref[...] + part


def _down(h, w_down, *, tm=DOWN_TM, tn=DOWN_TN, tk=DOWN_TK):
    m, kp = h.shape
    n = w_down.shape[1]
    assert m % tm == 0 and n % tn == 0 and kp % tk == 0 and w_down.shape[0] == kp
    nk = kp // tk
    blk = _nbytes((tm, tk), h.dtype) + _nbytes((tk, tn), w_down.dtype) + _nbytes((tm, tn), F32)
    extra = 3 * _nbytes((tm, tn), F32)
    return pl.pallas_call(
        functools.partial(_down_kernel, nk=nk),
        grid=(m // tm, n // tn, nk),
        in_specs=[pl.BlockSpec((tm, tk), lambda i, j, k: (i, k)),
                  pl.BlockSpec((tk, tn), lambda i, j, k: (k, j))],
        out_specs=pl.BlockSpec((tm, tn), lambda i, j, k: (i, j)),
        out_shape=jax.ShapeDtypeStruct((m, n), F32),
        scratch_shapes=[pltpu.VMEM((tm, tn), F32)],
        compiler_params=pltpu.CompilerParams(
            dimension_semantics=("arbitrary", "arbitrary", "arbitrary"),
            vmem_limit_bytes=_vmem_limit(blk, extra)),
        name="down_proj",
    )(h, w_down)


def _split3(x):
    a = x.astype(BF16).astype(F32)
    r = x - a
    b = r.astype(BF16).astype(F32)
    return a, b, (r - b).astype(BF16).astype(F32)


def _gates_kernel(x_ref, w_ref, bias_ref, tri_ref, o_ref, aq_ref, ak_ref, carry_ref):
    @pl.when(pl.program_id(1) == 0)
    def _():
        carry_ref[...] = jnp.zeros_like(carry_ref)

    z = jnp.dot(x_ref[...], w_ref[...], preferred_element_type=F32)
    log_f = _log_sigmoid(z + bias_ref[...])
    c = jnp.dot(tri_ref[...], log_f, preferred_element_type=F32,
                precision=lax.Precision.HIGHEST) + carry_ref[0:1, :]
    rows = c.shape[0]
    carry_ref[...] = jnp.broadcast_to(c[rows - 1:rows, :], carry_ref.shape)
    lane = lax.broadcasted_iota(jnp.int32, c.shape, 1)
    o_ref[...] = jnp.where(lane < FOX_HEADS, c, z)

    c2 = c * LOG2_E
    for h in range(FOX_HEADS):
        c1, c2_, c3 = _split3(jnp.broadcast_to(c2[:, h:h + 1], c.shape))
        aq = jnp.where(lane == 0, c1, jnp.where(lane == 1, c2_, jnp.where(lane == 2, c3,
                       jnp.where(lane < 6, 1.0, 0.0))))
        ak = jnp.where(lane < 3, 1.0, jnp.where(lane == 3, -c1, jnp.where(lane == 4, -c2_,
                       jnp.where(lane == 5, -c3, 0.0))))
        aq_ref[0, h] = aq.astype(BF16)
        ak_ref[0, h] = ak.astype(BF16)


def _gates(x, w_small, bias, bsz, seq):
    m, kdim = x.shape
    steps = seq // GATE_ROWS
    tri = jnp.asarray(np.tril(np.ones((GATE_ROWS, GATE_ROWS), np.float32)))
    aug_shape = jax.ShapeDtypeStruct((bsz, FOX_HEADS, seq, V7X_LANES), BF16)
    aug_spec = pl.BlockSpec((1, FOX_HEADS, GATE_ROWS, V7X_LANES), lambda b, i: (b, 0, i, 0))
    blk = (_nbytes((GATE_ROWS, kdim), x.dtype) + _nbytes((kdim, V7X_LANES), w_small.dtype)
           + _nbytes((GATE_ROWS, GATE_ROWS), F32) + _nbytes((GATE_ROWS, V7X_LANES), F32)
           + 2 * _nbytes((FOX_HEADS, GATE_ROWS, V7X_LANES), BF16))
    return pl.pallas_call(
        _gates_kernel,
        grid=(bsz, steps),
        in_specs=[pl.BlockSpec((GATE_ROWS, kdim), lambda b, i: (b * steps + i, 0)),
                  pl.BlockSpec((kdim, V7X_LANES), lambda b, i: (0, 0)),
                  pl.BlockSpec((1, V7X_LANES), lambda b, i: (0, 0)),
                  pl.BlockSpec((GATE_ROWS, GATE_ROWS), lambda b, i: (0, 0))],
        out_specs=[pl.BlockSpec((GATE_ROWS, V7X_LANES), lambda b, i: (b * steps + i, 0)), aug_spec, aug_spec],
        out_shape=[jax.ShapeDtypeStruct((m, V7X_LANES), F32), aug_shape, aug_shape],
        scratch_shapes=[pltpu.VMEM((8, V7X_LANES), F32)],
        compiler_params=pltpu.CompilerParams(
            dimension_semantics=("arbitrary", "arbitrary"),
            vmem_limit_bytes=_vmem_limit(blk, 16 * _nbytes((GATE_ROWS, V7X_LANES), F32))),
        name="gates",
    )(x, w_small, bias, tri)


def _fox_kernel(q_ref, k_ref, v_ref, aq_ref, ak_ref, g_ref, o_ref, kp_ref, vp_ref, m_ref, acc_ref, *, tq, tk):
    qi = pl.program_id(2)
    n_heads = aq_ref.shape[1]
    d = q_ref.shape[2] // n_heads

    @pl.when(qi == 0)
    def _():
        lane = lax.broadcasted_iota(jnp.int32, (vp_ref.shape[1], V7X_LANES), 1)
        for g in range(n_heads):
            kp_ref[g, :, :d] = k_ref[0, :, g * d:(g + 1) * d]
            kp_ref[g, :, d:] = ak_ref[0, g]
            vp_ref[g, :, :d] = v_ref[0, :, g * d:(g + 1) * d]
            vp_ref[g, :, d:] = jnp.where(lane == 0, 1.0, 0.0).astype(BF16)

    qp = [jnp.concatenate([q_ref[0, :, g * d:(g + 1) * d], aq_ref[0, g]], axis=1) for g in range(n_heads)]
    m_ref[...] = jnp.full_like(m_ref, NEG_BIG)
    acc_ref[...] = jnp.zeros_like(acc_ref)

    def step(j, masked):
        start = pl.multiple_of(j * tk, tk)
        for g in range(n_heads):
            s = lax.dot_general(qp[g], kp_ref[g, pl.ds(start, tk), :], _NT, preferred_element_type=F32)
            if masked:
                row = lax.broadcasted_iota(jnp.int32, (tq, tk), 0)
                col = lax.broadcasted_iota(jnp.int32, (tq, tk), 1)
                s = jnp.where(row >= col, s, NEG_BIG)
            m_prev = m_ref[g]
            m_new = jnp.maximum(m_prev, jnp.max(s, axis=-1, keepdims=True))
            p = jnp.exp2(s - m_new).astype(BF16)
            acc_ref[g] = (jnp.exp2(m_prev - m_new) * acc_ref[g]
                          + jnp.dot(p, vp_ref[g, pl.ds(start, tk), :], preferred_element_type=F32))
            m_ref[g] = m_new

    def body(j, carry):
        step(j, False)
        return carry

    lax.fori_loop(0, qi, body, 0)
    step(qi, True)

    for g in range(n_heads):
        acc = acc_ref[g]
        o = acc[:, :d] / acc[:, d:d + 1]
        o = o * lax.rsqrt(jnp.mean(o * o, axis=-1, keepdims=True) + EPS) * g_ref[g]
        o_ref[0, :, g * d:(g + 1) * d] = o.astype(o_ref.dtype)


def _fox(proj, aq, ak, norm_g, *, tq=FOX_TQ, tk=FOX_TK, group=FOX_GROUP):
    bsz, seq, _ = proj.shape
    d = FOX_HEAD_DIM
    dp = d + V7X_LANES
    gd = group * d
    n_groups = FOX_HEADS // group
    assert tq == tk and seq % tq == 0 and FOX_HEADS % group == 0
    blk = (2 * _nbytes((tq, gd), BF16) + 2 * _nbytes((seq, gd), BF16) + _nbytes((group, tq, V7X_LANES), BF16)
           + _nbytes((group, seq, V7X_LANES), BF16) + _nbytes((group, 8, d), F32))
    scratch = (2 * _nbytes((group, seq, dp), BF16) + _nbytes((group, tq, V7X_LANES), F32)
               + _nbytes((group, tq, dp), F32))
    return pl.pallas_call(
        functools.partial(_fox_kernel, tq=tq, tk=tk),
        grid=(bsz, n_groups, seq // tq),
        in_specs=[pl.BlockSpec((1, tq, gd), lambda b, h, i: (b, i, h)),
                  pl.BlockSpec((1, seq, gd), lambda b, h, i: (b, 0, n_groups + h)),
                  pl.BlockSpec((1, seq, gd), lambda b, h, i: (b, 0, 2 * n_groups + h)),
                  pl.BlockSpec((1, group, tq, V7X_LANES), lambda b, h, i: (b, h, i, 0)),
                  pl.BlockSpec((1, group, seq, V7X_LANES), lambda b, h, i: (b, h, 0, 0)),
                  pl.BlockSpec((group, 1, d), lambda b, h, i: (h, 0, 0))],
        out_specs=pl.BlockSpec((1, tq, gd), lambda b, h, i: (b, i, h)),
        out_shape=jax.ShapeDtypeStruct((bsz, seq, MIX_WIDTH), BF16),
        scratch_shapes=[pltpu.VMEM((group, seq, dp), BF16), pltpu.VMEM((group, seq, dp), BF16),
                        pltpu.VMEM((group, tq, 1), F32), pltpu.VMEM((group, tq, dp), F32)],
        compiler_params=pltpu.CompilerParams(
            dimension_semantics=("arbitrary", "arbitrary", "arbitrary"),
            vmem_limit_bytes=_vmem_limit(blk, scratch + 8 * group * _nbytes((tq, tk), F32))),
        name="fox_attention",
    )(proj, proj, proj, aq, ak, norm_g)


def _gla_tables(chunk):
    idx = np.arange(chunk)
    masks = []
    half = chunk // 2
    while half >= 1:
        blk = idx // (2 * half)
        second = (idx // half) % 2 == 1
        masks.append(((blk[:, None] == blk[None, :]) & second[:, None] & ~second[None, :]).astype(np.float32))
        half //= 2
    return np.tril(np.ones((chunk, chunk), np.float32)), np.stack(masks, axis=0)


def _level_refs(b):
    chunk, width = b.shape
    refs = []
    half = chunk // 2
    while half >= 4:
        pieces = [jnp.broadcast_to(b[s + half - 1:s + half], (2 * half, width)) for s in range(0, chunk, 2 * half)]
        refs.append(pieces[0] if len(pieces) == 1 else jnp.concatenate(pieces, axis=0))
        half //= 2
    prev1 = pltpu.roll(b, 1, 0)
    prev2 = pltpu.roll(b, 2, 0)
    next1 = pltpu.roll(b, chunk - 1, 0)
    pos = lax.broadcasted_iota(jnp.int32, b.shape, 0) & 3
    refs.append(jnp.where(pos == 0, next1, jnp.where(pos == 1, b, jnp.where(pos == 2, prev1, prev2))))
    refs.append(jnp.where((pos & 1) == 0, b, prev1))
    return refs


def _gla_kernel(q_ref, k_ref, v_ref, gg_ref, gates_ref, wup_ref, bgk_ref, gn_ref, tri_ref, mask_ref,
                mixed_hbm_ref, o_ref, state_ref, la_ref, *, rows, chunk, scale):
    del mixed_hbm_ref
    dk, dv = GLA_KEY_DIM, GLA_VAL_DIM

    @pl.when(pl.program_id(1) == 0)
    def _():
        state_ref[...] = jnp.zeros_like(state_ref)

    pre = jnp.dot(gates_ref[0], wup_ref[...], preferred_element_type=F32,
                  precision=lax.Precision.HIGHEST) + bgk_ref[...]
    la_ref[...] = _log_sigmoid(pre) * (1.0 / GLA_GATE_NORMALIZER)

    n_levels = mask_ref.shape[0]
    eye = (lax.broadcasted_iota(jnp.int32, (chunk, chunk), 0)
           == lax.broadcasted_iota(jnp.int32, (chunk, chunk), 1))
    row_dk = lax.broadcasted_iota(jnp.int32, (chunk, dk), 0)
    ones_cols = jnp.ones((chunk, V7X_LANES), BF16)
    tri = tri_ref[...]

    def head_chunk(r0, h):
        ks = slice(h * dk, (h + 1) * dk)
        vs = slice(h * dv, (h + 1) * dv)
        g = la_ref[pl.ds(r0, chunk), ks]
        g_parts = jnp.concatenate([p.astype(BF16) for p in _split3(g)], axis=1)
        b3 = jnp.dot(tri, g_parts, preferred_element_type=F32)
        b = b3[:, :dk] + b3[:, dk:2 * dk] + b3[:, 2 * dk:]
        q = q_ref[0, pl.ds(r0, chunk), ks].astype(F32) * scale
        k = k_ref[0, pl.ds(r0, chunk), ks].astype(F32)
        v = v_ref[0, pl.ds(r0, chunk), vs]

        state = state_ref[h]
        o = jnp.dot((q * jnp.exp(b)).astype(BF16), state.astype(BF16), preferred_element_type=F32)

        scores = jnp.zeros((chunk, chunk), F32)
        for lvl, ref in enumerate(_level_refs(b)):
            qt = (q * jnp.exp(jnp.minimum(b - ref, 0.0))).astype(BF16)
            kt = (k * jnp.exp(jnp.minimum(ref - b, 0.0))).astype(BF16)
            scores = scores + mask_ref[lvl] * lax.dot_general(qt, kt, _NT, preferred_element_type=F32)
        scores = jnp.where(eye, jnp.sum(q * k, axis=-1, keepdims=True), scores)
        o = o + jnp.dot(scores.astype(BF16), v, preferred_element_type=F32)

        b_last = b[chunk - 1:chunk]
        k_dec = (k * jnp.exp(b_last - b)).astype(BF16)
        e1, e2, e3 = _split3(jnp.broadcast_to(jnp.exp(b_last), (chunk, dk)))
        e_rows = jnp.where(row_dk == 0, e1, jnp.where(row_dk == 1, e2, jnp.where(row_dk == 2, e3, 0.0)))
        dec_col = lax.dot_general(e_rows.astype(BF16), ones_cols, _TN, preferred_element_type=F32)
        decay = jnp.concatenate([dec_col] * (dv // V7X_LANES), axis=1)
        state_ref[h] = state * decay + lax.dot_general(k_dec, v, _TN, preferred_element_type=F32)

        o = o * lax.rsqrt(jnp.mean(o * o, axis=-1, keepdims=True) + EPS) * gn_ref[...]
        gate = _silu(gg_ref[0, pl.ds(r0, chunk), vs].astype(F32))
        o_ref[0, pl.ds(r0, chunk), vs] = (o * gate).astype(o_ref.dtype)

    assert n_levels == chunk.bit_length() - 1

    def chunk_body(c, carry):
        r0 = pl.multiple_of(c * chunk, chunk)
        for h in range(GLA_HEADS):
            head_chunk(r0, h)
        return carry

    lax.fori_loop(0, rows // chunk, chunk_body, 0)


def _gla(proj, gates, w_up_pad, b_gk, norm_g, mixed, *, rows=GLA_ROWS, chunk=GLA_CHUNK):
    bsz, seq, _ = proj.shape
    kw, vw = GLA_KEY_WIDTH, GLA_WIDTH
    assert seq % rows == 0 and rows % chunk == 0 and vw == 2 * kw and FOX_WIDTH == vw
    tri, masks = _gla_tables(chunk)
    blk = (2 * _nbytes((rows, kw), BF16) + 3 * _nbytes((rows, vw), BF16) + _nbytes((rows, V7X_LANES), F32)
           + _nbytes((V7X_LANES, kw), F32) + _nbytes(masks.shape, F32))
    scratch = _nbytes((GLA_HEADS, GLA_KEY_DIM, GLA_VAL_DIM), F32) + _nbytes((rows, kw), F32)
    return pl.pallas_call(
        functools.partial(_gla_kernel, rows=rows, chunk=chunk, scale=GLA_KEY_DIM ** -0.5),
        grid=(bsz, seq // rows),
        in_specs=[pl.BlockSpec((1, rows, kw), lambda b, t: (b, t, 0)),
                  pl.BlockSpec((1, rows, kw), lambda b, t: (b, t, 1)),
                  pl.BlockSpec((1, rows, vw), lambda b, t: (b, t, 1)),
                  pl.BlockSpec((1, rows, vw), lambda b, t: (b, t, 2)),
                  pl.BlockSpec((1, rows, V7X_LANES), lambda b, t: (b, t, 0)),
                  pl.BlockSpec((V7X_LANES, kw), lambda b, t: (0, 0)),
                  pl.BlockSpec((1, kw), lambda b, t: (0, 0)),
                  pl.BlockSpec((1, GLA_VAL_DIM), lambda b, t: (0, 0)),
                  pl.BlockSpec(tri.shape, lambda b, t: (0, 0)),
                  pl.BlockSpec(masks.shape, lambda b, t: (0, 0, 0)),
                  pl.BlockSpec(memory_space=pl.ANY)],
        out_specs=pl.BlockSpec((1, rows, vw), lambda b, t: (b, t, 1)),
        out_shape=jax.ShapeDtypeStruct(mixed.shape, mixed.dtype),
        scratch_shapes=[pltpu.VMEM((GLA_HEADS, GLA_KEY_DIM, GLA_VAL_DIM), F32), pltpu.VMEM((rows, kw), F32)],
        input_output_aliases={10: 0},
        compiler_params=pltpu.CompilerParams(
            dimension_semantics=("arbitrary", "arbitrary"),
            vmem_limit_bytes=_vmem_limit(blk, scratch + (8 << 20))),
        name="gla",
    )(proj, proj, proj, proj, gates, w_up_pad, b_gk.reshape(1, -1), norm_g.reshape(1, -1),
      jnp.asarray(tri, BF16), jnp.asarray(masks), mixed)


IN_F_LO = 3 * FOX_WIDTH
IN_G_LO = IN_F_LO + FOX_HEADS
IN_LOW_LO = IN_G_LO + 2 * GLA_KEY_WIDTH + 2 * GLA_WIDTH


def _prep_layer(w_in, b_f, w_gk_up):
    d_model = w_in.shape[0]
    f_lo, g_lo, low_lo = IN_F_LO, IN_G_LO, IN_LOW_LO
    pad = V7X_LANES - FOX_HEADS - GLA_GATE_RANK
    w_small = jnp.concatenate([w_in[:, f_lo:g_lo], w_in[:, low_lo:], jnp.zeros((d_model, pad), F32)],
                              axis=1).astype(BF16)
    gate_bias = jnp.concatenate([b_f, jnp.zeros((V7X_LANES - FOX_HEADS,), F32)]).reshape(1, V7X_LANES)
    w_up_pad = jnp.zeros((V7X_LANES, GLA_KEY_WIDTH), F32).at[FOX_HEADS:FOX_HEADS + GLA_GATE_RANK].set(w_gk_up)
    return w_small, gate_bias, w_up_pad


def kernel(x, ln_in_g, ln_in_b, w_in, b_f, w_gk_up, b_gk, fox_norm_g, gla_norm_g, w_out, ln1_g, ln1_b,
           w_gate, w_up, w_down, ln2_g, ln2_b):
    bsz, seq, d_model = x.shape
    depth = w_in.shape[0]
    alpha = (2 * depth) ** 0.25
    rows = bsz * seq

    x32, x16 = _layer_norm(x.reshape(rows, d_model), None, ln_in_g, ln_in_b)
    for l in range(depth):
        w_small, gate_bias, w_up_pad = _prep_layer(w_in[l], b_f[l], w_gk_up[l])
        w_gla = _shift_cast(w_in, l, IN_G_LO, IN_LOW_LO - IN_G_LO)[None]

        proj_fox = _matmul(x16, w_in, l, IN_F_LO, BF16, scaled_cols=FOX_WIDTH,
                           scale=FOX_HEAD_DIM ** -0.5 * LOG2_E).reshape(bsz, seq, -1)
        proj_gla = _matmul(x16, w_gla, 0, w_gla.shape[2], BF16).reshape(bsz, seq, -1)
        gates, aug_q, aug_k = _gates(x16, w_small, gate_bias, bsz, seq)
        gates = gates.reshape(bsz, seq, V7X_LANES)

        mixed = _fox(proj_fox, aug_q, aug_k, fox_norm_g[l].reshape(FOX_HEADS, 1, FOX_HEAD_DIM))
        mixed = _gla(proj_gla, gates, w_up_pad, b_gk[l], gla_norm_g[l], mixed)

        mix = _matmul(mixed.reshape(rows, MIX_WIDTH), w_out, l, d_model, F32)
        x32, x16 = _layer_norm(x32, mix, ln1_g[l], ln1_b[l], alpha=alpha)

        hidden = _gate_up(x16, w_gate, w_up, l)
        w_dn = jnp.pad(w_down[l], ((0, hidden.shape[1] - w_down.shape[1]), (0, 0))).astype(BF16)
        ffn = _down(hidden, w_dn)
        last = l == depth - 1
        outs = _layer_norm(x32, ffn, ln2_g[l], ln2_b[l], alpha=alpha, want_bf16=not last)
        if last:
            x32, = outs
        else:
            x32, x16 = outs
    return x32.reshape(bsz, seq, d_model)
```

```python
import functools

import numpy as np
import jax
import jax.numpy as jnp
from jax import lax
from jax.experimental import pallas as pl
from jax.experimental.pallas import tpu as pltpu

F32 = jnp.float32
BF16 = jnp.bfloat16

V7X_VMEM_BYTES = 64 * 2**20
V7X_LANES = 128
V7X_MXU_DIM = 256

FOX_HEADS = 16
FOX_HEAD_DIM = 128
FOX_WIDTH = FOX_HEADS * FOX_HEAD_DIM
GLA_HEADS = 4
GLA_KEY_DIM = 256
GLA_VAL_DIM = 512
GLA_KEY_WIDTH = GLA_HEADS * GLA_KEY_DIM
GLA_WIDTH = GLA_HEADS * GLA_VAL_DIM
GLA_GATE_RANK = 16
GLA_GATE_NORMALIZER = 16.0
MIX_WIDTH = FOX_WIDTH + GLA_WIDTH
EPS = 1e-5

LN_ROWS = 256
MM_TM = 512
MM_TN = 1024
FFN_TN = 512
PREP_ROWS = 1024
PREP_COLS = 1024
PREP_ROW_BLOCK = 256
DOWN_TM = 1024
DOWN_TN = 1024
DOWN_TK = 2816
GATE_ROWS = 512
FOX_TQ = 512
FOX_TK = 512
FOX_GROUP = 4
GLA_ROWS = 512
GLA_CHUNK = 64
NEG_BIG = -1e30
LOG2_E = 1.4426950408889634

_NT = (((1,), (1,)), ((), ()))
_TN = (((0,), (0,)), ((), ()))


def _vmem_limit(block_bytes, extra_bytes):
    need = 2 * block_bytes + extra_bytes + (4 << 20)
    return int(min(need, V7X_VMEM_BYTES - (6 << 20)))


def _nbytes(shape, dtype):
    return int(np.prod(shape)) * jnp.dtype(dtype).itemsize


def _log_sigmoid(x):
    return jnp.minimum(x, 0.0) - jnp.log1p(jnp.exp(-jnp.abs(x)))


def _silu(x):
    return x * (1.0 / (1.0 + jnp.exp(-x)))


def _ln_math(t, g, b):
    mu = jnp.mean(t, axis=-1, keepdims=True)
    tc = t - mu
    var = jnp.mean(tc * tc, axis=-1, keepdims=True)
    return tc * lax.rsqrt(var + EPS) * g + b


def _ln_kernel(*refs, alpha, has_res, want_f32, want_bf16):
    if has_res:
        x_ref, y_ref, g_ref, b_ref = refs[:4]
        outs = refs[4:]
        t = alpha * x_ref[...] + y_ref[...]
    else:
        x_ref, g_ref, b_ref = refs[:3]
        outs = refs[3:]
        t = x_ref[...]
    y = _ln_math(t, g_ref[...], b_ref[...])
    i = 0
    if want_f32:
        outs[i][...] = y
        i += 1
    if want_bf16:
        outs[i][...] = y.astype(BF16)


def _layer_norm(x, res, g, b, *, alpha=1.0, want_f32=True, want_bf16=True):
    rows, d = x.shape
    assert rows % LN_ROWS == 0
    row_spec = pl.BlockSpec((LN_ROWS, d), lambda i: (i, 0))
    vec_spec = pl.BlockSpec((1, d), lambda i: (0, 0))
    ins = [x] + ([res] if res is not None else []) + [g.reshape(1, d), b.reshape(1, d)]
    in_specs = [row_spec] * (len(ins) - 2) + [vec_spec, vec_spec]
    out_shape, out_specs = [], []
    if want_f32:
        out_shape.append(jax.ShapeDtypeStruct((rows, d), F32))
        out_specs.append(row_spec)
    if want_bf16:
        out_shape.append(jax.ShapeDtypeStruct((rows, d), BF16))
        out_specs.append(row_spec)
    blk = _nbytes((LN_ROWS, d), F32) * (len(ins) - 2 + len(out_shape))
    return pl.pallas_call(
        functools.partial(_ln_kernel, alpha=alpha, has_res=res is not None,
                          want_f32=want_f32, want_bf16=want_bf16),
        grid=(rows // LN_ROWS,),
        in_specs=in_specs, out_specs=out_specs, out_shape=out_shape,
        compiler_params=pltpu.CompilerParams(
            dimension_semantics=("arbitrary",),
            vmem_limit_bytes=_vmem_limit(blk, 4 * _nbytes((LN_ROWS, d), F32))),
        name="layer_norm",
    )(*ins)


def _mm_kernel(x_ref, w_ref, o_ref, *scratch, cast, scaled_tiles, scale):
    if cast:
        wb_ref, = scratch

        @pl.when(pl.program_id(1) == 0)
        def _():
            wb_ref[...] = w_ref[...].astype(BF16)
        w = wb_ref[...]
    else:
        w = w_ref[...]
    acc = jnp.dot(x_ref[...], w, preferred_element_type=F32)
    if scaled_tiles:
        acc = acc * jnp.where(pl.program_id(0) < scaled_tiles, scale, 1.0)
    o_ref[...] = acc.astype(o_ref.dtype)


def _matmul(x, w, layer, n_cols, out_dtype, *, tm=MM_TM, tn=MM_TN, scaled_cols=0, scale=1.0):
    m, kdim = x.shape
    assert m % tm == 0 and n_cols % tn == 0 and w.shape[1] == kdim and scaled_cols % tn == 0
    cast = w.dtype != BF16
    blk = _nbytes((tm, kdim), x.dtype) + _nbytes((kdim, tn), w.dtype) + _nbytes((tm, tn), out_dtype)
    extra = _nbytes((kdim, tn), BF16) + 2 * _nbytes((tm, tn), F32)
    return pl.pallas_call(
        functools.partial(_mm_kernel, cast=cast, scaled_tiles=scaled_cols // tn, scale=scale),
        grid=(n_cols // tn, m // tm),
        in_specs=[pl.BlockSpec((tm, kdim), lambda j, i: (i, 0)),
                  pl.BlockSpec((None, kdim, tn), lambda j, i: (layer, 0, j))],
        out_specs=pl.BlockSpec((tm, tn), lambda j, i: (i, j)),
        out_shape=jax.ShapeDtypeStruct((m, n_cols), out_dtype),
        scratch_shapes=[pltpu.VMEM((kdim, tn), BF16)] if cast else [],
        compiler_params=pltpu.CompilerParams(
            dimension_semantics=("arbitrary", "arbitrary"),
            vmem_limit_bytes=_vmem_limit(blk, extra)),
        name="matmul",
    )(x, w)


def _shift_cast_kernel(a_ref, b_ref, o_ref, *, shift):
    o_ref[...] = jnp.concatenate([a_ref[:, shift:], b_ref[:, :shift]], axis=1).astype(o_ref.dtype)


def _shift_cast(w, layer, col0, n_cols, *, tr=PREP_ROWS, tc=PREP_COLS):
    kdim = w.shape[1]
    shift = col0 % V7X_LANES
    base = col0 - shift
    assert shift and base % tc == 0 and n_cols % tc == 0 and kdim % tr == 0
    tail_blk = tc // V7X_LANES
    blk = _nbytes((tr, tc + V7X_LANES), w.dtype) + _nbytes((tr, tc), BF16)
    return pl.pallas_call(
        functools.partial(_shift_cast_kernel, shift=shift),
        grid=(kdim // tr, n_cols // tc),
        in_specs=[pl.BlockSpec((None, tr, tc), lambda r, c: (layer, r, base // tc + c)),
                  pl.BlockSpec((None, tr, V7X_LANES),
                               lambda r, c: (layer, r, base // V7X_LANES + (c + 1) * tail_blk))],
        out_specs=pl.BlockSpec((tr, tc), lambda r, c: (r, c)),
        out_shape=jax.ShapeDtypeStruct((kdim, n_cols), BF16),
        compiler_params=pltpu.CompilerParams(
            dimension_semantics=("arbitrary", "arbitrary"),
            vmem_limit_bytes=_vmem_limit(blk, 3 * _nbytes((tr, tc), F32))),
        name="shift_cast",
    )(w, w)


def _cast_rows_kernel(w_ref, o_ref, *, valid_blocks):
    w = w_ref[...].astype(o_ref.dtype)
    o_ref[...] = jnp.where(pl.program_id(0) < valid_blocks, w, jnp.zeros_like(w))


def _cast_rows(w, layer, rows_out, *, tr=PREP_ROW_BLOCK):
    rows, n = w.shape[1:]
    assert rows % tr == 0 and rows_out % tr == 0
    valid_blocks = rows // tr
    blk = _nbytes((tr, n), w.dtype) + _nbytes((tr, n), BF16)
    return pl.pallas_call(
        functools.partial(_cast_rows_kernel, valid_blocks=valid_blocks),
        grid=(rows_out // tr,),
        in_specs=[pl.BlockSpec((None, tr, n), lambda r: (layer, jnp.minimum(r, valid_blocks - 1), 0))],
        out_specs=pl.BlockSpec((tr, n), lambda r: (r, 0)),
        out_shape=jax.ShapeDtypeStruct((rows_out, n), BF16),
        compiler_params=pltpu.CompilerParams(
            dimension_semantics=("arbitrary",),
            vmem_limit_bytes=_vmem_limit(blk, 2 * _nbytes((tr, n), F32))),
        name="cast_rows",
    )(w)


def _gate_up_kernel(x_ref, wg_ref, wu_ref, o_ref, wb_ref, *, tn, d_ff):
    @pl.when(pl.program_id(1) == 0)
    def _():
        wb_ref[:, :tn] = wg_ref[...].astype(BF16)
        wb_ref[:, tn:] = wu_ref[...].astype(BF16)

    z = jnp.dot(x_ref[...], wb_ref[...], preferred_element_type=F32)
    h = _silu(z[:, :tn]) * z[:, tn:]
    col = pl.program_id(0) * tn + lax.broadcasted_iota(jnp.int32, h.shape, 1)
    o_ref[...] = jnp.where(col < d_ff, h, 0.0).astype(o_ref.dtype)


def _gate_up(x, w_gate, w_up, layer, *, tm=MM_TM, tn=FFN_TN):
    m, kdim = x.shape
    d_ff = w_gate.shape[2]
    n_out = pl.cdiv(d_ff, DOWN_TK) * DOWN_TK
    assert m % tm == 0 and n_out % tn == 0
    last_w_blk = pl.cdiv(d_ff, tn) - 1
    w_spec = pl.BlockSpec((None, kdim, tn), lambda j, i: (layer, 0, jnp.minimum(j, last_w_blk)))
    blk = _nbytes((tm, kdim), x.dtype) + 2 * _nbytes((kdim, tn), F32) + _nbytes((tm, tn), BF16)
    extra = _nbytes((kdim, 2 * tn), BF16) + 3 * _nbytes((tm, 2 * tn), F32)
    return pl.pallas_call(
        functools.partial(_gate_up_kernel, tn=tn, d_ff=d_ff),
        grid=(n_out // tn, m // tm),
        in_specs=[pl.BlockSpec((tm, kdim), lambda j, i: (i, 0)), w_spec, w_spec],
        out_specs=pl.BlockSpec((tm, tn), lambda j, i: (i, j)),
        out_shape=jax.ShapeDtypeStruct((m, n_out), BF16),
        scratch_shapes=[pltpu.VMEM((kdim, 2 * tn), BF16)],
        compiler_params=pltpu.CompilerParams(
            dimension_semantics=("arbitrary", "arbitrary"),
            vmem_limit_bytes=_vmem_limit(blk, extra)),
        name="gate_up",
    )(x, w_gate, w_up)


def _down_kernel(h_ref, w_ref, o_ref, acc_ref, *, nk):
    k = pl.program_id(2)
    part = jnp.dot(h_ref[...], w_ref[...], preferred_element_type=F32)
    if nk == 1:
        o_ref[...] = part
        return

    @pl.when(k == 0)
    def _():
        acc_ref[...] = part

    @pl.when(jnp.logical_and(k > 0, k < nk - 1))
    def _():
        acc_ref[...] += part

    @pl.when(k == nk - 1)
    def _():
        o_ref[...] = acc_ref[...] + part


def _down(h, w_down, *, tm=DOWN_TM, tn=DOWN_TN, tk=DOWN_TK):
    m, kp = h.shape
    n = w_down.shape[1]
    assert m % tm == 0 and n % tn == 0 and kp % tk == 0 and w_down.shape[0] == kp
    nk = kp // tk
    blk = _nbytes((tm, tk), h.dtype) + _nbytes((tk, tn), w_down.dtype) + _nbytes((tm, tn), F32)
    extra = 3 * _nbytes((tm, tn), F32)
    return pl.pallas_call(
        functools.partial(_down_kernel, nk=nk),
        grid=(m // tm, n // tn, nk),
        in_specs=[pl.BlockSpec((tm, tk), lambda i, j, k: (i, k)),
                  pl.BlockSpec((tk, tn), lambda i, j, k: (k, j))],
        out_specs=pl.BlockSpec((tm, tn), lambda i, j, k: (i, j)),
        out_shape=jax.ShapeDtypeStruct((m, n), F32),
        scratch_shapes=[pltpu.VMEM((tm, tn), F32)],
        compiler_params=pltpu.CompilerParams(
            dimension_semantics=("arbitrary", "arbitrary", "arbitrary"),
            vmem_limit_bytes=_vmem_limit(blk, extra)),
        name="down_proj",
    )(h, w_down)


def _split3(x):
    a = x.astype(BF16).astype(F32)
    r = x - a
    b = r.astype(BF16).astype(F32)
    return a, b, (r - b).astype(BF16).astype(F32)


def _gates_kernel(x_ref, w_ref, bias_ref, tri_ref, o_ref, aq_ref, ak_ref, carry_ref):
    @pl.when(pl.program_id(1) == 0)
    def _():
        carry_ref[...] = jnp.zeros_like(carry_ref)

    z = jnp.dot(x_ref[...], w_ref[...], preferred_element_type=F32)
    log_f = _log_sigmoid(z + bias_ref[...])
    c = jnp.dot(tri_ref[...], log_f, preferred_element_type=F32,
                precision=lax.Precision.HIGHEST) + carry_ref[0:1, :]
    rows = c.shape[0]
    carry_ref[...] = jnp.broadcast_to(c[rows - 1:rows, :], carry_ref.shape)
    lane = lax.broadcasted_iota(jnp.int32, c.shape, 1)
    o_ref[...] = jnp.where(lane < FOX_HEADS, c, z)

    c2 = c * LOG2_E
    for h in range(FOX_HEADS):
        c1, c2_, c3 = _split3(jnp.broadcast_to(c2[:, h:h + 1], c.shape))
        aq = jnp.where(lane == 0, c1, jnp.where(lane == 1, c2_, jnp.where(lane == 2, c3,
                       jnp.where(lane < 6, 1.0, 0.0))))
        ak = jnp.where(lane < 3, 1.0, jnp.where(lane == 3, -c1, jnp.where(lane == 4, -c2_,
                       jnp.where(lane == 5, -c3, 0.0))))
        aq_ref[0, h] = aq.astype(BF16)
        ak_ref[0, h] = ak.astype(BF16)


def _gates(x, w_small, bias, bsz, seq):
    m, kdim = x.shape
    steps = seq // GATE_ROWS
    tri = jnp.asarray(np.tril(np.ones((GATE_ROWS, GATE_ROWS), np.float32)))
    aug_shape = jax.ShapeDtypeStruct((bsz, FOX_HEADS, seq, V7X_LANES), BF16)
    aug_spec = pl.BlockSpec((1, FOX_HEADS, GATE_ROWS, V7X_LANES), lambda b, i: (b, 0, i, 0))
    blk = (_nbytes((GATE_ROWS, kdim), x.dtype) + _nbytes((kdim, V7X_LANES), w_small.dtype)
           + _nbytes((GATE_ROWS, GATE_ROWS), F32) + _nbytes((GATE_ROWS, V7X_LANES), F32)
           + 2 * _nbytes((FOX_HEADS, GATE_ROWS, V7X_LANES), BF16))
    return pl.pallas_call(
        _gates_kernel,
        grid=(bsz, steps),
        in_specs=[pl.BlockSpec((GATE_ROWS, kdim), lambda b, i: (b * steps + i, 0)),
                  pl.BlockSpec((kdim, V7X_LANES), lambda b, i: (0, 0)),
                  pl.BlockSpec((1, V7X_LANES), lambda b, i: (0, 0)),
                  pl.BlockSpec((GATE_ROWS, GATE_ROWS), lambda b, i: (0, 0))],
        out_specs=[pl.BlockSpec((GATE_ROWS, V7X_LANES), lambda b, i: (b * steps + i, 0)), aug_spec, aug_spec],
        out_shape=[jax.ShapeDtypeStruct((m, V7X_LANES), F32), aug_shape, aug_shape],
        scratch_shapes=[pltpu.VMEM((8, V7X_LANES), F32)],
        compiler_params=pltpu.CompilerParams(
            dimension_semantics=("arbitrary", "arbitrary"),
            vmem_limit_bytes=_vmem_limit(blk, 16 * _nbytes((GATE_ROWS, V7X_LANES), F32))),
        name="gates",
    )(x, w_small, bias, tri)


def _fox_kernel(q_ref, k_ref, v_ref, aq_ref, ak_ref, g_ref, o_ref, kp_ref, vp_ref, m_ref, acc_ref, *, tq, tk):
    qi = pl.program_id(2)
    n_heads = aq_ref.shape[1]
    d = q_ref.shape[2] // n_heads
    dp = acc_ref.shape[2]

    @pl.when(qi == 0)
    def _():
        lane = lax.broadcasted_iota(jnp.int32, (vp_ref.shape[1], V7X_LANES), 1)
        for g in range(n_heads):
            kp_ref[g, :, :d] = k_ref[0, :, g * d:(g + 1) * d]
            kp_ref[g, :, d:] = ak_ref[0, g]
            vp_ref[g, :, :d] = v_ref[0, :, g * d:(g + 1) * d]
            vp_ref[g, :, d:] = jnp.where(lane == 0, 1.0, 0.0).astype(BF16)

    qp = [jnp.concatenate([q_ref[0, :, g * d:(g + 1) * d], aq_ref[0, g]], axis=1) for g in range(n_heads)]
    m_ref[...] = jnp.full_like(m_ref, NEG_BIG)
    acc_ref[...] = jnp.zeros_like(acc_ref)

    def step(j, masked):
        start = pl.multiple_of(j * tk, tk)
        for g in range(n_heads):
            s = lax.dot_general(qp[g], kp_ref[g, pl.ds(start, tk), :], _NT, preferred_element_type=F32)
            if masked:
                row = lax.broadcasted_iota(jnp.int32, (tq, tk), 0)
                col = lax.broadcasted_iota(jnp.int32, (tq, tk), 1)
                s = jnp.where(row >= col, s, NEG_BIG)
            m_prev = m_ref[g]
            m_new = jnp.maximum(m_prev, jnp.max(s, axis=-1, keepdims=True))
            p = jnp.exp2(s - jnp.concatenate([m_new] * (tk // V7X_LANES), axis=1)).astype(BF16)
            alpha = jnp.exp2(m_prev - m_new)
            acc_ref[g] = (jnp.concatenate([alpha] * (dp // V7X_LANES), axis=1) * acc_ref[g]
                          + jnp.dot(p, vp_ref[g, pl.ds(start, tk), :], preferred_element_type=F32))
            m_ref[g] = m_new

    def body(i, carry):
        step(2 * i, False)
        step(2 * i + 1, False)
        return carry

    lax.fori_loop(0, lax.shift_right_logical(qi, 1), body, 0)

    @pl.when((qi & 1) == 1)
    def _():
        step(qi - 1, False)
        step(qi, True)

    @pl.when((qi & 1) == 0)
    def _():
        step(qi, True)

    for g in range(n_heads):
        acc = acc_ref[g]
        o = acc[:, :d] / acc[:, d:d + 1]
        o = o * lax.rsqrt(jnp.mean(o * o, axis=-1, keepdims=True) + EPS) * g_ref[g]
        o_ref[0, :, g * d:(g + 1) * d] = o.astype(o_ref.dtype)


def _fox(proj, aq, ak, norm_g, *, tq=FOX_TQ, tk=FOX_TK, group=FOX_GROUP):
    bsz, seq, _ = proj.shape
    d = FOX_HEAD_DIM
    dp = d + V7X_LANES
    gd = group * d
    n_groups = FOX_HEADS // group
    assert tq == tk and seq % tq == 0 and FOX_HEADS % group == 0
    blk = (2 * _nbytes((tq, gd), BF16) + 2 * _nbytes((seq, gd), BF16) + _nbytes((group, tq, V7X_LANES), BF16)
           + _nbytes((group, seq, V7X_LANES), BF16) + _nbytes((group, 8, d), F32))
    scratch = (2 * _nbytes((group, seq, dp), BF16) + _nbytes((group, tq, V7X_LANES), F32)
               + _nbytes((group, tq, dp), F32))
    return pl.pallas_call(
        functools.partial(_fox_kernel, tq=tq, tk=tk),
        grid=(bsz, n_groups, seq // tq),
        in_specs=[pl.BlockSpec((1, tq, gd), lambda b, h, i: (b, i, h)),
                  pl.BlockSpec((1, seq, gd), lambda b, h, i: (b, 0, n_groups + h)),
                  pl.BlockSpec((1, seq, gd), lambda b, h, i: (b, 0, 2 * n_groups + h)),
                  pl.BlockSpec((1, group, tq, V7X_LANES), lambda b, h, i: (b, h, i, 0)),
                  pl.BlockSpec((1, group, seq, V7X_LANES), lambda b, h, i: (b, h, 0, 0)),
                  pl.BlockSpec((group, 1, d), lambda b, h, i: (h, 0, 0))],
        out_specs=pl.BlockSpec((1, tq, gd), lambda b, h, i: (b, i, h)),
        out_shape=jax.ShapeDtypeStruct((bsz, seq, MIX_WIDTH), BF16),
        scratch_shapes=[pltpu.VMEM((group, seq, dp), BF16), pltpu.VMEM((group, seq, dp), BF16),
                        pltpu.VMEM((group, tq, V7X_LANES), F32), pltpu.VMEM((group, tq, dp), F32)],
        compiler_params=pltpu.CompilerParams(
            dimension_semantics=("arbitrary", "arbitrary", "arbitrary"),
            vmem_limit_bytes=_vmem_limit(blk, scratch + 8 * group * _nbytes((tq, tk), F32))),
        name="fox_attention",
    )(proj, proj, proj, aq, ak, norm_g)


def _gla_tables(chunk):
    idx = np.arange(chunk)
    masks = []
    half = chunk // 2
    while half >= 1:
        blk = idx // (2 * half)
        second = (idx // half) % 2 == 1
        masks.append(((blk[:, None] == blk[None, :]) & second[:, None] & ~second[None, :]).astype(np.float32))
        half //= 2
    return np.tril(np.ones((chunk, chunk), np.float32)), np.stack(masks, axis=0)


def _level_refs(b):
    chunk, width = b.shape
    refs = []
    half = chunk // 2
    while half >= 4:
        pieces = [jnp.broadcast_to(b[s + half - 1:s + half], (2 * half, width)) for s in range(0, chunk, 2 * half)]
        refs.append(pieces[0] if len(pieces) == 1 else jnp.concatenate(pieces, axis=0))
        half //= 2
    prev1 = pltpu.roll(b, 1, 0)
    prev2 = pltpu.roll(b, 2, 0)
    next1 = pltpu.roll(b, chunk - 1, 0)
    pos = lax.broadcasted_iota(jnp.int32, b.shape, 0) & 3
    refs.append(jnp.where(pos == 0, next1, jnp.where(pos == 1, b, jnp.where(pos == 2, prev1, prev2))))
    refs.append(jnp.where((pos & 1) == 0, b, prev1))
    return refs


def _gla_kernel(q_ref, k_ref, v_ref, gg_ref, gates_ref, wup_ref, bgk_ref, gn_ref, tri_ref, mask_ref,
                mixed_hbm_ref, o_ref, state_ref, la_ref, *, rows, chunk, scale):
    del mixed_hbm_ref
    dk, dv = GLA_KEY_DIM, GLA_VAL_DIM

    @pl.when(pl.program_id(1) == 0)
    def _():
        state_ref[...] = jnp.zeros_like(state_ref)

    pre = jnp.dot(gates_ref[0], wup_ref[...], preferred_element_type=F32,
                  precision=lax.Precision.HIGHEST) + bgk_ref[...]
    la_ref[...] = _log_sigmoid(pre) * (1.0 / GLA_GATE_NORMALIZER)

    n_levels = mask_ref.shape[0]
    eye = (lax.broadcasted_iota(jnp.int32, (chunk, chunk), 0)
           == lax.broadcasted_iota(jnp.int32, (chunk, chunk), 1))
    row_dk = lax.broadcasted_iota(jnp.int32, (chunk, dk), 0)
    ones_cols = jnp.ones((chunk, V7X_LANES), BF16)
    tri = tri_ref[...]

    def head_chunk(r0, h):
        ks = slice(h * dk, (h + 1) * dk)
        vs = slice(h * dv, (h + 1) * dv)
        g = la_ref[pl.ds(r0, chunk), ks]
        g_parts = jnp.concatenate([p.astype(BF16) for p in _split3(g)], axis=1)
        b3 = jnp.dot(tri, g_parts, preferred_element_type=F32)
        b = b3[:, :dk] + b3[:, dk:2 * dk] + b3[:, 2 * dk:]
        q = q_ref[0, pl.ds(r0, chunk), ks].astype(F32) * scale
        k = k_ref[0, pl.ds(r0, chunk), ks].astype(F32)
        v = v_ref[0, pl.ds(r0, chunk), vs]

        state = state_ref[h]
        o = jnp.dot((q * jnp.exp(b)).astype(BF16), state.astype(BF16), preferred_element_type=F32)

        scores = jnp.zeros((chunk, chunk), F32)
        for lvl, ref in enumerate(_level_refs(b)):
            qt = (q * jnp.exp(jnp.minimum(b - ref, 0.0))).astype(BF16)
            kt = (k * jnp.exp(jnp.minimum(ref - b, 0.0))).astype(BF16)
            scores = scores + mask_ref[lvl] * lax.dot_general(qt, kt, _NT, preferred_element_type=F32)
        scores = jnp.where(eye, jnp.sum(q * k, axis=-1, keepdims=True), scores)
        o = o + jnp.dot(scores.astype(BF16), v, preferred_element_type=F32)

        b_last = b[chunk - 1:chunk]
        k_dec = (k * jnp.exp(b_last - b)).astype(BF16)
        e1, e2, e3 = _split3(jnp.broadcast_to(jnp.exp(b_last), (chunk, dk)))
        e_rows = jnp.where(row_dk == 0, e1, jnp.where(row_dk == 1, e2, jnp.where(row_dk == 2, e3, 0.0)))
        dec_col = lax.dot_general(e_rows.astype(BF16), ones_cols, _TN, preferred_element_type=F32)
        decay = jnp.concatenate([dec_col] * (dv // V7X_LANES), axis=1)
        state_ref[h] = state * decay + lax.dot_general(k_dec, v, _TN, preferred_element_type=F32)

        o = o * lax.rsqrt(jnp.mean(o * o, axis=-1, keepdims=True) + EPS) * gn_ref[...]
        gate = _silu(gg_ref[0, pl.ds(r0, chunk), vs].astype(F32))
        o_ref[0, pl.ds(r0, chunk), vs] = (o * gate).astype(o_ref.dtype)

    assert n_levels == chunk.bit_length() - 1

    def chunk_body(c, carry):
        r0 = pl.multiple_of(c * chunk, chunk)
        for h in range(GLA_HEADS):
            head_chunk(r0, h)
        return carry

    lax.fori_loop(0, rows // chunk, chunk_body, 0)


def _gla(proj, gates, w_up_pad, b_gk, norm_g, mixed, *, rows=GLA_ROWS, chunk=GLA_CHUNK):
    bsz, seq, _ = proj.shape
    kw, vw = GLA_KEY_WIDTH, GLA_WIDTH
    assert seq % rows == 0 and rows % chunk == 0 and vw == 2 * kw and FOX_WIDTH == vw
    tri, masks = _gla_tables(chunk)
    blk = (2 * _nbytes((rows, kw), BF16) + 3 * _nbytes((rows, vw), BF16) + _nbytes((rows, V7X_LANES), F32)
           + _nbytes((V7X_LANES, kw), F32) + _nbytes(masks.shape, F32))
    scratch = _nbytes((GLA_HEADS, GLA_KEY_DIM, GLA_VAL_DIM), F32) + _nbytes((rows, kw), F32)
    return pl.pallas_call(
        functools.partial(_gla_kernel, rows=rows, chunk=chunk, scale=GLA_KEY_DIM ** -0.5),
        grid=(bsz, seq // rows),
        in_specs=[pl.BlockSpec((1, rows, kw), lambda b, t: (b, t, 0)),
                  pl.BlockSpec((1, rows, kw), lambda b, t: (b, t, 1)),
                  pl.BlockSpec((1, rows, vw), lambda b, t: (b, t, 1)),
                  pl.BlockSpec((1, rows, vw), lambda b, t: (b, t, 2)),
                  pl.BlockSpec((1, rows, V7X_LANES), lambda b, t: (b, t, 0)),
                  pl.BlockSpec((V7X_LANES, kw), lambda b, t: (0, 0)),
                  pl.BlockSpec((1, kw), lambda b, t: (0, 0)),
                  pl.BlockSpec((1, GLA_VAL_DIM), lambda b, t: (0, 0)),
                  pl.BlockSpec(tri.shape, lambda b, t: (0, 0)),
                  pl.BlockSpec(masks.shape, lambda b, t: (0, 0, 0)),
                  pl.BlockSpec(memory_space=pl.ANY)],
        out_specs=pl.BlockSpec((1, rows, vw), lambda b, t: (b, t, 1)),
        out_shape=jax.ShapeDtypeStruct(mixed.shape, mixed.dtype),
        scratch_shapes=[pltpu.VMEM((GLA_HEADS, GLA_KEY_DIM, GLA_VAL_DIM), F32), pltpu.VMEM((rows, kw), F32)],
        input_output_aliases={10: 0},
        compiler_params=pltpu.CompilerParams(
            dimension_semantics=("arbitrary", "arbitrary"),
            vmem_limit_bytes=_vmem_limit(blk, scratch + (8 << 20))),
        name="gla",
    )(proj, proj, proj, proj, gates, w_up_pad, b_gk.reshape(1, -1), norm_g.reshape(1, -1),
      jnp.asarray(tri, BF16), jnp.asarray(masks), mixed)


IN_F_LO = 3 * FOX_WIDTH
IN_G_LO = IN_F_LO + FOX_HEADS
IN_LOW_LO = IN_G_LO + 2 * GLA_KEY_WIDTH + 2 * GLA_WIDTH


def _prep_layer(w_in, layer, b_f, w_gk_up):
    d_model = w_in.shape[1]
    pad = V7X_LANES - FOX_HEADS - GLA_GATE_RANK
    w_small = jnp.concatenate([w_in[layer, :, IN_F_LO:IN_G_LO], w_in[layer, :, IN_LOW_LO:],
                               jnp.zeros((d_model, pad), F32)], axis=1).astype(BF16)
    gate_bias = jnp.concatenate([b_f, jnp.zeros((V7X_LANES - FOX_HEADS,), F32)]).reshape(1, V7X_LANES)
    w_up_pad = jnp.zeros((V7X_LANES, GLA_KEY_WIDTH), F32).at[FOX_HEADS:FOX_HEADS + GLA_GATE_RANK].set(w_gk_up)
    return w_small, gate_bias, w_up_pad


def kernel(x, ln_in_g, ln_in_b, w_in, b_f, w_gk_up, b_gk, fox_norm_g, gla_norm_g, w_out, ln1_g, ln1_b,
           w_gate, w_up, w_down, ln2_g, ln2_b):
    bsz, seq, d_model = x.shape
    depth = w_in.shape[0]
    alpha = (2 * depth) ** 0.25
    rows = bsz * seq

    x32, x16 = _layer_norm(x.reshape(rows, d_model), None, ln_in_g, ln_in_b)
    for l in range(depth):
        w_small, gate_bias, w_up_pad = _prep_layer(w_in, l, b_f[l], w_gk_up[l])
        w_gla = _shift_cast(w_in, l, IN_G_LO, IN_LOW_LO - IN_G_LO)[None]

        proj_fox = _matmul(x16, w_in, l, IN_F_LO, BF16, scaled_cols=FOX_WIDTH,
                           scale=FOX_HEAD_DIM ** -0.5 * LOG2_E).reshape(bsz, seq, -1)
        proj_gla = _matmul(x16, w_gla, 0, w_gla.shape[2], BF16).reshape(bsz, seq, -1)
        gates, aug_q, aug_k = _gates(x16, w_small, gate_bias, bsz, seq)
        gates = gates.reshape(bsz, seq, V7X_LANES)

        mixed = _fox(proj_fox, aug_q, aug_k, fox_norm_g[l].reshape(FOX_HEADS, 1, FOX_HEAD_DIM))
        mixed = _gla(proj_gla, gates, w_up_pad, b_gk[l], gla_norm_g[l], mixed)

        mix = _matmul(mixed.reshape(rows, MIX_WIDTH), w_out, l, d_model, F32)
        x32, x16 = _layer_norm(x32, mix, ln1_g[l], ln1_b[l], alpha=alpha)

        hidden = _gate_up(x16, w_gate, w_up, l)
        ffn = _down(hidden, _cast_rows(w_down, l, hidden.shape[1]))
        last = l == depth - 1
        outs = _layer_norm(x32, ffn, ln2_g[l], ln2_b[l], alpha=alpha, want_bf16=not last)
        if last:
            x32, = outs
        else:
            x32, x16 = outs
    return x32.reshape(bsz, seq, d_model)
```

```python
import functools

import numpy as np
import jax
import jax.numpy as jnp
from jax import lax
from jax.experimental import pallas as pl
from jax.experimental.pallas import tpu as pltpu

F32 = jnp.float32
BF16 = jnp.bfloat16

V7X_VMEM_BYTES = 64 * 2**20
V7X_LANES = 128
V7X_MXU_DIM = 256
F32_SUBLANES = 8

FOX_HEADS = 16
FOX_HEAD_DIM = 128
FOX_WIDTH = FOX_HEADS * FOX_HEAD_DIM
GLA_HEADS = 4
GLA_KEY_DIM = 256
GLA_VAL_DIM = 512
GLA_KEY_WIDTH = GLA_HEADS * GLA_KEY_DIM
GLA_WIDTH = GLA_HEADS * GLA_VAL_DIM
GLA_GATE_RANK = 16
GLA_GATE_NORMALIZER = 16.0
MIX_WIDTH = FOX_WIDTH + GLA_WIDTH
EPS = 1e-5
IN_F_LO = 3 * FOX_WIDTH
IN_G_LO = IN_F_LO + FOX_HEADS
IN_LOW_LO = IN_G_LO + 2 * GLA_KEY_WIDTH + 2 * GLA_WIDTH

LN_ROWS = 256
MM_TM = 512
MM_TN = 1024
FFN_TN = 512
MM_T_CHUNK = 512
PREP_ROW_BLOCK = 256
DOWN_TM = 1024
DOWN_TN = 1024
DOWN_TK = 2816
GATE_ROWS = 512
FOX_TQ = 512
FOX_TK = 512
FOX_GROUP = 4
GLA_ROWS = 512
GLA_CHUNK = 64
NEG_BIG = -1e30
LOG2_E = 1.4426950408889634

_NT = (((1,), (1,)), ((), ()))
_TN = (((0,), (0,)), ((), ()))


def _vmem_limit(block_bytes, extra_bytes):
    need = 2 * block_bytes + extra_bytes + (4 << 20)
    return int(min(need, V7X_VMEM_BYTES - (6 << 20)))


def _nbytes(shape, dtype):
    return int(np.prod(shape)) * jnp.dtype(dtype).itemsize


def _log_sigmoid(x):
    return jnp.minimum(x, 0.0) - jnp.log1p(jnp.exp(-jnp.abs(x)))


def _silu(x):
    return x * (1.0 / (1.0 + jnp.exp(-x)))


def _ln_math(t, g, b):
    mu = jnp.mean(t, axis=-1, keepdims=True)
    tc = t - mu
    var = jnp.mean(tc * tc, axis=-1, keepdims=True)
    return tc * lax.rsqrt(var + EPS) * g + b


def _ln_kernel(*refs, alpha, has_res, want_f32, want_bf16):
    if has_res:
        x_ref, y_ref, g_ref, b_ref = refs[:4]
        outs = refs[4:]
        t = alpha * x_ref[...] + y_ref[...]
    else:
        x_ref, g_ref, b_ref = refs[:3]
        outs = refs[3:]
        t = x_ref[...]
    y = _ln_math(t, g_ref[...], b_ref[...])
    i = 0
    if want_f32:
        outs[i][...] = y
        i += 1
    if want_bf16:
        outs[i][...] = y.astype(BF16)


def _layer_norm(x, res, g, b, *, alpha=1.0, want_f32=True, want_bf16=True):
    rows, d = x.shape
    assert rows % LN_ROWS == 0
    row_spec = pl.BlockSpec((LN_ROWS, d), lambda i: (i, 0))
    vec_spec = pl.BlockSpec((1, d), lambda i: (0, 0))
    ins = [x] + ([res] if res is not None else []) + [g.reshape(1, d), b.reshape(1, d)]
    in_specs = [row_spec] * (len(ins) - 2) + [vec_spec, vec_spec]
    out_shape, out_specs = [], []
    if want_f32:
        out_shape.append(jax.ShapeDtypeStruct((rows, d), F32))
        out_specs.append(row_spec)
    if want_bf16:
        out_shape.append(jax.ShapeDtypeStruct((rows, d), BF16))
        out_specs.append(row_spec)
    blk = _nbytes((LN_ROWS, d), F32) * (len(ins) - 2 + len(out_shape))
    return pl.pallas_call(
        functools.partial(_ln_kernel, alpha=alpha, has_res=res is not None,
                          want_f32=want_f32, want_bf16=want_bf16),
        grid=(rows // LN_ROWS,),
        in_specs=in_specs, out_specs=out_specs, out_shape=out_shape,
        compiler_params=pltpu.CompilerParams(
            dimension_semantics=("arbitrary",),
            vmem_limit_bytes=_vmem_limit(blk, 4 * _nbytes((LN_ROWS, d), F32))),
        name="layer_norm",
    )(*ins)


def _mm_kernel(x_ref, w_ref, o_ref, *scratch, cast, scaled_tiles, scale):
    if cast:
        wb_ref, = scratch

        @pl.when(pl.program_id(1) == 0)
        def _():
            wb_ref[...] = w_ref[...].astype(BF16)
        w = wb_ref[...]
    else:
        w = w_ref[...]
    acc = jnp.dot(x_ref[...], w, preferred_element_type=F32)
    if scaled_tiles:
        acc = acc * jnp.where(pl.program_id(0) < scaled_tiles, scale, 1.0)
    o_ref[...] = acc.astype(o_ref.dtype)


def _matmul(x, w, layer, n_cols, out_dtype, *, tm=MM_TM, tn=MM_TN, scaled_cols=0, scale=1.0):
    m, kdim = x.shape
    assert m % tm == 0 and n_cols % tn == 0 and w.shape[1] == kdim and scaled_cols % tn == 0
    cast = w.dtype != BF16
    blk = _nbytes((tm, kdim), x.dtype) + _nbytes((kdim, tn), w.dtype) + _nbytes((tm, tn), out_dtype)
    extra = _nbytes((kdim, tn), BF16) + 2 * _nbytes((tm, tn), F32)
    return pl.pallas_call(
        functools.partial(_mm_kernel, cast=cast, scaled_tiles=scaled_cols // tn, scale=scale),
        grid=(n_cols // tn, m // tm),
        in_specs=[pl.BlockSpec((tm, kdim), lambda j, i: (i, 0)),
                  pl.BlockSpec((None, kdim, tn), lambda j, i: (layer, 0, j))],
        out_specs=pl.BlockSpec((tm, tn), lambda j, i: (i, j)),
        out_shape=jax.ShapeDtypeStruct((m, n_cols), out_dtype),
        scratch_shapes=[pltpu.VMEM((kdim, tn), BF16)] if cast else [],
        compiler_params=pltpu.CompilerParams(
            dimension_semantics=("arbitrary", "arbitrary"),
            vmem_limit_bytes=_vmem_limit(blk, extra)),
        name="matmul",
    )(x, w)


def _mm_wt_kernel(x_ref, w_ref, o_ref, wb_ref, *, scaled_tiles, scale, chunk):
    @pl.when(pl.program_id(1) == 0)
    def _():
        for c in range(0, wb_ref.shape[0], chunk):
            wb_ref[c:c + chunk, :] = w_ref[0, :, c:c + chunk].T.astype(BF16)

    acc = jnp.dot(x_ref[...], wb_ref[...], preferred_element_type=F32)
    if scaled_tiles:
        acc = acc * jnp.where(pl.program_id(0) < scaled_tiles, scale, 1.0)
    o_ref[...] = acc.astype(o_ref.dtype)


def _matmul_wt(x, w_t, layer, n_tiles, row_start, out_dtype, *, tm=MM_TM, tn=MM_TN, scaled_cols=0, scale=1.0):
    m, kdim = x.shape
    assert m % tm == 0 and w_t.shape[2] == kdim and scaled_cols % tn == 0 and kdim % MM_T_CHUNK == 0
    blk = _nbytes((tm, kdim), x.dtype) + _nbytes((tn, kdim), w_t.dtype) + _nbytes((tm, tn), out_dtype)
    extra = _nbytes((kdim, tn), BF16) + 2 * _nbytes((tm, tn), F32) + 3 * _nbytes((tn, MM_T_CHUNK), F32)
    return pl.pallas_call(
        functools.partial(_mm_wt_kernel, scaled_tiles=scaled_cols // tn, scale=scale, chunk=MM_T_CHUNK),
        grid=(n_tiles, m // tm),
        in_specs=[pl.BlockSpec((tm, kdim), lambda j, i: (i, 0)),
                  pl.BlockSpec((pl.Element(1), pl.Element(tn), pl.Element(kdim)),
                               lambda j, i: (layer, row_start(j), 0))],
        out_specs=pl.BlockSpec((tm, tn), lambda j, i: (i, j)),
        out_shape=jax.ShapeDtypeStruct((m, n_tiles * tn), out_dtype),
        scratch_shapes=[pltpu.VMEM((kdim, tn), BF16)],
        compiler_params=pltpu.CompilerParams(
            dimension_semantics=("arbitrary", "arbitrary"),
            vmem_limit_bytes=_vmem_limit(blk, extra)),
        name="matmul_wt",
    )(x, w_t)


def _cast_rows_kernel(w_ref, o_ref, *, valid_blocks):
    w = w_ref[...].astype(o_ref.dtype)
    o_ref[...] = jnp.where(pl.program_id(0) < valid_blocks, w, jnp.zeros_like(w))


def _cast_rows(w, layer, rows_out, *, tr=PREP_ROW_BLOCK):
    rows, n = w.shape[1:]
    assert rows % tr == 0 and rows_out % tr == 0
    valid_blocks = rows // tr
    blk = _nbytes((tr, n), w.dtype) + _nbytes((tr, n), BF16)
    return pl.pallas_call(
        functools.partial(_cast_rows_kernel, valid_blocks=valid_blocks),
        grid=(rows_out // tr,),
        in_specs=[pl.BlockSpec((None, tr, n), lambda r: (layer, jnp.minimum(r, valid_blocks - 1), 0))],
        out_specs=pl.BlockSpec((tr, n), lambda r: (r, 0)),
        out_shape=jax.ShapeDtypeStruct((rows_out, n), BF16),
        compiler_params=pltpu.CompilerParams(
            dimension_semantics=("arbitrary",),
            vmem_limit_bytes=_vmem_limit(blk, 2 * _nbytes((tr, n), F32))),
        name="cast_rows",
    )(w)


def _gate_up_kernel(x_ref, wg_ref, wu_ref, o_ref, wb_ref, *, tn, d_ff):
    @pl.when(pl.program_id(1) == 0)
    def _():
        wb_ref[:, :tn] = wg_ref[...].astype(BF16)
        wb_ref[:, tn:] = wu_ref[...].astype(BF16)

    z = jnp.dot(x_ref[...], wb_ref[...], preferred_element_type=F32)
    h = _silu(z[:, :tn]) * z[:, tn:]
    col = pl.program_id(0) * tn + lax.broadcasted_iota(jnp.int32, h.shape, 1)
    o_ref[...] = jnp.where(col < d_ff, h, 0.0).astype(o_ref.dtype)


def _gate_up(x, w_gate, w_up, layer, *, tm=MM_TM, tn=FFN_TN):
    m, kdim = x.shape
    d_ff = w_gate.shape[2]
    n_out = pl.cdiv(d_ff, DOWN_TK) * DOWN_TK
    assert m % tm == 0 and n_out % tn == 0
    last_w_blk = pl.cdiv(d_ff, tn) - 1
    w_spec = pl.BlockSpec((None, kdim, tn), lambda j, i: (layer, 0, jnp.minimum(j, last_w_blk)))
    blk = _nbytes((tm, kdim), x.dtype) + 2 * _nbytes((kdim, tn), F32) + _nbytes((tm, tn), BF16)
    extra = _nbytes((kdim, 2 * tn), BF16) + 3 * _nbytes((tm, 2 * tn), F32)
    return pl.pallas_call(
        functools.partial(_gate_up_kernel, tn=tn, d_ff=d_ff),
        grid=(n_out // tn, m // tm),
        in_specs=[pl.BlockSpec((tm, kdim), lambda j, i: (i, 0)), w_spec, w_spec],
        out_specs=pl.BlockSpec((tm, tn), lambda j, i: (i, j)),
        out_shape=jax.ShapeDtypeStruct((m, n_out), BF16),
        scratch_shapes=[pltpu.VMEM((kdim, 2 * tn), BF16)],
        compiler_params=pltpu.CompilerParams(
            dimension_semantics=("arbitrary", "arbitrary"),
            vmem_limit_bytes=_vmem_limit(blk, extra)),
        name="gate_up",
    )(x, w_gate, w_up)


def _down_kernel(h_ref, w_ref, o_ref, acc_ref, *, nk):
    k = pl.program_id(2)
    part = jnp.dot(h_ref[...], w_ref[...], preferred_element_type=F32)
    if nk == 1:
        o_ref[...] = part
        return

    @pl.when(k == 0)
    def _():
        acc_ref[...] = part

    @pl.when(jnp.logical_and(k > 0, k < nk - 1))
    def _():
        acc_ref[...] += part

    @pl.when(k == nk - 1)
    def _():
        o_ref[...] = acc_ref[...] + part


def _down(h, w_down, *, tm=DOWN_TM, tn=DOWN_TN, tk=DOWN_TK):
    m, kp = h.shape
    n = w_down.shape[1]
    assert m % tm == 0 and n % tn == 0 and kp % tk == 0 and w_down.shape[0] == kp
    nk = kp // tk
    blk = _nbytes((tm, tk), h.dtype) + _nbytes((tk, tn), w_down.dtype) + _nbytes((tm, tn), F32)
    extra = 3 * _nbytes((tm, tn), F32)
    return pl.pallas_call(
        functools.partial(_down_kernel, nk=nk),
        grid=(m // tm, n // tn, nk),
        in_specs=[pl.BlockSpec((tm, tk), lambda i, j, k: (i, k)),
                  pl.BlockSpec((tk, tn), lambda i, j, k: (k, j))],
        out_specs=pl.BlockSpec((tm, tn), lambda i, j, k: (i, j)),
        out_shape=jax.ShapeDtypeStruct((m, n), F32),
        scratch_shapes=[pltpu.VMEM((tm, tn), F32)],
        compiler_params=pltpu.CompilerParams(
            dimension_semantics=("arbitrary", "arbitrary", "arbitrary"),
            vmem_limit_bytes=_vmem_limit(blk, extra)),
        name="down_proj",
    )(h, w_down)


def _split3(x):
    a = x.astype(BF16).astype(F32)
    r = x - a
    b = r.astype(BF16).astype(F32)
    return a, b, (r - b).astype(BF16).astype(F32)


def _gates_kernel(x_ref, wf_ref, wl_ref, bias_ref, tri_ref, o_ref, aq_ref, ak_ref, carry_ref):
    @pl.when(pl.program_id(1) == 0)
    def _():
        carry_ref[...] = jnp.zeros_like(carry_ref)

    wf, wl = wf_ref[0], wl_ref[0]
    pad = jnp.zeros((V7X_LANES - wf.shape[0] - wl.shape[0], wf.shape[1]), F32)
    w = jnp.concatenate([wf, wl, pad], axis=0).astype(BF16)
    z = lax.dot_general(x_ref[...], w, _NT, preferred_element_type=F32)
    log_f = _log_sigmoid(z + bias_ref[...])
    c = jnp.dot(tri_ref[...], log_f, preferred_element_type=F32,
                precision=lax.Precision.HIGHEST) + carry_ref[0:1, :]
    rows = c.shape[0]
    carry_ref[...] = jnp.broadcast_to(c[rows - 1:rows, :], carry_ref.shape)
    lane = lax.broadcasted_iota(jnp.int32, c.shape, 1)
    o_ref[...] = jnp.where(lane < FOX_HEADS, c, z)

    c2 = c * LOG2_E
    for h in range(FOX_HEADS):
        c1, c2_, c3 = _split3(jnp.broadcast_to(c2[:, h:h + 1], c.shape))
        aq = jnp.where(lane == 0, c1, jnp.where(lane == 1, c2_, jnp.where(lane == 2, c3,
                       jnp.where(lane < 6, 1.0, 0.0))))
        ak = jnp.where(lane < 3, 1.0, jnp.where(lane == 3, -c1, jnp.where(lane == 4, -c2_,
                       jnp.where(lane == 5, -c3, 0.0))))
        aq_ref[0, h] = aq.astype(BF16)
        ak_ref[0, h] = ak.astype(BF16)


def _gates(x, w_in_t, layer, bias, bsz, seq):
    m, kdim = x.shape
    steps = seq // GATE_ROWS
    tri = jnp.asarray(np.tril(np.ones((GATE_ROWS, GATE_ROWS), np.float32)))
    aug_shape = jax.ShapeDtypeStruct((bsz, FOX_HEADS, seq, V7X_LANES), BF16)
    aug_spec = pl.BlockSpec((1, FOX_HEADS, GATE_ROWS, V7X_LANES), lambda b, i: (b, 0, i, 0))
    assert IN_F_LO % F32_SUBLANES == 0 and IN_LOW_LO % F32_SUBLANES == 0

    def rows_spec(n_rows, row0):
        return pl.BlockSpec((pl.Element(1), pl.Element(n_rows), pl.Element(kdim)), lambda b, i: (layer, row0, 0))

    blk = (_nbytes((GATE_ROWS, kdim), x.dtype) + _nbytes((V7X_LANES, kdim), F32)
           + _nbytes((GATE_ROWS, GATE_ROWS), F32) + _nbytes((GATE_ROWS, V7X_LANES), F32)
           + 2 * _nbytes((FOX_HEADS, GATE_ROWS, V7X_LANES), BF16))
    return pl.pallas_call(
        _gates_kernel,
        grid=(bsz, steps),
        in_specs=[pl.BlockSpec((GATE_ROWS, kdim), lambda b, i: (b * steps + i, 0)),
                  rows_spec(FOX_HEADS, IN_F_LO), rows_spec(GLA_GATE_RANK, IN_LOW_LO),
                  pl.BlockSpec((1, V7X_LANES), lambda b, i: (0, 0)),
                  pl.BlockSpec((GATE_ROWS, GATE_ROWS), lambda b, i: (0, 0))],
        out_specs=[pl.BlockSpec((GATE_ROWS, V7X_LANES), lambda b, i: (b * steps + i, 0)), aug_spec, aug_spec],
        out_shape=[jax.ShapeDtypeStruct((m, V7X_LANES), F32), aug_shape, aug_shape],
        scratch_shapes=[pltpu.VMEM((8, V7X_LANES), F32)],
        compiler_params=pltpu.CompilerParams(
            dimension_semantics=("arbitrary", "arbitrary"),
            vmem_limit_bytes=_vmem_limit(blk, 16 * _nbytes((GATE_ROWS, V7X_LANES), F32))),
        name="gates",
    )(x, w_in_t, w_in_t, bias, tri)


def _fox_kernel(q_ref, k_ref, v_ref, aq_ref, ak_ref, g_ref, o_ref, kp_ref, vp_ref, m_ref, acc_ref, *, tq, tk):
    qi = pl.program_id(2)
    n_heads = aq_ref.shape[1]
    d = q_ref.shape[2] // n_heads
    dp = acc_ref.shape[2]

    @pl.when(qi == 0)
    def _():
        lane = lax.broadcasted_iota(jnp.int32, (vp_ref.shape[1], V7X_LANES), 1)
        for g in range(n_heads):
            kp_ref[g, :, :d] = k_ref[0, :, g * d:(g + 1) * d]
            kp_ref[g, :, d:] = ak_ref[0, g]
            vp_ref[g, :, :d] = v_ref[0, :, g * d:(g + 1) * d]
            vp_ref[g, :, d:] = jnp.where(lane == 0, 1.0, 0.0).astype(BF16)

    qp = [jnp.concatenate([q_ref[0, :, g * d:(g + 1) * d], aq_ref[0, g]], axis=1) for g in range(n_heads)]
    m_ref[...] = jnp.full_like(m_ref, NEG_BIG)
    acc_ref[...] = jnp.zeros_like(acc_ref)

    def step(j, masked):
        start = pl.multiple_of(j * tk, tk)
        for g in range(n_heads):
            s = lax.dot_general(qp[g], kp_ref[g, pl.ds(start, tk), :], _NT, preferred_element_type=F32)
            if masked:
                row = lax.broadcasted_iota(jnp.int32, (tq, tk), 0)
                col = lax.broadcasted_iota(jnp.int32, (tq, tk), 1)
                s = jnp.where(row >= col, s, NEG_BIG)
            m_prev = m_ref[g]
            m_new = jnp.maximum(m_prev, jnp.max(s, axis=-1, keepdims=True))
            p = jnp.exp2(s - jnp.concatenate([m_new] * (tk // V7X_LANES), axis=1)).astype(BF16)
            alpha = jnp.exp2(m_prev - m_new)
            acc_ref[g] = (jnp.concatenate([alpha] * (dp // V7X_LANES), axis=1) * acc_ref[g]
                          + jnp.dot(p, vp_ref[g, pl.ds(start, tk), :], preferred_element_type=F32))
            m_ref[g] = m_new

    def body(i, carry):
        step(2 * i, False)
        step(2 * i + 1, False)
        return carry

    lax.fori_loop(0, lax.shift_right_logical(qi, 1), body, 0)

    @pl.when((qi & 1) == 1)
    def _():
        step(qi - 1, False)
        step(qi, True)

    @pl.when((qi & 1) == 0)
    def _():
        step(qi, True)

    for g in range(n_heads):
        acc = acc_ref[g]
        o = acc[:, :d] / acc[:, d:d + 1]
        o = o * lax.rsqrt(jnp.mean(o * o, axis=-1, keepdims=True) + EPS) * g_ref[g]
        o_ref[0, :, g * d:(g + 1) * d] = o.astype(o_ref.dtype)


def _fox(proj, aq, ak, norm_g, *, tq=FOX_TQ, tk=FOX_TK, group=FOX_GROUP):
    bsz, seq, _ = proj.shape
    d = FOX_HEAD_DIM
    dp = d + V7X_LANES
    gd = group * d
    n_groups = FOX_HEADS // group
    assert tq == tk and seq % tq == 0 and FOX_HEADS % group == 0
    blk = (2 * _nbytes((tq, gd), BF16) + 2 * _nbytes((seq, gd), BF16) + _nbytes((group, tq, V7X_LANES), BF16)
           + _nbytes((group, seq, V7X_LANES), BF16) + _nbytes((group, 8, d), F32))
    scratch = (2 * _nbytes((group, seq, dp), BF16) + _nbytes((group, tq, V7X_LANES), F32)
               + _nbytes((group, tq, dp), F32))
    return pl.pallas_call(
        functools.partial(_fox_kernel, tq=tq, tk=tk),
        grid=(bsz, n_groups, seq // tq),
        in_specs=[pl.BlockSpec((1, tq, gd), lambda b, h, i: (b, i, h)),
                  pl.BlockSpec((1, seq, gd), lambda b, h, i: (b, 0, n_groups + h)),
                  pl.BlockSpec((1, seq, gd), lambda b, h, i: (b, 0, 2 * n_groups + h)),
                  pl.BlockSpec((1, group, tq, V7X_LANES), lambda b, h, i: (b, h, i, 0)),
                  pl.BlockSpec((1, group, seq, V7X_LANES), lambda b, h, i: (b, h, 0, 0)),
                  pl.BlockSpec((group, 1, d), lambda b, h, i: (h, 0, 0))],
        out_specs=pl.BlockSpec((1, tq, gd), lambda b, h, i: (b, i, h)),
        out_shape=jax.ShapeDtypeStruct((bsz, seq, MIX_WIDTH), BF16),
        scratch_shapes=[pltpu.VMEM((group, seq, dp), BF16), pltpu.VMEM((group, seq, dp), BF16),
                        pltpu.VMEM((group, tq, V7X_LANES), F32), pltpu.VMEM((group, tq, dp), F32)],
        compiler_params=pltpu.CompilerParams(
            dimension_semantics=("arbitrary", "arbitrary", "arbitrary"),
            vmem_limit_bytes=_vmem_limit(blk, scratch + 8 * group * _nbytes((tq, tk), F32))),
        name="fox_attention",
    )(proj, proj, proj, aq, ak, norm_g)


def _gla_tables(chunk):
    idx = np.arange(chunk)
    masks = []
    half = chunk // 2
    while half >= 1:
        blk = idx // (2 * half)
        second = (idx // half) % 2 == 1
        masks.append(((blk[:, None] == blk[None, :]) & second[:, None] & ~second[None, :]).astype(np.float32))
        half //= 2
    return np.tril(np.ones((chunk, chunk), np.float32)), np.stack(masks, axis=0)


def _level_refs(b):
    chunk, width = b.shape
    refs = []
    half = chunk // 2
    while half >= 4:
        pieces = [jnp.broadcast_to(b[s + half - 1:s + half], (2 * half, width)) for s in range(0, chunk, 2 * half)]
        refs.append(pieces[0] if len(pieces) == 1 else jnp.concatenate(pieces, axis=0))
        half //= 2
    prev1 = pltpu.roll(b, 1, 0)
    prev2 = pltpu.roll(b, 2, 0)
    next1 = pltpu.roll(b, chunk - 1, 0)
    pos = lax.broadcasted_iota(jnp.int32, b.shape, 0) & 3
    refs.append(jnp.where(pos == 0, next1, jnp.where(pos == 1, b, jnp.where(pos == 2, prev1, prev2))))
    refs.append(jnp.where((pos & 1) == 0, b, prev1))
    return refs


def _gla_kernel(q_ref, k_ref, v_ref, gg_ref, gates_ref, wup_ref, bgk_ref, gn_ref, tri_ref, mask_ref,
                mixed_hbm_ref, o_ref, state_ref, la_ref, *, rows, chunk, scale):
    del mixed_hbm_ref
    dk, dv = GLA_KEY_DIM, GLA_VAL_DIM

    @pl.when(pl.program_id(1) == 0)
    def _():
        state_ref[...] = jnp.zeros_like(state_ref)

    pre = jnp.dot(gates_ref[0], wup_ref[...], preferred_element_type=F32,
                  precision=lax.Precision.HIGHEST) + bgk_ref[...]
    la_ref[...] = _log_sigmoid(pre) * (1.0 / GLA_GATE_NORMALIZER)

    n_levels = mask_ref.shape[0]
    eye = (lax.broadcasted_iota(jnp.int32, (chunk, chunk), 0)
           == lax.broadcasted_iota(jnp.int32, (chunk, chunk), 1))
    row_dk = lax.broadcasted_iota(jnp.int32, (chunk, dk), 0)
    ones_cols = jnp.ones((chunk, V7X_LANES), BF16)
    tri = tri_ref[...]

    def head_chunk(r0, h):
        ks = slice(h * dk, (h + 1) * dk)
        vs = slice(h * dv, (h + 1) * dv)
        g = la_ref[pl.ds(r0, chunk), ks]
        g_parts = jnp.concatenate([p.astype(BF16) for p in _split3(g)], axis=1)
        b3 = jnp.dot(tri, g_parts, preferred_element_type=F32)
        b = b3[:, :dk] + b3[:, dk:2 * dk] + b3[:, 2 * dk:]
        q = q_ref[0, pl.ds(r0, chunk), ks].astype(F32) * scale
        k = k_ref[0, pl.ds(r0, chunk), ks].astype(F32)
        v = v_ref[0, pl.ds(r0, chunk), vs]

        state = state_ref[h]
        o = jnp.dot((q * jnp.exp(b)).astype(BF16), state.astype(BF16), preferred_element_type=F32)

        scores = jnp.zeros((chunk, chunk), F32)
        for lvl, ref in enumerate(_level_refs(b)):
            qt = (q * jnp.exp(jnp.minimum(b - ref, 0.0))).astype(BF16)
            kt = (k * jnp.exp(jnp.minimum(ref - b, 0.0))).astype(BF16)
            scores = scores + mask_ref[lvl] * lax.dot_general(qt, kt, _NT, preferred_element_type=F32)
        scores = jnp.where(eye, jnp.sum(q * k, axis=-1, keepdims=True), scores)
        o = o + jnp.dot(scores.astype(BF16), v, preferred_element_type=F32)

        b_last = b[chunk - 1:chunk]
        k_dec = (k * jnp.exp(b_last - b)).astype(BF16)
        e1, e2, e3 = _split3(jnp.broadcast_to(jnp.exp(b_last), (chunk, dk)))
        e_rows = jnp.where(row_dk == 0, e1, jnp.where(row_dk == 1, e2, jnp.where(row_dk == 2, e3, 0.0)))
        dec_col = lax.dot_general(e_rows.astype(BF16), ones_cols, _TN, preferred_element_type=F32)
        decay = jnp.concatenate([dec_col] * (dv // V7X_LANES), axis=1)
        state_ref[h] = state * decay + lax.dot_general(k_dec, v, _TN, preferred_element_type=F32)

        o = o * lax.rsqrt(jnp.mean(o * o, axis=-1, keepdims=True) + EPS) * gn_ref[...]
        gate = _silu(gg_ref[0, pl.ds(r0, chunk), vs].astype(F32))
        o_ref[0, pl.ds(r0, chunk), vs] = (o * gate).astype(o_ref.dtype)

    assert n_levels == chunk.bit_length() - 1

    def chunk_body(c, carry):
        r0 = pl.multiple_of(c * chunk, chunk)
        for h in range(GLA_HEADS):
            head_chunk(r0, h)
        return carry

    lax.fori_loop(0, rows // chunk, chunk_body, 0)


def _gla(proj, col0, gates, w_up_pad, b_gk, norm_g, mixed, *, rows=GLA_ROWS, chunk=GLA_CHUNK):
    bsz, seq, _ = proj.shape
    kw, vw = GLA_KEY_WIDTH, GLA_WIDTH
    assert seq % rows == 0 and rows % chunk == 0 and FOX_WIDTH == vw
    assert col0 % kw == 0 and (col0 + 2 * kw) % vw == 0
    q_blk = col0 // kw
    v_blk = (col0 + 2 * kw) // vw
    tri, masks = _gla_tables(chunk)
    blk = (2 * _nbytes((rows, kw), BF16) + 3 * _nbytes((rows, vw), BF16) + _nbytes((rows, V7X_LANES), F32)
           + _nbytes((V7X_LANES, kw), F32) + _nbytes(masks.shape, F32))
    scratch = _nbytes((GLA_HEADS, GLA_KEY_DIM, GLA_VAL_DIM), F32) + _nbytes((rows, kw), F32)
    return pl.pallas_call(
        functools.partial(_gla_kernel, rows=rows, chunk=chunk, scale=GLA_KEY_DIM ** -0.5),
        grid=(bsz, seq // rows),
        in_specs=[pl.BlockSpec((1, rows, kw), lambda b, t: (b, t, q_blk)),
                  pl.BlockSpec((1, rows, kw), lambda b, t: (b, t, q_blk + 1)),
                  pl.BlockSpec((1, rows, vw), lambda b, t: (b, t, v_blk)),
                  pl.BlockSpec((1, rows, vw), lambda b, t: (b, t, v_blk + 1)),
                  pl.BlockSpec((1, rows, V7X_LANES), lambda b, t: (b, t, 0)),
                  pl.BlockSpec((V7X_LANES, kw), lambda b, t: (0, 0)),
                  pl.BlockSpec((1, kw), lambda b, t: (0, 0)),
                  pl.BlockSpec((1, GLA_VAL_DIM), lambda b, t: (0, 0)),
                  pl.BlockSpec(tri.shape, lambda b, t: (0, 0)),
                  pl.BlockSpec(masks.shape, lambda b, t: (0, 0, 0)),
                  pl.BlockSpec(memory_space=pl.ANY)],
        out_specs=pl.BlockSpec((1, rows, vw), lambda b, t: (b, t, 1)),
        out_shape=jax.ShapeDtypeStruct(mixed.shape, mixed.dtype),
        scratch_shapes=[pltpu.VMEM((GLA_HEADS, GLA_KEY_DIM, GLA_VAL_DIM), F32), pltpu.VMEM((rows, kw), F32)],
        input_output_aliases={10: 0},
        compiler_params=pltpu.CompilerParams(
            dimension_semantics=("arbitrary", "arbitrary"),
            vmem_limit_bytes=_vmem_limit(blk, scratch + (8 << 20))),
        name="gla",
    )(proj, proj, proj, proj, gates, w_up_pad, b_gk.reshape(1, -1), norm_g.reshape(1, -1),
      jnp.asarray(tri, BF16), jnp.asarray(masks), mixed)


def _prep_layer(b_f, w_gk_up):
    gate_bias = jnp.concatenate([b_f, jnp.zeros((V7X_LANES - FOX_HEADS,), F32)]).reshape(1, V7X_LANES)
    w_up_pad = jnp.zeros((V7X_LANES, GLA_KEY_WIDTH), F32).at[FOX_HEADS:FOX_HEADS + GLA_GATE_RANK].set(w_gk_up)
    return gate_bias, w_up_pad


def kernel(x, ln_in_g, ln_in_b, w_in, b_f, w_gk_up, b_gk, fox_norm_g, gla_norm_g, w_out, ln1_g, ln1_b,
           w_gate, w_up, w_down, ln2_g, ln2_b):
    bsz, seq, d_model = x.shape
    depth = w_in.shape[0]
    alpha = (2 * depth) ** 0.25
    rows = bsz * seq

    w_in_t = jnp.swapaxes(w_in, 1, 2)
    n_fox_tiles = IN_F_LO // MM_TN
    n_proj_tiles = n_fox_tiles + (IN_LOW_LO - IN_G_LO) // MM_TN

    def proj_row_start(j):
        return pl.multiple_of(j * MM_TN + jnp.where(j >= n_fox_tiles, IN_G_LO - IN_F_LO, 0), F32_SUBLANES)

    x32, x16 = _layer_norm(x.reshape(rows, d_model), None, ln_in_g, ln_in_b)
    for l in range(depth):
        gate_bias, w_up_pad = _prep_layer(b_f[l], w_gk_up[l])

        proj = _matmul_wt(x16, w_in_t, l, n_proj_tiles, proj_row_start, BF16, scaled_cols=FOX_WIDTH,
                          scale=FOX_HEAD_DIM ** -0.5 * LOG2_E).reshape(bsz, seq, -1)
        gates, aug_q, aug_k = _gates(x16, w_in_t, l, gate_bias, bsz, seq)
        gates = gates.reshape(bsz, seq, V7X_LANES)

        mixed = _fox(proj, aug_q, aug_k, fox_norm_g[l].reshape(FOX_HEADS, 1, FOX_HEAD_DIM))
        mixed = _gla(proj, IN_F_LO, gates, w_up_pad, b_gk[l], gla_norm_g[l], mixed)

        mix = _matmul(mixed.reshape(rows, MIX_WIDTH), w_out, l, d_model, F32)
        x32, x16 = _layer_norm(x32, mix, ln1_g[l], ln1_b[l], alpha=alpha)

        hidden = _gate_up(x16, w_gate, w_up, l)
        ffn = _down(hidden, _cast_rows(w_down, l, hidden.shape[1]))
        last = l == depth - 1
        outs = _layer_norm(x32, ffn, ln2_g[l], ln2_b[l], alpha=alpha, want_bf16=not last)
        if last:
            x32, = outs
        else:
            x32, x16 = outs
    return x32.reshape(bsz, seq, d_model)
```

```python
import functools

import numpy as np
import jax
import jax.numpy as jnp
from jax import lax
from jax.experimental import pallas as pl
from jax.experimental.pallas import tpu as pltpu

F32 = jnp.float32
BF16 = jnp.bfloat16

V7X_VMEM_BYTES = 64 * 2**20
V7X_LANES = 128
V7X_MXU_DIM = 256
F32_SUBLANES = 8

FOX_HEADS = 16
FOX_HEAD_DIM = 128
FOX_WIDTH = FOX_HEADS * FOX_HEAD_DIM
GLA_HEADS = 4
GLA_KEY_DIM = 256
GLA_VAL_DIM = 512
GLA_KEY_WIDTH = GLA_HEADS * GLA_KEY_DIM
GLA_WIDTH = GLA_HEADS * GLA_VAL_DIM
GLA_GATE_RANK = 16
GLA_GATE_NORMALIZER = 16.0
MIX_WIDTH = FOX_WIDTH + GLA_WIDTH
EPS = 1e-5
IN_F_LO = 3 * FOX_WIDTH
IN_G_LO = IN_F_LO + FOX_HEADS
IN_LOW_LO = IN_G_LO + 2 * GLA_KEY_WIDTH + 2 * GLA_WIDTH

LN_ROWS = 256
MM_TM = 1024
MM_TN = 1024
FFN_TN = 512
PREP_ROW_BLOCK = 256
DOWN_TM = 1024
DOWN_TN = 1024
DOWN_TK = 2816
GATE_ROWS = 512
FOX_TQ = 512
FOX_TK = 512
FOX_GROUP = 4
GLA_ROWS = 512
GLA_CHUNK = 64
NEG_BIG = -1e30
LOG2_E = 1.4426950408889634

_NT = (((1,), (1,)), ((), ()))
_TN = (((0,), (0,)), ((), ()))


def _vmem_limit(block_bytes, extra_bytes):
    need = 2 * block_bytes + extra_bytes + (4 << 20)
    return int(min(need, V7X_VMEM_BYTES - (6 << 20)))


def _nbytes(shape, dtype):
    return int(np.prod(shape)) * jnp.dtype(dtype).itemsize


def _log_sigmoid(x):
    return jnp.minimum(x, 0.0) - jnp.log1p(jnp.exp(-jnp.abs(x)))


def _silu(x):
    return x * (1.0 / (1.0 + jnp.exp(-x)))


def _ln_math(t, g, b):
    mu = jnp.mean(t, axis=-1, keepdims=True)
    tc = t - mu
    var = jnp.mean(tc * tc, axis=-1, keepdims=True)
    return tc * lax.rsqrt(var + EPS) * g + b


def _ln_kernel(*refs, alpha, has_res, want_f32, want_bf16):
    if has_res:
        x_ref, y_ref, g_ref, b_ref = refs[:4]
        outs = refs[4:]
        t = alpha * x_ref[...] + y_ref[...]
    else:
        x_ref, g_ref, b_ref = refs[:3]
        outs = refs[3:]
        t = x_ref[...]
    y = _ln_math(t, g_ref[...], b_ref[...])
    i = 0
    if want_f32:
        outs[i][...] = y
        i += 1
    if want_bf16:
        outs[i][...] = y.astype(BF16)


def _layer_norm(x, res, g, b, *, alpha=1.0, want_f32=True, want_bf16=True):
    rows, d = x.shape
    assert rows % LN_ROWS == 0
    row_spec = pl.BlockSpec((LN_ROWS, d), lambda i: (i, 0))
    vec_spec = pl.BlockSpec((1, d), lambda i: (0, 0))
    ins = [x] + ([res] if res is not None else []) + [g.reshape(1, d), b.reshape(1, d)]
    in_specs = [row_spec] * (len(ins) - 2) + [vec_spec, vec_spec]
    out_shape, out_specs = [], []
    if want_f32:
        out_shape.append(jax.ShapeDtypeStruct((rows, d), F32))
        out_specs.append(row_spec)
    if want_bf16:
        out_shape.append(jax.ShapeDtypeStruct((rows, d), BF16))
        out_specs.append(row_spec)
    blk = _nbytes((LN_ROWS, d), F32) * (len(ins) - 2 + len(out_shape))
    return pl.pallas_call(
        functools.partial(_ln_kernel, alpha=alpha, has_res=res is not None,
                          want_f32=want_f32, want_bf16=want_bf16),
        grid=(rows // LN_ROWS,),
        in_specs=in_specs, out_specs=out_specs, out_shape=out_shape,
        compiler_params=pltpu.CompilerParams(
            dimension_semantics=("arbitrary",),
            vmem_limit_bytes=_vmem_limit(blk, 4 * _nbytes((LN_ROWS, d), F32))),
        name="layer_norm",
    )(*ins)


def _ws_kernel(x_ref, *rest, n_w, transposed, epilogue):
    w_refs, o_ref, wb_ref = rest[:n_w], rest[n_w], rest[n_w + 1]
    j, i = pl.program_id(0), pl.program_id(1)
    tn = wb_ref.shape[2] // n_w
    ck = w_refs[0].shape[-1] if transposed else w_refs[0].shape[0]

    def convert_chunk(slot):
        rows = pl.ds(pl.multiple_of(i * ck, ck), ck)
        for t, w_ref in enumerate(w_refs):
            chunk = w_ref[0].T if transposed else w_ref[...]
            wb_ref[slot, rows, t * tn:(t + 1) * tn] = chunk.astype(BF16)

    @pl.when(j == 0)
    def _():
        convert_chunk(0)

    for parity in (0, 1):
        @pl.when(jnp.logical_and(j > 0, (j & 1) == parity))
        def _():
            convert_chunk(parity)
            acc = jnp.dot(x_ref[...], wb_ref[1 - parity], preferred_element_type=F32)
            o_ref[...] = epilogue(acc, j - 1).astype(o_ref.dtype)


def _ws_matmul(x, weights, w_specs, n_tiles, tn_w, tn_out, out_dtype, epilogue, *, transposed, name, tm=MM_TM):
    m, kdim = x.shape
    n_m = m // tm
    ck = kdim // n_m
    assert m % tm == 0 and kdim % n_m == 0 and ck % V7X_LANES == 0
    n_w = len(weights)
    blk = (_nbytes((tm, kdim), x.dtype) + n_w * _nbytes((ck, tn_w), F32) + _nbytes((tm, tn_out), out_dtype))
    scratch = _nbytes((2, kdim, n_w * tn_w), BF16)
    extra = scratch + 2 * _nbytes((tm, n_w * tn_w), F32) + 3 * n_w * _nbytes((ck, tn_w), F32)
    return pl.pallas_call(
        functools.partial(_ws_kernel, n_w=n_w, transposed=transposed, epilogue=epilogue),
        grid=(n_tiles + 1, n_m),
        in_specs=[pl.BlockSpec((tm, kdim), lambda j, i: (jnp.where(j == 0, 0, i), 0))] + w_specs(ck),
        out_specs=pl.BlockSpec((tm, tn_out), lambda j, i: (jnp.where(j == 0, 0, i), jnp.maximum(j - 1, 0))),
        out_shape=jax.ShapeDtypeStruct((m, n_tiles * tn_out), out_dtype),
        scratch_shapes=[pltpu.VMEM((2, kdim, n_w * tn_w), BF16)],
        compiler_params=pltpu.CompilerParams(
            dimension_semantics=("arbitrary", "arbitrary"),
            vmem_limit_bytes=_vmem_limit(blk, extra)),
        name=name,
    )(x, *weights)


def _matmul(x, w, layer, n_cols, out_dtype, *, tn=MM_TN):
    assert n_cols % tn == 0 and w.shape[1] == x.shape[1]
    n_tiles = n_cols // tn

    def w_specs(ck):
        return [pl.BlockSpec((None, ck, tn), lambda j, i: (layer, i, jnp.minimum(j, n_tiles - 1)))]

    return _ws_matmul(x, [w], w_specs, n_tiles, tn, tn, out_dtype, lambda acc, tile: acc,
                      transposed=False, name="matmul")


def _matmul_wt(x, w_t, layer, n_tiles, row_start, out_dtype, *, tn=MM_TN, scaled_cols=0, scale=1.0):
    assert w_t.shape[2] == x.shape[1] and scaled_cols % tn == 0
    scaled_tiles = scaled_cols // tn

    def w_specs(ck):
        return [pl.BlockSpec((pl.Element(1), pl.Element(tn), pl.Element(ck)),
                             lambda j, i: (layer, row_start(jnp.minimum(j, n_tiles - 1)), i * ck))]

    def epilogue(acc, tile):
        return acc * jnp.where(tile < scaled_tiles, scale, 1.0) if scaled_tiles else acc

    return _ws_matmul(x, [w_t], w_specs, n_tiles, tn, tn, out_dtype, epilogue, transposed=True, name="matmul_wt")


def _cast_rows_kernel(w_ref, o_ref, *, valid_blocks):
    w = w_ref[...].astype(o_ref.dtype)
    o_ref[...] = jnp.where(pl.program_id(0) < valid_blocks, w, jnp.zeros_like(w))


def _cast_rows(w, layer, rows_out, *, tr=PREP_ROW_BLOCK):
    rows, n = w.shape[1:]
    assert rows % tr == 0 and rows_out % tr == 0
    valid_blocks = rows // tr
    blk = _nbytes((tr, n), w.dtype) + _nbytes((tr, n), BF16)
    return pl.pallas_call(
        functools.partial(_cast_rows_kernel, valid_blocks=valid_blocks),
        grid=(rows_out // tr,),
        in_specs=[pl.BlockSpec((None, tr, n), lambda r: (layer, jnp.minimum(r, valid_blocks - 1), 0))],
        out_specs=pl.BlockSpec((tr, n), lambda r: (r, 0)),
        out_shape=jax.ShapeDtypeStruct((rows_out, n), BF16),
        compiler_params=pltpu.CompilerParams(
            dimension_semantics=("arbitrary",),
            vmem_limit_bytes=_vmem_limit(blk, 2 * _nbytes((tr, n), F32))),
        name="cast_rows",
    )(w)


def _gate_up(x, w_gate, w_up, layer, *, tn=FFN_TN):
    d_ff = w_gate.shape[2]
    n_out = pl.cdiv(d_ff, DOWN_TK) * DOWN_TK
    assert n_out % tn == 0 and w_gate.shape[1] == x.shape[1]
    last_w_blk = pl.cdiv(d_ff, tn) - 1

    def w_specs(ck):
        return [pl.BlockSpec((None, ck, tn), lambda j, i: (layer, i, jnp.minimum(j, last_w_blk)))] * 2

    def epilogue(z, tile):
        h = _silu(z[:, :tn]) * z[:, tn:]
        col = tile * tn + lax.broadcasted_iota(jnp.int32, h.shape, 1)
        return jnp.where(col < d_ff, h, 0.0)

    return _ws_matmul(x, [w_gate, w_up], w_specs, n_out // tn, tn, tn, BF16, epilogue,
                      transposed=False, name="gate_up")


def _down_kernel(h_ref, w_ref, o_ref, acc_ref, *, nk):
    k = pl.program_id(2)
    part = jnp.dot(h_ref[...], w_ref[...], preferred_element_type=F32)
    if nk == 1:
        o_ref[...] = part
        return

    @pl.when(k == 0)
    def _():
        acc_ref[...] = part

    @pl.when(jnp.logical_and(k > 0, k < nk - 1))
    def _():
        acc_ref[...] += part

    @pl.when(k == nk - 1)
    def _():
        o_ref[...] = acc_ref[...] + part


def _down(h, w_down, *, tm=DOWN_TM, tn=DOWN_TN, tk=DOWN_TK):
    m, kp = h.shape
    n = w_down.shape[1]
    assert m % tm == 0 and n % tn == 0 and kp % tk == 0 and w_down.shape[0] == kp
    nk = kp // tk
    blk = _nbytes((tm, tk), h.dtype) + _nbytes((tk, tn), w_down.dtype) + _nbytes((tm, tn), F32)
    extra = 3 * _nbytes((tm, tn), F32)
    return pl.pallas_call(
        functools.partial(_down_kernel, nk=nk),
        grid=(m // tm, n // tn, nk),
        in_specs=[pl.BlockSpec((tm, tk), lambda i, j, k: (i, k)),
                  pl.BlockSpec((tk, tn), lambda i, j, k: (k, j))],
        out_specs=pl.BlockSpec((tm, tn), lambda i, j, k: (i, j)),
        out_shape=jax.ShapeDtypeStruct((m, n), F32),
        scratch_shapes=[pltpu.VMEM((tm, tn), F32)],
        compiler_params=pltpu.CompilerParams(
            dimension_semantics=("arbitrary", "arbitrary", "arbitrary"),
            vmem_limit_bytes=_vmem_limit(blk, extra)),
        name="down_proj",
    )(h, w_down)


def _split3(x):
    a = x.astype(BF16).astype(F32)
    r = x - a
    b = r.astype(BF16).astype(F32)
    return a, b, (r - b).astype(BF16).astype(F32)


def _gates_kernel(x_ref, wf_ref, wl_ref, bias_ref, tri_ref, o_ref, aq_ref, ak_ref, carry_ref):
    @pl.when(pl.program_id(1) == 0)
    def _():
        carry_ref[...] = jnp.zeros_like(carry_ref)

    wf, wl = wf_ref[0], wl_ref[0]
    pad = jnp.zeros((V7X_LANES - wf.shape[0] - wl.shape[0], wf.shape[1]), F32)
    w = jnp.concatenate([wf, wl, pad], axis=0).astype(BF16)
    z = lax.dot_general(x_ref[...], w, _NT, preferred_element_type=F32)
    log_f = _log_sigmoid(z + bias_ref[...])
    c = jnp.dot(tri_ref[...], log_f, preferred_element_type=F32,
                precision=lax.Precision.HIGHEST) + carry_ref[0:1, :]
    rows = c.shape[0]
    carry_ref[...] = jnp.broadcast_to(c[rows - 1:rows, :], carry_ref.shape)
    lane = lax.broadcasted_iota(jnp.int32, c.shape, 1)
    o_ref[...] = jnp.where(lane < FOX_HEADS, c, z)

    c2 = c * LOG2_E
    for h in range(FOX_HEADS):
        c1, c2_, c3 = _split3(jnp.broadcast_to(c2[:, h:h + 1], c.shape))
        aq = jnp.where(lane == 0, c1, jnp.where(lane == 1, c2_, jnp.where(lane == 2, c3,
                       jnp.where(lane < 6, 1.0, 0.0))))
        ak = jnp.where(lane < 3, 1.0, jnp.where(lane == 3, -c1, jnp.where(lane == 4, -c2_,
                       jnp.where(lane == 5, -c3, 0.0))))
        aq_ref[0, h] = aq.astype(BF16)
        ak_ref[0, h] = ak.astype(BF16)


def _gates(x, w_in_t, layer, bias, bsz, seq):
    m, kdim = x.shape
    steps = seq // GATE_ROWS
    tri = jnp.asarray(np.tril(np.ones((GATE_ROWS, GATE_ROWS), np.float32)))
    aug_shape = jax.ShapeDtypeStruct((bsz, FOX_HEADS, seq, V7X_LANES), BF16)
    aug_spec = pl.BlockSpec((1, FOX_HEADS, GATE_ROWS, V7X_LANES), lambda b, i: (b, 0, i, 0))
    assert IN_F_LO % F32_SUBLANES == 0 and IN_LOW_LO % F32_SUBLANES == 0

    def rows_spec(n_rows, row0):
        return pl.BlockSpec((pl.Element(1), pl.Element(n_rows), pl.Element(kdim)), lambda b, i: (layer, row0, 0))

    blk = (_nbytes((GATE_ROWS, kdim), x.dtype) + _nbytes((V7X_LANES, kdim), F32)
           + _nbytes((GATE_ROWS, GATE_ROWS), F32) + _nbytes((GATE_ROWS, V7X_LANES), F32)
           + 2 * _nbytes((FOX_HEADS, GATE_ROWS, V7X_LANES), BF16))
    return pl.pallas_call(
        _gates_kernel,
        grid=(bsz, steps),
        in_specs=[pl.BlockSpec((GATE_ROWS, kdim), lambda b, i: (b * steps + i, 0)),
                  rows_spec(FOX_HEADS, IN_F_LO), rows_spec(GLA_GATE_RANK, IN_LOW_LO),
                  pl.BlockSpec((1, V7X_LANES), lambda b, i: (0, 0)),
                  pl.BlockSpec((GATE_ROWS, GATE_ROWS), lambda b, i: (0, 0))],
        out_specs=[pl.BlockSpec((GATE_ROWS, V7X_LANES), lambda b, i: (b * steps + i, 0)), aug_spec, aug_spec],
        out_shape=[jax.ShapeDtypeStruct((m, V7X_LANES), F32), aug_shape, aug_shape],
        scratch_shapes=[pltpu.VMEM((8, V7X_LANES), F32)],
        compiler_params=pltpu.CompilerParams(
            dimension_semantics=("arbitrary", "arbitrary"),
            vmem_limit_bytes=_vmem_limit(blk, 16 * _nbytes((GATE_ROWS, V7X_LANES), F32))),
        name="gates",
    )(x, w_in_t, w_in_t, bias, tri)


def _fox_kernel(q_ref, k_ref, v_ref, aq_ref, ak_ref, g_ref, o_ref, kp_ref, vp_ref, m_ref, acc_ref, *, tq, tk):
    qi = pl.program_id(2)
    n_heads = aq_ref.shape[1]
    d = q_ref.shape[2] // n_heads
    dp = acc_ref.shape[2]

    @pl.when(qi == 0)
    def _():
        lane = lax.broadcasted_iota(jnp.int32, (vp_ref.shape[1], V7X_LANES), 1)
        for g in range(n_heads):
            kp_ref[g, :, :d] = k_ref[0, :, g * d:(g + 1) * d]
            kp_ref[g, :, d:] = ak_ref[0, g]
            vp_ref[g, :, :d] = v_ref[0, :, g * d:(g + 1) * d]
            vp_ref[g, :, d:] = jnp.where(lane == 0, 1.0, 0.0).astype(BF16)

    qp = [jnp.concatenate([q_ref[0, :, g * d:(g + 1) * d], aq_ref[0, g]], axis=1) for g in range(n_heads)]
    m_ref[...] = jnp.full_like(m_ref, NEG_BIG)
    acc_ref[...] = jnp.zeros_like(acc_ref)

    def step(j, masked):
        start = pl.multiple_of(j * tk, tk)
        for g in range(n_heads):
            s = lax.dot_general(qp[g], kp_ref[g, pl.ds(start, tk), :], _NT, preferred_element_type=F32)
            if masked:
                row = lax.broadcasted_iota(jnp.int32, (tq, tk), 0)
                col = lax.broadcasted_iota(jnp.int32, (tq, tk), 1)
                s = jnp.where(row >= col, s, NEG_BIG)
            m_prev = m_ref[g]
            m_new = jnp.maximum(m_prev, jnp.max(s, axis=-1, keepdims=True))
            p = jnp.exp2(s - jnp.concatenate([m_new] * (tk // V7X_LANES), axis=1)).astype(BF16)
            alpha = jnp.exp2(m_prev - m_new)
            acc_ref[g] = (jnp.concatenate([alpha] * (dp // V7X_LANES), axis=1) * acc_ref[g]
                          + jnp.dot(p, vp_ref[g, pl.ds(start, tk), :], preferred_element_type=F32))
            m_ref[g] = m_new

    def body(i, carry):
        step(2 * i, False)
        step(2 * i + 1, False)
        return carry

    lax.fori_loop(0, lax.shift_right_logical(qi, 1), body, 0)

    @pl.when((qi & 1) == 1)
    def _():
        step(qi - 1, False)
        step(qi, True)

    @pl.when((qi & 1) == 0)
    def _():
        step(qi, True)

    for g in range(n_heads):
        acc = acc_ref[g]
        o = acc[:, :d] / acc[:, d:d + 1]
        o = o * lax.rsqrt(jnp.mean(o * o, axis=-1, keepdims=True) + EPS) * g_ref[g]
        o_ref[0, :, g * d:(g + 1) * d] = o.astype(o_ref.dtype)


def _fox(proj, aq, ak, norm_g, *, tq=FOX_TQ, tk=FOX_TK, group=FOX_GROUP):
    bsz, seq, _ = proj.shape
    d = FOX_HEAD_DIM
    dp = d + V7X_LANES
    gd = group * d
    n_groups = FOX_HEADS // group
    assert tq == tk and seq % tq == 0 and FOX_HEADS % group == 0
    blk = (2 * _nbytes((tq, gd), BF16) + 2 * _nbytes((seq, gd), BF16) + _nbytes((group, tq, V7X_LANES), BF16)
           + _nbytes((group, seq, V7X_LANES), BF16) + _nbytes((group, 8, d), F32))
    scratch = (2 * _nbytes((group, seq, dp), BF16) + _nbytes((group, tq, V7X_LANES), F32)
               + _nbytes((group, tq, dp), F32))
    return pl.pallas_call(
        functools.partial(_fox_kernel, tq=tq, tk=tk),
        grid=(bsz, n_groups, seq // tq),
        in_specs=[pl.BlockSpec((1, tq, gd), lambda b, h, i: (b, i, h)),
                  pl.BlockSpec((1, seq, gd), lambda b, h, i: (b, 0, n_groups + h)),
                  pl.BlockSpec((1, seq, gd), lambda b, h, i: (b, 0, 2 * n_groups + h)),
                  pl.BlockSpec((1, group, tq, V7X_LANES), lambda b, h, i: (b, h, i, 0)),
                  pl.BlockSpec((1, group, seq, V7X_LANES), lambda b, h, i: (b, h, 0, 0)),
                  pl.BlockSpec((group, 1, d), lambda b, h, i: (h, 0, 0))],
        out_specs=pl.BlockSpec((1, tq, gd), lambda b, h, i: (b, i, h)),
        out_shape=jax.ShapeDtypeStruct((bsz, seq, MIX_WIDTH), BF16),
        scratch_shapes=[pltpu.VMEM((group, seq, dp), BF16), pltpu.VMEM((group, seq, dp), BF16),
                        pltpu.VMEM((group, tq, V7X_LANES), F32), pltpu.VMEM((group, tq, dp), F32)],
        compiler_params=pltpu.CompilerParams(
            dimension_semantics=("arbitrary", "arbitrary", "arbitrary"),
            vmem_limit_bytes=_vmem_limit(blk, scratch + 8 * group * _nbytes((tq, tk), F32))),
        name="fox_attention",
    )(proj, proj, proj, aq, ak, norm_g)


def _gla_tables(chunk):
    idx = np.arange(chunk)
    masks = []
    half = chunk // 2
    while half >= 1:
        blk = idx // (2 * half)
        second = (idx // half) % 2 == 1
        masks.append(((blk[:, None] == blk[None, :]) & second[:, None] & ~second[None, :]).astype(np.float32))
        half //= 2
    return np.tril(np.ones((chunk, chunk), np.float32)), np.stack(masks, axis=0)


def _level_refs(b):
    chunk, width = b.shape
    refs = []
    half = chunk // 2
    while half >= 4:
        pieces = [jnp.broadcast_to(b[s + half - 1:s + half], (2 * half, width)) for s in range(0, chunk, 2 * half)]
        refs.append(pieces[0] if len(pieces) == 1 else jnp.concatenate(pieces, axis=0))
        half //= 2
    prev1 = pltpu.roll(b, 1, 0)
    prev2 = pltpu.roll(b, 2, 0)
    next1 = pltpu.roll(b, chunk - 1, 0)
    pos = lax.broadcasted_iota(jnp.int32, b.shape, 0) & 3
    refs.append(jnp.where(pos == 0, next1, jnp.where(pos == 1, b, jnp.where(pos == 2, prev1, prev2))))
    refs.append(jnp.where((pos & 1) == 0, b, prev1))
    return refs


def _gla_kernel(q_ref, k_ref, v_ref, gg_ref, gates_ref, wup_ref, bgk_ref, gn_ref, tri_ref, mask_ref,
                mixed_hbm_ref, o_ref, state_ref, la_ref, *, rows, chunk, scale):
    del mixed_hbm_ref
    dk, dv = GLA_KEY_DIM, GLA_VAL_DIM

    @pl.when(pl.program_id(1) == 0)
    def _():
        state_ref[...] = jnp.zeros_like(state_ref)

    pre = jnp.dot(gates_ref[0], wup_ref[...], preferred_element_type=F32,
                  precision=lax.Precision.HIGHEST) + bgk_ref[...]
    la_ref[...] = _log_sigmoid(pre) * (1.0 / GLA_GATE_NORMALIZER)

    n_levels = mask_ref.shape[0]
    eye = (lax.broadcasted_iota(jnp.int32, (chunk, chunk), 0)
           == lax.broadcasted_iota(jnp.int32, (chunk, chunk), 1))
    row_dk = lax.broadcasted_iota(jnp.int32, (chunk, dk), 0)
    ones_cols = jnp.ones((chunk, V7X_LANES), BF16)
    tri = tri_ref[...]

    def head_chunk(r0, h):
        ks = slice(h * dk, (h + 1) * dk)
        vs = slice(h * dv, (h + 1) * dv)
        g = la_ref[pl.ds(r0, chunk), ks]
        g_parts = jnp.concatenate([p.astype(BF16) for p in _split3(g)], axis=1)
        b3 = jnp.dot(tri, g_parts, preferred_element_type=F32)
        b = b3[:, :dk] + b3[:, dk:2 * dk] + b3[:, 2 * dk:]
        q = q_ref[0, pl.ds(r0, chunk), ks].astype(F32) * scale
        k = k_ref[0, pl.ds(r0, chunk), ks].astype(F32)
        v = v_ref[0, pl.ds(r0, chunk), vs]

        state = state_ref[h]
        o = jnp.dot((q * jnp.exp(b)).astype(BF16), state.astype(BF16), preferred_element_type=F32)

        scores = jnp.zeros((chunk, chunk), F32)
        for lvl, ref in enumerate(_level_refs(b)):
            qt = (q * jnp.exp(jnp.minimum(b - ref, 0.0))).astype(BF16)
            kt = (k * jnp.exp(jnp.minimum(ref - b, 0.0))).astype(BF16)
            scores = scores + mask_ref[lvl] * lax.dot_general(qt, kt, _NT, preferred_element_type=F32)
        scores = jnp.where(eye, jnp.sum(q * k, axis=-1, keepdims=True), scores)
        o = o + jnp.dot(scores.astype(BF16), v, preferred_element_type=F32)

        b_last = b[chunk - 1:chunk]
        k_dec = (k * jnp.exp(b_last - b)).astype(BF16)
        e1, e2, e3 = _split3(jnp.broadcast_to(jnp.exp(b_last), (chunk, dk)))
        e_rows = jnp.where(row_dk == 0, e1, jnp.where(row_dk == 1, e2, jnp.where(row_dk == 2, e3, 0.0)))
        dec_col = lax.dot_general(e_rows.astype(BF16), ones_cols, _TN, preferred_element_type=F32)
        decay = jnp.concatenate([dec_col] * (dv // V7X_LANES), axis=1)
        state_ref[h] = state * decay + lax.dot_general(k_dec, v, _TN, preferred_element_type=F32)

        o = o * lax.rsqrt(jnp.mean(o * o, axis=-1, keepdims=True) + EPS) * gn_ref[...]
        gate = _silu(gg_ref[0, pl.ds(r0, chunk), vs].astype(F32))
        o_ref[0, pl.ds(r0, chunk), vs] = (o * gate).astype(o_ref.dtype)

    assert n_levels == chunk.bit_length() - 1

    def chunk_body(c, carry):
        r0 = pl.multiple_of(c * chunk, chunk)
        for h in range(GLA_HEADS):
            head_chunk(r0, h)
        return carry

    lax.fori_loop(0, rows // chunk, chunk_body, 0)


def _gla(proj, col0, gates, w_up_pad, b_gk, norm_g, mixed, *, rows=GLA_ROWS, chunk=GLA_CHUNK):
    bsz, seq, _ = proj.shape
    kw, vw = GLA_KEY_WIDTH, GLA_WIDTH
    assert seq % rows == 0 and rows % chunk == 0 and FOX_WIDTH == vw
    assert col0 % kw == 0 and (col0 + 2 * kw) % vw == 0
    q_blk = col0 // kw
    v_blk = (col0 + 2 * kw) // vw
    tri, masks = _gla_tables(chunk)
    blk = (2 * _nbytes((rows, kw), BF16) + 3 * _nbytes((rows, vw), BF16) + _nbytes((rows, V7X_LANES), F32)
           + _nbytes((V7X_LANES, kw), F32) + _nbytes(masks.shape, F32))
    scratch = _nbytes((GLA_HEADS, GLA_KEY_DIM, GLA_VAL_DIM), F32) + _nbytes((rows, kw), F32)
    return pl.pallas_call(
        functools.partial(_gla_kernel, rows=rows, chunk=chunk, scale=GLA_KEY_DIM ** -0.5),
        grid=(bsz, seq // rows),
        in_specs=[pl.BlockSpec((1, rows, kw), lambda b, t: (b, t, q_blk)),
                  pl.BlockSpec((1, rows, kw), lambda b, t: (b, t, q_blk + 1)),
                  pl.BlockSpec((1, rows, vw), lambda b, t: (b, t, v_blk)),
                  pl.BlockSpec((1, rows, vw), lambda b, t: (b, t, v_blk + 1)),
                  pl.BlockSpec((1, rows, V7X_LANES), lambda b, t: (b, t, 0)),
                  pl.BlockSpec((V7X_LANES, kw), lambda b, t: (0, 0)),
                  pl.BlockSpec((1, kw), lambda b, t: (0, 0)),
                  pl.BlockSpec((1, GLA_VAL_DIM), lambda b, t: (0, 0)),
                  pl.BlockSpec(tri.shape, lambda b, t: (0, 0)),
                  pl.BlockSpec(masks.shape, lambda b, t: (0, 0, 0)),
                  pl.BlockSpec(memory_space=pl.ANY)],
        out_specs=pl.BlockSpec((1, rows, vw), lambda b, t: (b, t, 1)),
        out_shape=jax.ShapeDtypeStruct(mixed.shape, mixed.dtype),
        scratch_shapes=[pltpu.VMEM((GLA_HEADS, GLA_KEY_DIM, GLA_VAL_DIM), F32), pltpu.VMEM((rows, kw), F32)],
        input_output_aliases={10: 0},
        compiler_params=pltpu.CompilerParams(
            dimension_semantics=("arbitrary", "arbitrary"),
            vmem_limit_bytes=_vmem_limit(blk, scratch + (8 << 20))),
        name="gla",
    )(proj, proj, proj, proj, gates, w_up_pad, b_gk.reshape(1, -1), norm_g.reshape(1, -1),
      jnp.asarray(tri, BF16), jnp.asarray(masks), mixed)


def _prep_layer(b_f, w_gk_up):
    gate_bias = jnp.concatenate([b_f, jnp.zeros((V7X_LANES - FOX_HEADS,), F32)]).reshape(1, V7X_LANES)
    w_up_pad = jnp.zeros((V7X_LANES, GLA_KEY_WIDTH), F32).at[FOX_HEADS:FOX_HEADS + GLA_GATE_RANK].set(w_gk_up)
    return gate_bias, w_up_pad


def kernel(x, ln_in_g, ln_in_b, w_in, b_f, w_gk_up, b_gk, fox_norm_g, gla_norm_g, w_out, ln1_g, ln1_b,
           w_gate, w_up, w_down, ln2_g, ln2_b):
    bsz, seq, d_model = x.shape
    depth = w_in.shape[0]
    alpha = (2 * depth) ** 0.25
    rows = bsz * seq

    w_in_t = jnp.swapaxes(w_in, 1, 2)
    n_fox_tiles = IN_F_LO // MM_TN
    n_proj_tiles = n_fox_tiles + (IN_LOW_LO - IN_G_LO) // MM_TN

    def proj_row_start(j):
        return pl.multiple_of(j * MM_TN + jnp.where(j >= n_fox_tiles, IN_G_LO - IN_F_LO, 0), F32_SUBLANES)

    x32, x16 = _layer_norm(x.reshape(rows, d_model), None, ln_in_g, ln_in_b)
    for l in range(depth):
        gate_bias, w_up_pad = _prep_layer(b_f[l], w_gk_up[l])

        proj = _matmul_wt(x16, w_in_t, l, n_proj_tiles, proj_row_start, BF16, scaled_cols=FOX_WIDTH,
                          scale=FOX_HEAD_DIM ** -0.5 * LOG2_E).reshape(bsz, seq, -1)
        gates, aug_q, aug_k = _gates(x16, w_in_t, l, gate_bias, bsz, seq)
        gates = gates.reshape(bsz, seq, V7X_LANES)

        mixed = _fox(proj, aug_q, aug_k, fox_norm_g[l].reshape(FOX_HEADS, 1, FOX_HEAD_DIM))
        mixed = _gla(proj, IN_F_LO, gates, w_up_pad, b_gk[l], gla_norm_g[l], mixed)

        mix = _matmul(mixed.reshape(rows, MIX_WIDTH), w_out, l, d_model, F32)
        x32, x16 = _layer_norm(x32, mix, ln1_g[l], ln1_b[l], alpha=alpha)

        hidden = _gate_up(x16, w_gate, w_up, l)
        ffn = _down(hidden, _cast_rows(w_down, l, hidden.shape[1]))
        last = l == depth - 1
        outs = _layer_norm(x32, ffn, ln2_g[l], ln2_b[l], alpha=alpha, want_bf16=not last)
        if last:
            x32, = outs
        else:
            x32, x16 = outs
    return x32.reshape(bsz, seq, d_model)
```

```python
import functools

import numpy as np
import jax
import jax.numpy as jnp
from jax import lax
from jax.experimental import pallas as pl
from jax.experimental.pallas import tpu as pltpu

F32 = jnp.float32
BF16 = jnp.bfloat16

V7X_VMEM_BYTES = 64 * 2**20
V7X_LANES = 128
V7X_MXU_DIM = 256
F32_SUBLANES = 8
BF16_SUBLANES = 16

FOX_HEADS = 16
FOX_HEAD_DIM = 128
FOX_WIDTH = FOX_HEADS * FOX_HEAD_DIM
GLA_HEADS = 4
GLA_KEY_DIM = 256
GLA_VAL_DIM = 512
GLA_KEY_WIDTH = GLA_HEADS * GLA_KEY_DIM
GLA_WIDTH = GLA_HEADS * GLA_VAL_DIM
GLA_GATE_RANK = 16
GLA_GATE_NORMALIZER = 16.0
MIX_WIDTH = FOX_WIDTH + GLA_WIDTH
EPS = 1e-5
IN_F_LO = 3 * FOX_WIDTH
IN_G_LO = IN_F_LO + FOX_HEADS
IN_LOW_LO = IN_G_LO + 2 * GLA_KEY_WIDTH + 2 * GLA_WIDTH

LN_ROWS = 256
MM_TM = 1024
MM_TN = 1024
FFN_TN = 512
FFN_PAD = 512
DOWN_TM = 512
DOWN_TN = 512
GATE_ROWS = 512
FOX_TQ = 512
FOX_TK = 512
FOX_GROUP = 4
GLA_ROWS = 512
GLA_CHUNK = 64
NEG_BIG = -1e30
LOG2_E = 1.4426950408889634

_NT = (((1,), (1,)), ((), ()))
_TN = (((0,), (0,)), ((), ()))


def _vmem_limit(block_bytes, extra_bytes):
    need = 2 * block_bytes + extra_bytes + (4 << 20)
    return int(min(need, V7X_VMEM_BYTES - (6 << 20)))


def _nbytes(shape, dtype):
    return int(np.prod(shape)) * jnp.dtype(dtype).itemsize


def _log_sigmoid(x):
    return jnp.minimum(x, 0.0) - jnp.log1p(jnp.exp(-jnp.abs(x)))


def _silu(x):
    return x * (1.0 / (1.0 + jnp.exp(-x)))


def _ln_math(t, g, b):
    mu = jnp.mean(t, axis=-1, keepdims=True)
    tc = t - mu
    var = jnp.mean(tc * tc, axis=-1, keepdims=True)
    return tc * lax.rsqrt(var + EPS) * g + b


def _ln_kernel(*refs, alpha, has_res, want_f32, want_bf16):
    if has_res:
        x_ref, y_ref, g_ref, b_ref = refs[:4]
        outs = refs[4:]
        t = alpha * x_ref[...] + y_ref[...]
    else:
        x_ref, g_ref, b_ref = refs[:3]
        outs = refs[3:]
        t = x_ref[...]
    y = _ln_math(t, g_ref[...], b_ref[...])
    i = 0
    if want_f32:
        outs[i][...] = y
        i += 1
    if want_bf16:
        outs[i][...] = y.astype(BF16)


def _layer_norm(x, res, g, b, *, alpha=1.0, want_f32=True, want_bf16=True):
    rows, d = x.shape
    assert rows % LN_ROWS == 0
    row_spec = pl.BlockSpec((LN_ROWS, d), lambda i: (i, 0))
    vec_spec = pl.BlockSpec((1, d), lambda i: (0, 0))
    ins = [x] + ([res] if res is not None else []) + [g.reshape(1, d), b.reshape(1, d)]
    in_specs = [row_spec] * (len(ins) - 2) + [vec_spec, vec_spec]
    out_shape, out_specs = [], []
    if want_f32:
        out_shape.append(jax.ShapeDtypeStruct((rows, d), F32))
        out_specs.append(row_spec)
    if want_bf16:
        out_shape.append(jax.ShapeDtypeStruct((rows, d), BF16))
        out_specs.append(row_spec)
    blk = _nbytes((LN_ROWS, d), F32) * (len(ins) - 2 + len(out_shape))
    return pl.pallas_call(
        functools.partial(_ln_kernel, alpha=alpha, has_res=res is not None,
                          want_f32=want_f32, want_bf16=want_bf16),
        grid=(rows // LN_ROWS,),
        in_specs=in_specs, out_specs=out_specs, out_shape=out_shape,
        compiler_params=pltpu.CompilerParams(
            dimension_semantics=("arbitrary",),
            vmem_limit_bytes=_vmem_limit(blk, 4 * _nbytes((LN_ROWS, d), F32))),
        name="layer_norm",
    )(*ins)


def _ws_kernel(x_ref, *rest, n_w, transposed, epilogue, k_valid):
    w_refs, o_ref, wb_ref = rest[:n_w], rest[n_w], rest[n_w + 1]
    j, i = pl.program_id(0), pl.program_id(1)
    tn = wb_ref.shape[2] // n_w
    ck = w_refs[0].shape[-1] if transposed else w_refs[0].shape[0]

    def convert_chunk(slot):
        rows = pl.ds(pl.multiple_of(i * ck, ck), ck)
        for t, w_ref in enumerate(w_refs):
            chunk = w_ref[0].T if transposed else w_ref[...]
            if k_valid < wb_ref.shape[1]:
                row = i * ck + lax.broadcasted_iota(jnp.int32, chunk.shape, 0)
                chunk = jnp.where(row < k_valid, chunk, 0.0)
            wb_ref[slot, rows, t * tn:(t + 1) * tn] = chunk.astype(BF16)

    @pl.when(j == 0)
    def _():
        convert_chunk(0)

    for parity in (0, 1):
        @pl.when(jnp.logical_and(j > 0, (j & 1) == parity))
        def _():
            convert_chunk(parity)
            acc = jnp.dot(x_ref[...], wb_ref[1 - parity], preferred_element_type=F32)
            o_ref[...] = epilogue(acc, j - 1).astype(o_ref.dtype)


def _ws_matmul(x, weights, w_specs, n_tiles, tn_w, tn_out, out_dtype, epilogue, *, transposed, name, tm=MM_TM,
               k_valid=None):
    m, kdim = x.shape
    n_m = m // tm
    ck = kdim // n_m
    assert m % tm == 0 and kdim % n_m == 0 and ck % (V7X_LANES if transposed else BF16_SUBLANES) == 0
    k_valid = kdim if k_valid is None else k_valid
    n_w = len(weights)
    blk = (_nbytes((tm, kdim), x.dtype) + n_w * _nbytes((ck, tn_w), F32) + _nbytes((tm, tn_out), out_dtype))
    scratch = _nbytes((2, kdim, n_w * tn_w), BF16)
    extra = scratch + 2 * _nbytes((tm, n_w * tn_w), F32) + 3 * n_w * _nbytes((ck, tn_w), F32)
    return pl.pallas_call(
        functools.partial(_ws_kernel, n_w=n_w, transposed=transposed, epilogue=epilogue, k_valid=k_valid),
        grid=(n_tiles + 1, n_m),
        in_specs=[pl.BlockSpec((tm, kdim), lambda j, i: (jnp.where(j == 0, 0, i), 0))] + w_specs(ck),
        out_specs=pl.BlockSpec((tm, tn_out), lambda j, i: (jnp.where(j == 0, 0, i), jnp.maximum(j - 1, 0))),
        out_shape=jax.ShapeDtypeStruct((m, n_tiles * tn_out), out_dtype),
        scratch_shapes=[pltpu.VMEM((2, kdim, n_w * tn_w), BF16)],
        compiler_params=pltpu.CompilerParams(
            dimension_semantics=("arbitrary", "arbitrary"),
            vmem_limit_bytes=_vmem_limit(blk, extra)),
        name=name,
    )(x, *weights)


def _matmul(x, w, layer, n_cols, out_dtype, *, tn=MM_TN):
    assert n_cols % tn == 0 and w.shape[1] == x.shape[1]
    n_tiles = n_cols // tn

    def w_specs(ck):
        return [pl.BlockSpec((None, ck, tn), lambda j, i: (layer, i, jnp.minimum(j, n_tiles - 1)))]

    return _ws_matmul(x, [w], w_specs, n_tiles, tn, tn, out_dtype, lambda acc, tile: acc,
                      transposed=False, name="matmul")


def _matmul_wt(x, w_t, layer, n_tiles, row_start, out_dtype, *, tn=MM_TN, scaled_cols=0, scale=1.0):
    assert w_t.shape[2] == x.shape[1] and scaled_cols % tn == 0
    scaled_tiles = scaled_cols // tn

    def w_specs(ck):
        return [pl.BlockSpec((pl.Element(1), pl.Element(tn), pl.Element(ck)),
                             lambda j, i: (layer, row_start(jnp.minimum(j, n_tiles - 1)), i * ck))]

    def epilogue(acc, tile):
        return acc * jnp.where(tile < scaled_tiles, scale, 1.0) if scaled_tiles else acc

    return _ws_matmul(x, [w_t], w_specs, n_tiles, tn, tn, out_dtype, epilogue, transposed=True, name="matmul_wt")


def _gate_up(x, w_gate, w_up, layer, *, tn=FFN_TN):
    d_ff = w_gate.shape[2]
    n_out = pl.cdiv(d_ff, FFN_PAD) * FFN_PAD
    assert n_out % tn == 0 and w_gate.shape[1] == x.shape[1]
    last_w_blk = pl.cdiv(d_ff, tn) - 1

    def w_specs(ck):
        return [pl.BlockSpec((None, ck, tn), lambda j, i: (layer, i, jnp.minimum(j, last_w_blk)))] * 2

    def epilogue(z, tile):
        h = _silu(z[:, :tn]) * z[:, tn:]
        col = tile * tn + lax.broadcasted_iota(jnp.int32, h.shape, 1)
        return jnp.where(col < d_ff, h, 0.0)

    return _ws_matmul(x, [w_gate, w_up], w_specs, n_out // tn, tn, tn, BF16, epilogue,
                      transposed=False, name="gate_up")


def _down(h, w_down, layer, *, tm=DOWN_TM, tn=DOWN_TN):
    d_ff, n = w_down.shape[1:]
    assert n % tn == 0 and h.shape[1] >= d_ff
    n_tiles = n // tn

    def w_specs(ck):
        assert h.shape[1] - d_ff < ck
        return [pl.BlockSpec((None, ck, tn), lambda j, i: (layer, i, jnp.minimum(j, n_tiles - 1)))]

    return _ws_matmul(h, [w_down], w_specs, n_tiles, tn, tn, F32, lambda acc, tile: acc,
                      transposed=False, name="down_proj", tm=tm, k_valid=d_ff)


def _split3(x):
    a = x.astype(BF16).astype(F32)
    r = x - a
    b = r.astype(BF16).astype(F32)
    return a, b, (r - b).astype(BF16).astype(F32)


def _gates_kernel(x_ref, wf_ref, wl_ref, bias_ref, tri_ref, o_ref, aq_ref, ak_ref, carry_ref):
    @pl.when(pl.program_id(1) == 0)
    def _():
        carry_ref[...] = jnp.zeros_like(carry_ref)

    wf, wl = wf_ref[0], wl_ref[0]
    pad = jnp.zeros((V7X_LANES - wf.shape[0] - wl.shape[0], wf.shape[1]), F32)
    w = jnp.concatenate([wf, wl, pad], axis=0).astype(BF16)
    z = lax.dot_general(x_ref[...], w, _NT, preferred_element_type=F32)
    log_f = _log_sigmoid(z + bias_ref[...])
    c = jnp.dot(tri_ref[...], log_f, preferred_element_type=F32,
                precision=lax.Precision.HIGHEST) + carry_ref[0:1, :]
    rows = c.shape[0]
    carry_ref[...] = jnp.broadcast_to(c[rows - 1:rows, :], carry_ref.shape)
    lane = lax.broadcasted_iota(jnp.int32, c.shape, 1)
    o_ref[...] = jnp.where(lane < FOX_HEADS, c, z)

    c2 = c * LOG2_E
    for h in range(FOX_HEADS):
        c1, c2_, c3 = _split3(jnp.broadcast_to(c2[:, h:h + 1], c.shape))
        aq = jnp.where(lane == 0, c1, jnp.where(lane == 1, c2_, jnp.where(lane == 2, c3,
                       jnp.where(lane < 6, 1.0, 0.0))))
        ak = jnp.where(lane < 3, 1.0, jnp.where(lane == 3, -c1, jnp.where(lane == 4, -c2_,
                       jnp.where(lane == 5, -c3, 0.0))))
        aq_ref[0, h] = aq.astype(BF16)
        ak_ref[0, h] = ak.astype(BF16)


def _gates(x, w_in_t, layer, bias, bsz, seq):
    m, kdim = x.shape
    steps = seq // GATE_ROWS
    tri = jnp.asarray(np.tril(np.ones((GATE_ROWS, GATE_ROWS), np.float32)))
    aug_shape = jax.ShapeDtypeStruct((bsz, FOX_HEADS, seq, V7X_LANES), BF16)
    aug_spec = pl.BlockSpec((1, FOX_HEADS, GATE_ROWS, V7X_LANES), lambda b, i: (b, 0, i, 0))
    assert IN_F_LO % F32_SUBLANES == 0 and IN_LOW_LO % F32_SUBLANES == 0

    def rows_spec(n_rows, row0):
        return pl.BlockSpec((pl.Element(1), pl.Element(n_rows), pl.Element(kdim)), lambda b, i: (layer, row0, 0))

    blk = (_nbytes((GATE_ROWS, kdim), x.dtype) + _nbytes((V7X_LANES, kdim), F32)
           + _nbytes((GATE_ROWS, GATE_ROWS), F32) + _nbytes((GATE_ROWS, V7X_LANES), F32)
           + 2 * _nbytes((FOX_HEADS, GATE_ROWS, V7X_LANES), BF16))
    return pl.pallas_call(
        _gates_kernel,
        grid=(bsz, steps),
        in_specs=[pl.BlockSpec((GATE_ROWS, kdim), lambda b, i: (b * steps + i, 0)),
                  rows_spec(FOX_HEADS, IN_F_LO), rows_spec(GLA_GATE_RANK, IN_LOW_LO),
                  pl.BlockSpec((1, V7X_LANES), lambda b, i: (0, 0)),
                  pl.BlockSpec((GATE_ROWS, GATE_ROWS), lambda b, i: (0, 0))],
        out_specs=[pl.BlockSpec((GATE_ROWS, V7X_LANES), lambda b, i: (b * steps + i, 0)), aug_spec, aug_spec],
        out_shape=[jax.ShapeDtypeStruct((m, V7X_LANES), F32), aug_shape, aug_shape],
        scratch_shapes=[pltpu.VMEM((8, V7X_LANES), F32)],
        compiler_params=pltpu.CompilerParams(
            dimension_semantics=("arbitrary", "arbitrary"),
            vmem_limit_bytes=_vmem_limit(blk, 16 * _nbytes((GATE_ROWS, V7X_LANES), F32))),
        name="gates",
    )(x, w_in_t, w_in_t, bias, tri)


def _fox_kernel(q_ref, k_ref, v_ref, aq_ref, ak_ref, g_ref, o_ref, kp_ref, vp_ref, m_ref, acc_ref, *, tq, tk):
    qi = pl.program_id(2)
    n_heads = aq_ref.shape[1]
    d = q_ref.shape[2] // n_heads
    dp = acc_ref.shape[2]

    @pl.when(qi == 0)
    def _():
        lane = lax.broadcasted_iota(jnp.int32, (vp_ref.shape[1], V7X_LANES), 1)
        for g in range(n_heads):
            kp_ref[g, :, :d] = k_ref[0, :, g * d:(g + 1) * d]
            kp_ref[g, :, d:] = ak_ref[0, g]
            vp_ref[g, :, :d] = v_ref[0, :, g * d:(g + 1) * d]
            vp_ref[g, :, d:] = jnp.where(lane == 0, 1.0, 0.0).astype(BF16)

    qp = [jnp.concatenate([q_ref[0, :, g * d:(g + 1) * d], aq_ref[0, g]], axis=1) for g in range(n_heads)]
    m_ref[...] = jnp.full_like(m_ref, NEG_BIG)
    acc_ref[...] = jnp.zeros_like(acc_ref)

    def step(j, masked):
        start = pl.multiple_of(j * tk, tk)
        for g in range(n_heads):
            s = lax.dot_general(qp[g], kp_ref[g, pl.ds(start, tk), :], _NT, preferred_element_type=F32)
            if masked:
                row = lax.broadcasted_iota(jnp.int32, (tq, tk), 0)
                col = lax.broadcasted_iota(jnp.int32, (tq, tk), 1)
                s = jnp.where(row >= col, s, NEG_BIG)
            m_prev = m_ref[g]
            m_new = jnp.maximum(m_prev, jnp.max(s, axis=-1, keepdims=True))
            p = jnp.exp2(s - jnp.concatenate([m_new] * (tk // V7X_LANES), axis=1)).astype(BF16)
            alpha = jnp.exp2(m_prev - m_new)
            acc_ref[g] = (jnp.concatenate([alpha] * (dp // V7X_LANES), axis=1) * acc_ref[g]
                          + jnp.dot(p, vp_ref[g, pl.ds(start, tk), :], preferred_element_type=F32))
            m_ref[g] = m_new

    def body(i, carry):
        step(2 * i, False)
        step(2 * i + 1, False)
        return carry

    lax.fori_loop(0, lax.shift_right_logical(qi, 1), body, 0)

    @pl.when((qi & 1) == 1)
    def _():
        step(qi - 1, False)
        step(qi, True)

    @pl.when((qi & 1) == 0)
    def _():
        step(qi, True)

    for g in range(n_heads):
        acc = acc_ref[g]
        o = acc[:, :d] / acc[:, d:d + 1]
        o = o * lax.rsqrt(jnp.mean(o * o, axis=-1, keepdims=True) + EPS) * g_ref[g]
        o_ref[0, :, g * d:(g + 1) * d] = o.astype(o_ref.dtype)


def _fox(proj, aq, ak, norm_g, *, tq=FOX_TQ, tk=FOX_TK, group=FOX_GROUP):
    bsz, seq, _ = proj.shape
    d = FOX_HEAD_DIM
    dp = d + V7X_LANES
    gd = group * d
    n_groups = FOX_HEADS // group
    assert tq == tk and seq % tq == 0 and FOX_HEADS % group == 0
    blk = (2 * _nbytes((tq, gd), BF16) + 2 * _nbytes((seq, gd), BF16) + _nbytes((group, tq, V7X_LANES), BF16)
           + _nbytes((group, seq, V7X_LANES), BF16) + _nbytes((group, 8, d), F32))
    scratch = (2 * _nbytes((group, seq, dp), BF16) + _nbytes((group, tq, V7X_LANES), F32)
               + _nbytes((group, tq, dp), F32))
    return pl.pallas_call(
        functools.partial(_fox_kernel, tq=tq, tk=tk),
        grid=(bsz, n_groups, seq // tq),
        in_specs=[pl.BlockSpec((1, tq, gd), lambda b, h, i: (b, i, h)),
                  pl.BlockSpec((1, seq, gd), lambda b, h, i: (b, 0, n_groups + h)),
                  pl.BlockSpec((1, seq, gd), lambda b, h, i: (b, 0, 2 * n_groups + h)),
                  pl.BlockSpec((1, group, tq, V7X_LANES), lambda b, h, i: (b, h, i, 0)),
                  pl.BlockSpec((1, group, seq, V7X_LANES), lambda b, h, i: (b, h, 0, 0)),
                  pl.BlockSpec((group, 1, d), lambda b, h, i: (h, 0, 0))],
        out_specs=pl.BlockSpec((1, tq, gd), lambda b, h, i: (b, i, h)),
        out_shape=jax.ShapeDtypeStruct((bsz, seq, MIX_WIDTH), BF16),
        scratch_shapes=[pltpu.VMEM((group, seq, dp), BF16), pltpu.VMEM((group, seq, dp), BF16),
                        pltpu.VMEM((group, tq, V7X_LANES), F32), pltpu.VMEM((group, tq, dp), F32)],
        compiler_params=pltpu.CompilerParams(
            dimension_semantics=("arbitrary", "arbitrary", "arbitrary"),
            vmem_limit_bytes=_vmem_limit(blk, scratch + 8 * group * _nbytes((tq, tk), F32))),
        name="fox_attention",
    )(proj, proj, proj, aq, ak, norm_g)


def _gla_tables(chunk):
    idx = np.arange(chunk)
    masks = []
    half = chunk // 2
    while half >= 1:
        blk = idx // (2 * half)
        second = (idx // half) % 2 == 1
        masks.append(((blk[:, None] == blk[None, :]) & second[:, None] & ~second[None, :]).astype(np.float32))
        half //= 2
    return np.tril(np.ones((chunk, chunk), np.float32)), np.stack(masks, axis=0)


def _level_refs(b):
    chunk, width = b.shape
    refs = []
    half = chunk // 2
    while half >= 4:
        pieces = [jnp.broadcast_to(b[s + half - 1:s + half], (2 * half, width)) for s in range(0, chunk, 2 * half)]
        refs.append(pieces[0] if len(pieces) == 1 else jnp.concatenate(pieces, axis=0))
        half //= 2
    prev1 = pltpu.roll(b, 1, 0)
    prev2 = pltpu.roll(b, 2, 0)
    next1 = pltpu.roll(b, chunk - 1, 0)
    pos = lax.broadcasted_iota(jnp.int32, b.shape, 0) & 3
    refs.append(jnp.where(pos == 0, next1, jnp.where(pos == 1, b, jnp.where(pos == 2, prev1, prev2))))
    refs.append(jnp.where((pos & 1) == 0, b, prev1))
    return refs


def _gla_kernel(q_ref, k_ref, v_ref, gg_ref, gates_ref, wup_ref, bgk_ref, gn_ref, tri_ref, mask_ref,
                mixed_hbm_ref, o_ref, state_ref, la_ref, *, rows, chunk, scale):
    del mixed_hbm_ref
    dk, dv = GLA_KEY_DIM, GLA_VAL_DIM

    @pl.when(pl.program_id(1) == 0)
    def _():
        state_ref[...] = jnp.zeros_like(state_ref)

    pre = jnp.dot(gates_ref[0], wup_ref[...], preferred_element_type=F32,
                  precision=lax.Precision.HIGHEST) + bgk_ref[...]
    la_ref[...] = _log_sigmoid(pre) * (1.0 / GLA_GATE_NORMALIZER)

    n_levels = mask_ref.shape[0]
    eye = (lax.broadcasted_iota(jnp.int32, (chunk, chunk), 0)
           == lax.broadcasted_iota(jnp.int32, (chunk, chunk), 1))
    row_dk = lax.broadcasted_iota(jnp.int32, (chunk, dk), 0)
    ones_cols = jnp.ones((chunk, V7X_LANES), BF16)
    tri = tri_ref[...]

    def head_chunk(r0, h):
        ks = slice(h * dk, (h + 1) * dk)
        vs = slice(h * dv, (h + 1) * dv)
        g = la_ref[pl.ds(r0, chunk), ks]
        g_parts = jnp.concatenate([p.astype(BF16) for p in _split3(g)], axis=1)
        b3 = jnp.dot(tri, g_parts, preferred_element_type=F32)
        b = b3[:, :dk] + b3[:, dk:2 * dk] + b3[:, 2 * dk:]
        q = q_ref[0, pl.ds(r0, chunk), ks].astype(F32) * scale
        k = k_ref[0, pl.ds(r0, chunk), ks].astype(F32)
        v = v_ref[0, pl.ds(r0, chunk), vs]

        state = state_ref[h]
        o = jnp.dot((q * jnp.exp(b)).astype(BF16), state.astype(BF16), preferred_element_type=F32)

        scores = jnp.zeros((chunk, chunk), F32)
        for lvl, ref in enumerate(_level_refs(b)):
            qt = (q * jnp.exp(jnp.minimum(b - ref, 0.0))).astype(BF16)
            kt = (k * jnp.exp(jnp.minimum(ref - b, 0.0))).astype(BF16)
            scores = scores + mask_ref[lvl] * lax.dot_general(qt, kt, _NT, preferred_element_type=F32)
        scores = jnp.where(eye, jnp.sum(q * k, axis=-1, keepdims=True), scores)
        o = o + jnp.dot(scores.astype(BF16), v, preferred_element_type=F32)

        b_last = b[chunk - 1:chunk]
        k_dec = (k * jnp.exp(b_last - b)).astype(BF16)
        e1, e2, e3 = _split3(jnp.broadcast_to(jnp.exp(b_last), (chunk, dk)))
        e_rows = jnp.where(row_dk == 0, e1, jnp.where(row_dk == 1, e2, jnp.where(row_dk == 2, e3, 0.0)))
        dec_col = lax.dot_general(e_rows.astype(BF16), ones_cols, _TN, preferred_element_type=F32)
        decay = jnp.concatenate([dec_col] * (dv // V7X_LANES), axis=1)
        state_ref[h] = state * decay + lax.dot_general(k_dec, v, _TN, preferred_element_type=F32)

        o = o * lax.rsqrt(jnp.mean(o * o, axis=-1, keepdims=True) + EPS) * gn_ref[...]
        gate = _silu(gg_ref[0, pl.ds(r0, chunk), vs].astype(F32))
        o_ref[0, pl.ds(r0, chunk), vs] = (o * gate).astype(o_ref.dtype)

    assert n_levels == chunk.bit_length() - 1

    def chunk_body(c, carry):
        r0 = pl.multiple_of(c * chunk, chunk)
        for h in range(GLA_HEADS):
            head_chunk(r0, h)
        return carry

    lax.fori_loop(0, rows // chunk, chunk_body, 0)


def _gla(proj, col0, gates, w_up_pad, b_gk, norm_g, mixed, *, rows=GLA_ROWS, chunk=GLA_CHUNK):
    bsz, seq, _ = proj.shape
    kw, vw = GLA_KEY_WIDTH, GLA_WIDTH
    assert seq % rows == 0 and rows % chunk == 0 and FOX_WIDTH == vw
    assert col0 % kw == 0 and (col0 + 2 * kw) % vw == 0
    q_blk = col0 // kw
    v_blk = (col0 + 2 * kw) // vw
    tri, masks = _gla_tables(chunk)
    blk = (2 * _nbytes((rows, kw), BF16) + 3 * _nbytes((rows, vw), BF16) + _nbytes((rows, V7X_LANES), F32)
           + _nbytes((V7X_LANES, kw), F32) + _nbytes(masks.shape, F32))
    scratch = _nbytes((GLA_HEADS, GLA_KEY_DIM, GLA_VAL_DIM), F32) + _nbytes((rows, kw), F32)
    return pl.pallas_call(
        functools.partial(_gla_kernel, rows=rows, chunk=chunk, scale=GLA_KEY_DIM ** -0.5),
        grid=(bsz, seq // rows),
        in_specs=[pl.BlockSpec((1, rows, kw), lambda b, t: (b, t, q_blk)),
                  pl.BlockSpec((1, rows, kw), lambda b, t: (b, t, q_blk + 1)),
                  pl.BlockSpec((1, rows, vw), lambda b, t: (b, t, v_blk)),
                  pl.BlockSpec((1, rows, vw), lambda b, t: (b, t, v_blk + 1)),
                  pl.BlockSpec((1, rows, V7X_LANES), lambda b, t: (b, t, 0)),
                  pl.BlockSpec((V7X_LANES, kw), lambda b, t: (0, 0)),
                  pl.BlockSpec((1, kw), lambda b, t: (0, 0)),
                  pl.BlockSpec((1, GLA_VAL_DIM), lambda b, t: (0, 0)),
                  pl.BlockSpec(tri.shape, lambda b, t: (0, 0)),
                  pl.BlockSpec(masks.shape, lambda b, t: (0, 0, 0)),
                  pl.BlockSpec(memory_space=pl.ANY)],
        out_specs=pl.BlockSpec((1, rows, vw), lambda b, t: (b, t, 1)),
        out_shape=jax.ShapeDtypeStruct(mixed.shape, mixed.dtype),
        scratch_shapes=[pltpu.VMEM((GLA_HEADS, GLA_KEY_DIM, GLA_VAL_DIM), F32), pltpu.VMEM((rows, kw), F32)],
        input_output_aliases={10: 0},
        compiler_params=pltpu.CompilerParams(
            dimension_semantics=("arbitrary", "arbitrary"),
            vmem_limit_bytes=_vmem_limit(blk, scratch + (8 << 20))),
        name="gla",
    )(proj, proj, proj, proj, gates, w_up_pad, b_gk.reshape(1, -1), norm_g.reshape(1, -1),
      jnp.asarray(tri, BF16), jnp.asarray(masks), mixed)


def _prep_layer(b_f, w_gk_up):
    gate_bias = jnp.concatenate([b_f, jnp.zeros((V7X_LANES - FOX_HEADS,), F32)]).reshape(1, V7X_LANES)
    w_up_pad = jnp.zeros((V7X_LANES, GLA_KEY_WIDTH), F32).at[FOX_HEADS:FOX_HEADS + GLA_GATE_RANK].set(w_gk_up)
    return gate_bias, w_up_pad


def kernel(x, ln_in_g, ln_in_b, w_in, b_f, w_gk_up, b_gk, fox_norm_g, gla_norm_g, w_out, ln1_g, ln1_b,
           w_gate, w_up, w_down, ln2_g, ln2_b):
    bsz, seq, d_model = x.shape
    depth = w_in.shape[0]
    alpha = (2 * depth) ** 0.25
    rows = bsz * seq

    w_in_t = jnp.swapaxes(w_in, 1, 2)
    n_fox_tiles = IN_F_LO // MM_TN
    n_proj_tiles = n_fox_tiles + (IN_LOW_LO - IN_G_LO) // MM_TN

    def proj_row_start(j):
        return pl.multiple_of(j * MM_TN + jnp.where(j >= n_fox_tiles, IN_G_LO - IN_F_LO, 0), F32_SUBLANES)

    x32, x16 = _layer_norm(x.reshape(rows, d_model), None, ln_in_g, ln_in_b)
    for l in range(depth):
        gate_bias, w_up_pad = _prep_layer(b_f[l], w_gk_up[l])

        proj = _matmul_wt(x16, w_in_t, l, n_proj_tiles, proj_row_start, BF16, scaled_cols=FOX_WIDTH,
                          scale=FOX_HEAD_DIM ** -0.5 * LOG2_E).reshape(bsz, seq, -1)
        gates, aug_q, aug_k = _gates(x16, w_in_t, l, gate_bias, bsz, seq)
        gates = gates.reshape(bsz, seq, V7X_LANES)

        mixed = _fox(proj, aug_q, aug_k, fox_norm_g[l].reshape(FOX_HEADS, 1, FOX_HEAD_DIM))
        mixed = _gla(proj, IN_F_LO, gates, w_up_pad, b_gk[l], gla_norm_g[l], mixed)

        mix = _matmul(mixed.reshape(rows, MIX_WIDTH), w_out, l, d_model, F32)
        x32, x16 = _layer_norm(x32, mix, ln1_g[l], ln1_b[l], alpha=alpha)

        hidden = _gate_up(x16, w_gate, w_up, l)
        ffn = _down(hidden, w_down, l)
        last = l == depth - 1
        outs = _layer_norm(x32, ffn, ln2_g[l], ln2_b[l], alpha=alpha, want_bf16=not last)
        if last:
            x32, = outs
        else:
            x32, x16 = outs
    return x32.reshape(bsz, seq, d_model)
```

```python
import functools

import numpy as np
import jax
import jax.numpy as jnp
from jax import lax
from jax.experimental import pallas as pl
from jax.experimental.pallas import tpu as pltpu

F32 = jnp.float32
BF16 = jnp.bfloat16

V7X_VMEM_BYTES = 64 * 2**20
V7X_LANES = 128
V7X_MXU_DIM = 256
F32_SUBLANES = 8
BF16_SUBLANES = 16

FOX_HEADS = 16
FOX_HEAD_DIM = 128
FOX_WIDTH = FOX_HEADS * FOX_HEAD_DIM
GLA_HEADS = 4
GLA_KEY_DIM = 256
GLA_VAL_DIM = 512
GLA_KEY_WIDTH = GLA_HEADS * GLA_KEY_DIM
GLA_WIDTH = GLA_HEADS * GLA_VAL_DIM
GLA_GATE_RANK = 16
GLA_GATE_NORMALIZER = 16.0
MIX_WIDTH = FOX_WIDTH + GLA_WIDTH
EPS = 1e-5
IN_F_LO = 3 * FOX_WIDTH
IN_G_LO = IN_F_LO + FOX_HEADS
IN_LOW_LO = IN_G_LO + 2 * GLA_KEY_WIDTH + 2 * GLA_WIDTH

LN_ROWS = 256
MM_TM = 1024
MM_TN = 1024
FFN_TN = 512
FFN_PAD = 512
DOWN_TM = 512
DOWN_TN = 512
GATE_ROWS = 512
FOX_TQ = 512
FOX_TK = 512
FOX_GROUP = 4
GLA_ROWS = 512
GLA_CHUNK = 64
NEG_BIG = -1e30
LOG2_E = 1.4426950408889634

_NT = (((1,), (1,)), ((), ()))
_TN = (((0,), (0,)), ((), ()))


def _vmem_limit(block_bytes, extra_bytes):
    need = 2 * block_bytes + extra_bytes + (4 << 20)
    return int(min(need, V7X_VMEM_BYTES - (6 << 20)))


def _nbytes(shape, dtype):
    return int(np.prod(shape)) * jnp.dtype(dtype).itemsize


def _log_sigmoid(x):
    return jnp.minimum(x, 0.0) - jnp.log1p(jnp.exp(-jnp.abs(x)))


def _silu(x):
    return x * (1.0 / (1.0 + jnp.exp(-x)))


def _ln_math(t, g, b):
    mu = jnp.mean(t, axis=-1, keepdims=True)
    tc = t - mu
    var = jnp.mean(tc * tc, axis=-1, keepdims=True)
    return tc * lax.rsqrt(var + EPS) * g + b


def _ln_kernel(*refs, alpha, has_res, want_f32, want_bf16):
    if has_res:
        x_ref, y_ref, g_ref, b_ref = refs[:4]
        outs = refs[4:]
        t = alpha * x_ref[...] + y_ref[...]
    else:
        x_ref, g_ref, b_ref = refs[:3]
        outs = refs[3:]
        t = x_ref[...]
    y = _ln_math(t, g_ref[...], b_ref[...])
    i = 0
    if want_f32:
        outs[i][...] = y
        i += 1
    if want_bf16:
        outs[i][...] = y.astype(BF16)


def _layer_norm(x, res, g, b, *, alpha=1.0, want_f32=True, want_bf16=True):
    rows, d = x.shape
    assert rows % LN_ROWS == 0
    row_spec = pl.BlockSpec((LN_ROWS, d), lambda i: (i, 0))
    vec_spec = pl.BlockSpec((1, d), lambda i: (0, 0))
    ins = [x] + ([res] if res is not None else []) + [g.reshape(1, d), b.reshape(1, d)]
    in_specs = [row_spec] * (len(ins) - 2) + [vec_spec, vec_spec]
    out_shape, out_specs = [], []
    if want_f32:
        out_shape.append(jax.ShapeDtypeStruct((rows, d), F32))
        out_specs.append(row_spec)
    if want_bf16:
        out_shape.append(jax.ShapeDtypeStruct((rows, d), BF16))
        out_specs.append(row_spec)
    blk = _nbytes((LN_ROWS, d), F32) * (len(ins) - 2 + len(out_shape))
    return pl.pallas_call(
        functools.partial(_ln_kernel, alpha=alpha, has_res=res is not None,
                          want_f32=want_f32, want_bf16=want_bf16),
        grid=(rows // LN_ROWS,),
        in_specs=in_specs, out_specs=out_specs, out_shape=out_shape,
        compiler_params=pltpu.CompilerParams(
            dimension_semantics=("arbitrary",),
            vmem_limit_bytes=_vmem_limit(blk, 4 * _nbytes((LN_ROWS, d), F32))),
        name="layer_norm",
    )(*ins)


def _ws_kernel(x_ref, *rest, n_w, transposed, epilogue, k_valid):
    w_refs, o_ref, wb_ref = rest[:n_w], rest[n_w], rest[n_w + 1]
    j, i = pl.program_id(0), pl.program_id(1)
    tn = wb_ref.shape[2] // n_w
    ck = w_refs[0].shape[-1] if transposed else w_refs[0].shape[0]

    def convert_chunk(slot):
        rows = pl.ds(pl.multiple_of(i * ck, ck), ck)
        for t, w_ref in enumerate(w_refs):
            chunk = w_ref[0].T if transposed else w_ref[...]
            if k_valid < wb_ref.shape[1]:
                row = i * ck + lax.broadcasted_iota(jnp.int32, chunk.shape, 0)
                chunk = jnp.where(row < k_valid, chunk, 0.0)
            wb_ref[slot, rows, t * tn:(t + 1) * tn] = chunk.astype(BF16)

    @pl.when(j == 0)
    def _():
        convert_chunk(0)

    for parity in (0, 1):
        @pl.when(jnp.logical_and(j > 0, (j & 1) == parity))
        def _():
            convert_chunk(parity)
            acc = jnp.dot(x_ref[...], wb_ref[1 - parity], preferred_element_type=F32)
            o_ref[...] = epilogue(acc, j - 1).astype(o_ref.dtype)


def _ws_matmul(x, weights, w_specs, n_tiles, tn_w, tn_out, out_dtype, epilogue, *, transposed, name, tm=MM_TM,
               k_valid=None):
    m, kdim = x.shape
    n_m = m // tm
    ck = kdim // n_m
    assert m % tm == 0 and kdim % n_m == 0 and ck % (V7X_LANES if transposed else BF16_SUBLANES) == 0
    k_valid = kdim if k_valid is None else k_valid
    n_w = len(weights)
    blk = (_nbytes((tm, kdim), x.dtype) + n_w * _nbytes((ck, tn_w), F32) + _nbytes((tm, tn_out), out_dtype))
    scratch = _nbytes((2, kdim, n_w * tn_w), BF16)
    extra = scratch + 2 * _nbytes((tm, n_w * tn_w), F32) + 3 * n_w * _nbytes((ck, tn_w), F32)
    return pl.pallas_call(
        functools.partial(_ws_kernel, n_w=n_w, transposed=transposed, epilogue=epilogue, k_valid=k_valid),
        grid=(n_tiles + 1, n_m),
        in_specs=[pl.BlockSpec((tm, kdim), lambda j, i: (jnp.where(j == 0, 0, i), 0))] + w_specs(ck),
        out_specs=pl.BlockSpec((tm, tn_out), lambda j, i: (jnp.where(j == 0, 0, i), jnp.maximum(j - 1, 0))),
        out_shape=jax.ShapeDtypeStruct((m, n_tiles * tn_out), out_dtype),
        scratch_shapes=[pltpu.VMEM((2, kdim, n_w * tn_w), BF16)],
        compiler_params=pltpu.CompilerParams(
            dimension_semantics=("arbitrary", "arbitrary"),
            vmem_limit_bytes=_vmem_limit(blk, extra)),
        name=name,
    )(x, *weights)


def _matmul(x, w, layer, n_cols, out_dtype, *, tn=MM_TN):
    assert n_cols % tn == 0 and w.shape[1] == x.shape[1]
    n_tiles = n_cols // tn

    def w_specs(ck):
        return [pl.BlockSpec((None, ck, tn), lambda j, i: (layer, i, jnp.minimum(j, n_tiles - 1)))]

    return _ws_matmul(x, [w], w_specs, n_tiles, tn, tn, out_dtype, lambda acc, tile: acc,
                      transposed=False, name="matmul")


def _matmul_wt(x, w_t, layer, n_tiles, row_start, out_dtype, *, tn=MM_TN, scaled_cols=0, scale=1.0):
    assert w_t.shape[2] == x.shape[1] and scaled_cols % tn == 0
    scaled_tiles = scaled_cols // tn

    def w_specs(ck):
        return [pl.BlockSpec((pl.Element(1), pl.Element(tn), pl.Element(ck)),
                             lambda j, i: (layer, row_start(jnp.minimum(j, n_tiles - 1)), i * ck))]

    def epilogue(acc, tile):
        return acc * jnp.where(tile < scaled_tiles, scale, 1.0) if scaled_tiles else acc

    return _ws_matmul(x, [w_t], w_specs, n_tiles, tn, tn, out_dtype, epilogue, transposed=True, name="matmul_wt")


def _gate_up(x, w_gate, w_up, layer, *, tn=FFN_TN):
    d_ff = w_gate.shape[2]
    n_out = pl.cdiv(d_ff, FFN_PAD) * FFN_PAD
    assert n_out % tn == 0 and w_gate.shape[1] == x.shape[1]
    last_w_blk = pl.cdiv(d_ff, tn) - 1

    def w_specs(ck):
        return [pl.BlockSpec((None, ck, tn), lambda j, i: (layer, i, jnp.minimum(j, last_w_blk)))] * 2

    def epilogue(z, tile):
        h = _silu(z[:, :tn]) * z[:, tn:]
        col = tile * tn + lax.broadcasted_iota(jnp.int32, h.shape, 1)
        return jnp.where(col < d_ff, h, 0.0)

    return _ws_matmul(x, [w_gate, w_up], w_specs, n_out // tn, tn, tn, BF16, epilogue,
                      transposed=False, name="gate_up")


def _down(h, w_down, layer, *, tm=DOWN_TM, tn=DOWN_TN):
    d_ff, n = w_down.shape[1:]
    assert n % tn == 0 and h.shape[1] >= d_ff
    n_tiles = n // tn

    def w_specs(ck):
        assert h.shape[1] - d_ff < ck
        return [pl.BlockSpec((None, ck, tn), lambda j, i: (layer, i, jnp.minimum(j, n_tiles - 1)))]

    return _ws_matmul(h, [w_down], w_specs, n_tiles, tn, tn, F32, lambda acc, tile: acc,
                      transposed=False, name="down_proj", tm=tm, k_valid=d_ff)


def _split3(x):
    a = x.astype(BF16).astype(F32)
    r = x - a
    b = r.astype(BF16).astype(F32)
    return a, b, (r - b).astype(BF16).astype(F32)


def _dot_split2(a, b):
    a_hi = a.astype(BF16)
    b_hi = b.astype(BF16)
    a_lo = (a - a_hi.astype(F32)).astype(BF16)
    b_lo = (b - b_hi.astype(F32)).astype(BF16)
    return (jnp.dot(a_hi, b_hi, preferred_element_type=F32) + jnp.dot(a_hi, b_lo, preferred_element_type=F32)
            + jnp.dot(a_lo, b_hi, preferred_element_type=F32))


def _gates_kernel(x_ref, wf_ref, wl_ref, bias_ref, tri_ref, o_ref, aq_ref, ak_ref, carry_ref):
    @pl.when(pl.program_id(1) == 0)
    def _():
        carry_ref[...] = jnp.zeros_like(carry_ref)

    wf, wl = wf_ref[0], wl_ref[0]
    pad = jnp.zeros((V7X_LANES - wf.shape[0] - wl.shape[0], wf.shape[1]), F32)
    w = jnp.concatenate([wf, wl, pad], axis=0).astype(BF16)
    z = lax.dot_general(x_ref[...], w, _NT, preferred_element_type=F32)
    log_f = _log_sigmoid(z + bias_ref[...])
    parts = jnp.concatenate([p.astype(BF16) for p in _split3(log_f)], axis=1)
    c3 = jnp.dot(tri_ref[...], parts, preferred_element_type=F32)
    nl = log_f.shape[1]
    c = c3[:, :nl] + c3[:, nl:2 * nl] + c3[:, 2 * nl:] + carry_ref[0:1, :]
    rows = c.shape[0]
    carry_ref[...] = jnp.broadcast_to(c[rows - 1:rows, :], carry_ref.shape)
    lane = lax.broadcasted_iota(jnp.int32, c.shape, 1)
    o_ref[...] = jnp.where(lane < FOX_HEADS, c, z)

    c2 = c * LOG2_E
    for h in range(FOX_HEADS):
        c1, c2_, c3 = _split3(jnp.broadcast_to(c2[:, h:h + 1], c.shape))
        aq = jnp.where(lane == 0, c1, jnp.where(lane == 1, c2_, jnp.where(lane == 2, c3,
                       jnp.where(lane < 6, 1.0, 0.0))))
        ak = jnp.where(lane < 3, 1.0, jnp.where(lane == 3, -c1, jnp.where(lane == 4, -c2_,
                       jnp.where(lane == 5, -c3, 0.0))))
        aq_ref[0, h] = aq.astype(BF16)
        ak_ref[0, h] = ak.astype(BF16)


def _gates(x, w_in_t, layer, bias, bsz, seq):
    m, kdim = x.shape
    steps = seq // GATE_ROWS
    tri = jnp.asarray(np.tril(np.ones((GATE_ROWS, GATE_ROWS), np.float32)), BF16)
    aug_shape = jax.ShapeDtypeStruct((bsz, FOX_HEADS, seq, V7X_LANES), BF16)
    aug_spec = pl.BlockSpec((1, FOX_HEADS, GATE_ROWS, V7X_LANES), lambda b, i: (b, 0, i, 0))
    assert IN_F_LO % F32_SUBLANES == 0 and IN_LOW_LO % F32_SUBLANES == 0

    def rows_spec(n_rows, row0):
        return pl.BlockSpec((pl.Element(1), pl.Element(n_rows), pl.Element(kdim)), lambda b, i: (layer, row0, 0))

    blk = (_nbytes((GATE_ROWS, kdim), x.dtype) + _nbytes((V7X_LANES, kdim), F32)
           + _nbytes((GATE_ROWS, GATE_ROWS), F32) + _nbytes((GATE_ROWS, V7X_LANES), F32)
           + 2 * _nbytes((FOX_HEADS, GATE_ROWS, V7X_LANES), BF16))
    return pl.pallas_call(
        _gates_kernel,
        grid=(bsz, steps),
        in_specs=[pl.BlockSpec((GATE_ROWS, kdim), lambda b, i: (b * steps + i, 0)),
                  rows_spec(FOX_HEADS, IN_F_LO), rows_spec(GLA_GATE_RANK, IN_LOW_LO),
                  pl.BlockSpec((1, V7X_LANES), lambda b, i: (0, 0)),
                  pl.BlockSpec((GATE_ROWS, GATE_ROWS), lambda b, i: (0, 0))],
        out_specs=[pl.BlockSpec((GATE_ROWS, V7X_LANES), lambda b, i: (b * steps + i, 0)), aug_spec, aug_spec],
        out_shape=[jax.ShapeDtypeStruct((m, V7X_LANES), F32), aug_shape, aug_shape],
        scratch_shapes=[pltpu.VMEM((8, V7X_LANES), F32)],
        compiler_params=pltpu.CompilerParams(
            dimension_semantics=("arbitrary", "arbitrary"),
            vmem_limit_bytes=_vmem_limit(blk, 16 * _nbytes((GATE_ROWS, V7X_LANES), F32))),
        name="gates",
    )(x, w_in_t, w_in_t, bias, tri)


def _fox_kernel(q_ref, k_ref, v_ref, aq_ref, ak_ref, g_ref, o_ref, kp_ref, vp_ref, m_ref, acc_ref, *, tq, tk):
    qi = pl.program_id(2)
    n_heads = aq_ref.shape[1]
    d = q_ref.shape[2] // n_heads
    dp = acc_ref.shape[2]

    @pl.when(qi == 0)
    def _():
        lane = lax.broadcasted_iota(jnp.int32, (vp_ref.shape[1], V7X_LANES), 1)
        for g in range(n_heads):
            kp_ref[g, :, :d] = k_ref[0, :, g * d:(g + 1) * d]
            kp_ref[g, :, d:] = ak_ref[0, g]
            vp_ref[g, :, :d] = v_ref[0, :, g * d:(g + 1) * d]
            vp_ref[g, :, d:] = jnp.where(lane == 0, 1.0, 0.0).astype(BF16)

    qp = [jnp.concatenate([q_ref[0, :, g * d:(g + 1) * d], aq_ref[0, g]], axis=1) for g in range(n_heads)]
    m_ref[...] = jnp.full_like(m_ref, NEG_BIG)
    acc_ref[...] = jnp.zeros_like(acc_ref)

    def step(j, masked):
        start = pl.multiple_of(j * tk, tk)
        for g in range(n_heads):
            s = lax.dot_general(qp[g], kp_ref[g, pl.ds(start, tk), :], _NT, preferred_element_type=F32)
            if masked:
                row = lax.broadcasted_iota(jnp.int32, (tq, tk), 0)
                col = lax.broadcasted_iota(jnp.int32, (tq, tk), 1)
                s = jnp.where(row >= col, s, NEG_BIG)
            m_prev = m_ref[g]
            m_new = jnp.maximum(m_prev, jnp.max(s, axis=-1, keepdims=True))
            p = jnp.exp2(s - jnp.concatenate([m_new] * (tk // V7X_LANES), axis=1)).astype(BF16)
            alpha = jnp.exp2(m_prev - m_new)
            acc_ref[g] = (jnp.concatenate([alpha] * (dp // V7X_LANES), axis=1) * acc_ref[g]
                          + jnp.dot(p, vp_ref[g, pl.ds(start, tk), :], preferred_element_type=F32))
            m_ref[g] = m_new

    def body(i, carry):
        step(2 * i, False)
        step(2 * i + 1, False)
        return carry

    lax.fori_loop(0, lax.shift_right_logical(qi, 1), body, 0)

    @pl.when((qi & 1) == 1)
    def _():
        step(qi - 1, False)
        step(qi, True)

    @pl.when((qi & 1) == 0)
    def _():
        step(qi, True)

    for g in range(n_heads):
        acc = acc_ref[g]
        o = acc[:, :d] / acc[:, d:d + 1]
        o = o * lax.rsqrt(jnp.mean(o * o, axis=-1, keepdims=True) + EPS) * g_ref[g]
        o_ref[0, :, g * d:(g + 1) * d] = o.astype(o_ref.dtype)


def _fox(proj, aq, ak, norm_g, *, tq=FOX_TQ, tk=FOX_TK, group=FOX_GROUP):
    bsz, seq, _ = proj.shape
    d = FOX_HEAD_DIM
    dp = d + V7X_LANES
    gd = group * d
    n_groups = FOX_HEADS // group
    assert tq == tk and seq % tq == 0 and FOX_HEADS % group == 0
    blk = (2 * _nbytes((tq, gd), BF16) + 2 * _nbytes((seq, gd), BF16) + _nbytes((group, tq, V7X_LANES), BF16)
           + _nbytes((group, seq, V7X_LANES), BF16) + _nbytes((group, 8, d), F32))
    scratch = (2 * _nbytes((group, seq, dp), BF16) + _nbytes((group, tq, V7X_LANES), F32)
               + _nbytes((group, tq, dp), F32))
    return pl.pallas_call(
        functools.partial(_fox_kernel, tq=tq, tk=tk),
        grid=(bsz, n_groups, seq // tq),
        in_specs=[pl.BlockSpec((1, tq, gd), lambda b, h, i: (b, i, h)),
                  pl.BlockSpec((1, seq, gd), lambda b, h, i: (b, 0, n_groups + h)),
                  pl.BlockSpec((1, seq, gd), lambda b, h, i: (b, 0, 2 * n_groups + h)),
                  pl.BlockSpec((1, group, tq, V7X_LANES), lambda b, h, i: (b, h, i, 0)),
                  pl.BlockSpec((1, group, seq, V7X_LANES), lambda b, h, i: (b, h, 0, 0)),
                  pl.BlockSpec((group, 1, d), lambda b, h, i: (h, 0, 0))],
        out_specs=pl.BlockSpec((1, tq, gd), lambda b, h, i: (b, i, h)),
        out_shape=jax.ShapeDtypeStruct((bsz, seq, MIX_WIDTH), BF16),
        scratch_shapes=[pltpu.VMEM((group, seq, dp), BF16), pltpu.VMEM((group, seq, dp), BF16),
                        pltpu.VMEM((group, tq, V7X_LANES), F32), pltpu.VMEM((group, tq, dp), F32)],
        compiler_params=pltpu.CompilerParams(
            dimension_semantics=("arbitrary", "arbitrary", "arbitrary"),
            vmem_limit_bytes=_vmem_limit(blk, scratch + 8 * group * _nbytes((tq, tk), F32))),
        name="fox_attention",
    )(proj, proj, proj, aq, ak, norm_g)


def _gla_tables(chunk):
    idx = np.arange(chunk)
    masks = []
    half = chunk // 2
    while half >= 1:
        blk = idx // (2 * half)
        second = (idx // half) % 2 == 1
        masks.append(((blk[:, None] == blk[None, :]) & second[:, None] & ~second[None, :]).astype(np.float32))
        half //= 2
    return np.tril(np.ones((chunk, chunk), np.float32)), np.stack(masks, axis=0)


def _level_refs(b):
    chunk, width = b.shape
    refs = []
    half = chunk // 2
    while half >= 4:
        pieces = [jnp.broadcast_to(b[s + half - 1:s + half], (2 * half, width)) for s in range(0, chunk, 2 * half)]
        refs.append(pieces[0] if len(pieces) == 1 else jnp.concatenate(pieces, axis=0))
        half //= 2
    prev1 = pltpu.roll(b, 1, 0)
    prev2 = pltpu.roll(b, 2, 0)
    next1 = pltpu.roll(b, chunk - 1, 0)
    pos = lax.broadcasted_iota(jnp.int32, b.shape, 0) & 3
    refs.append(jnp.where(pos == 0, next1, jnp.where(pos == 1, b, jnp.where(pos == 2, prev1, prev2))))
    refs.append(jnp.where((pos & 1) == 0, b, prev1))
    return refs


def _gla_kernel(q_ref, k_ref, v_ref, gg_ref, gates_ref, wup_ref, bgk_ref, gn_ref, tri_ref, mask_ref,
                mixed_hbm_ref, o_ref, state_ref, la_ref, *, rows, chunk, scale):
    del mixed_hbm_ref
    dk, dv = GLA_KEY_DIM, GLA_VAL_DIM

    @pl.when(pl.program_id(1) == 0)
    def _():
        state_ref[...] = jnp.zeros_like(state_ref)

    pre = _dot_split2(gates_ref[0], wup_ref[...]) + bgk_ref[...]
    la_ref[...] = _log_sigmoid(pre) * (LOG2_E / GLA_GATE_NORMALIZER)

    n_levels = mask_ref.shape[0]
    eye = (lax.broadcasted_iota(jnp.int32, (chunk, chunk), 0)
           == lax.broadcasted_iota(jnp.int32, (chunk, chunk), 1))
    row_dk = lax.broadcasted_iota(jnp.int32, (chunk, dk), 0)
    row_e = lax.broadcasted_iota(jnp.int32, (BF16_SUBLANES, dk), 0)
    ones_cols = jnp.ones((BF16_SUBLANES, V7X_LANES), BF16)
    tri = tri_ref[...]

    def head_chunk(r0, h):
        ks = slice(h * dk, (h + 1) * dk)
        vs = slice(h * dv, (h + 1) * dv)
        g = la_ref[pl.ds(r0, chunk), ks]
        g_parts = jnp.concatenate([p.astype(BF16) for p in _split3(g)], axis=1)
        b3 = jnp.dot(tri, g_parts, preferred_element_type=F32)
        b = b3[:, :dk] + b3[:, dk:2 * dk] + b3[:, 2 * dk:]
        q = q_ref[0, pl.ds(r0, chunk), ks].astype(F32) * scale
        k = k_ref[0, pl.ds(r0, chunk), ks].astype(F32)
        v = v_ref[0, pl.ds(r0, chunk), vs]

        state = state_ref[h]
        o = jnp.dot((q * jnp.exp2(b)).astype(BF16), state.astype(BF16), preferred_element_type=F32)

        scores = jnp.zeros((chunk, chunk), F32)
        half = chunk // 2
        for lvl, ref in enumerate(_level_refs(b)):
            u = (jnp.where((row_dk & half) != 0, q, k) * jnp.exp2(-jnp.abs(b - ref))).astype(BF16)
            scores = scores + mask_ref[lvl] * lax.dot_general(u, u, _NT, preferred_element_type=F32)
            half //= 2
        scores = jnp.where(eye, jnp.sum(q * k, axis=-1, keepdims=True), scores)
        o = o + jnp.dot(scores.astype(BF16), v, preferred_element_type=F32)

        b_last = b[chunk - 1:chunk]
        k_dec = (k * jnp.exp2(b_last - b)).astype(BF16)
        e1, e2, e3 = _split3(jnp.broadcast_to(jnp.exp2(b_last), row_e.shape))
        e_rows = jnp.where(row_e == 0, e1, jnp.where(row_e == 1, e2, jnp.where(row_e == 2, e3, 0.0)))
        dec_col = lax.dot_general(e_rows.astype(BF16), ones_cols, _TN, preferred_element_type=F32)
        decay = jnp.concatenate([dec_col] * (dv // V7X_LANES), axis=1)
        state_ref[h] = state * decay + lax.dot_general(k_dec, v, _TN, preferred_element_type=F32)

        o = o * lax.rsqrt(jnp.mean(o * o, axis=-1, keepdims=True) + EPS) * gn_ref[...]
        gate = _silu(gg_ref[0, pl.ds(r0, chunk), vs].astype(F32))
        o_ref[0, pl.ds(r0, chunk), vs] = (o * gate).astype(o_ref.dtype)

    assert n_levels == chunk.bit_length() - 1

    def chunk_body(c, carry):
        r0 = pl.multiple_of(c * chunk, chunk)
        for h in range(GLA_HEADS):
            head_chunk(r0, h)
        return carry

    lax.fori_loop(0, rows // chunk, chunk_body, 0)


def _gla(proj, col0, gates, w_up_pad, b_gk, norm_g, mixed, *, rows=GLA_ROWS, chunk=GLA_CHUNK):
    bsz, seq, _ = proj.shape
    kw, vw = GLA_KEY_WIDTH, GLA_WIDTH
    assert seq % rows == 0 and rows % chunk == 0 and FOX_WIDTH == vw
    assert col0 % kw == 0 and (col0 + 2 * kw) % vw == 0
    q_blk = col0 // kw
    v_blk = (col0 + 2 * kw) // vw
    tri, masks = _gla_tables(chunk)
    blk = (2 * _nbytes((rows, kw), BF16) + 3 * _nbytes((rows, vw), BF16) + _nbytes((rows, V7X_LANES), F32)
           + _nbytes((V7X_LANES, kw), F32) + _nbytes(masks.shape, F32))
    scratch = _nbytes((GLA_HEADS, GLA_KEY_DIM, GLA_VAL_DIM), F32) + _nbytes((rows, kw), F32)
    return pl.pallas_call(
        functools.partial(_gla_kernel, rows=rows, chunk=chunk, scale=GLA_KEY_DIM ** -0.5),
        grid=(bsz, seq // rows),
        in_specs=[pl.BlockSpec((1, rows, kw), lambda b, t: (b, t, q_blk)),
                  pl.BlockSpec((1, rows, kw), lambda b, t: (b, t, q_blk + 1)),
                  pl.BlockSpec((1, rows, vw), lambda b, t: (b, t, v_blk)),
                  pl.BlockSpec((1, rows, vw), lambda b, t: (b, t, v_blk + 1)),
                  pl.BlockSpec((1, rows, V7X_LANES), lambda b, t: (b, t, 0)),
                  pl.BlockSpec((V7X_LANES, kw), lambda b, t: (0, 0)),
                  pl.BlockSpec((1, kw), lambda b, t: (0, 0)),
                  pl.BlockSpec((1, GLA_VAL_DIM), lambda b, t: (0, 0)),
                  pl.BlockSpec(tri.shape, lambda b, t: (0, 0)),
                  pl.BlockSpec(masks.shape, lambda b, t: (0, 0, 0)),
                  pl.BlockSpec(memory_space=pl.ANY)],
        out_specs=pl.BlockSpec((1, rows, vw), lambda b, t: (b, t, 1)),
        out_shape=jax.ShapeDtypeStruct(mixed.shape, mixed.dtype),
        scratch_shapes=[pltpu.VMEM((GLA_HEADS, GLA_KEY_DIM, GLA_VAL_DIM), F32), pltpu.VMEM((rows, kw), F32)],
        input_output_aliases={10: 0},
        compiler_params=pltpu.CompilerParams(
            dimension_semantics=("arbitrary", "arbitrary"),
            vmem_limit_bytes=_vmem_limit(blk, scratch + (8 << 20))),
        name="gla",
    )(proj, proj, proj, proj, gates, w_up_pad, b_gk.reshape(1, -1), norm_g.reshape(1, -1),
      jnp.asarray(tri, BF16), jnp.asarray(masks), mixed)


def _prep_layer(b_f, w_gk_up):
    gate_bias = jnp.concatenate([b_f, jnp.zeros((V7X_LANES - FOX_HEADS,), F32)]).reshape(1, V7X_LANES)
    w_up_pad = jnp.zeros((V7X_LANES, GLA_KEY_WIDTH), F32).at[FOX_HEADS:FOX_HEADS + GLA_GATE_RANK].set(w_gk_up)
    return gate_bias, w_up_pad


def kernel(x, ln_in_g, ln_in_b, w_in, b_f, w_gk_up, b_gk, fox_norm_g, gla_norm_g, w_out, ln1_g, ln1_b,
           w_gate, w_up, w_down, ln2_g, ln2_b):
    bsz, seq, d_model = x.shape
    depth = w_in.shape[0]
    alpha = (2 * depth) ** 0.25
    rows = bsz * seq

    w_in_t = jnp.swapaxes(w_in, 1, 2)
    n_fox_tiles = IN_F_LO // MM_TN
    n_proj_tiles = n_fox_tiles + (IN_LOW_LO - IN_G_LO) // MM_TN

    def proj_row_start(j):
        return pl.multiple_of(j * MM_TN + jnp.where(j >= n_fox_tiles, IN_G_LO - IN_F_LO, 0), F32_SUBLANES)

    x32, x16 = _layer_norm(x.reshape(rows, d_model), None, ln_in_g, ln_in_b)
    for l in range(depth):
        gate_bias, w_up_pad = _prep_layer(b_f[l], w_gk_up[l])

        proj = _matmul_wt(x16, w_in_t, l, n_proj_tiles, proj_row_start, BF16, scaled_cols=FOX_WIDTH,
                          scale=FOX_HEAD_DIM ** -0.5 * LOG2_E).reshape(bsz, seq, -1)
        gates, aug_q, aug_k = _gates(x16, w_in_t, l, gate_bias, bsz, seq)
        gates = gates.reshape(bsz, seq, V7X_LANES)

        mixed = _fox(proj, aug_q, aug_k, fox_norm_g[l].reshape(FOX_HEADS, 1, FOX_HEAD_DIM))
        mixed = _gla(proj, IN_F_LO, gates, w_up_pad, b_gk[l], gla_norm_g[l], mixed)

        mix = _matmul(mixed.reshape(rows, MIX_WIDTH), w_out, l, d_model, F32)
        x32, x16 = _layer_norm(x32, mix, ln1_g[l], ln1_b[l], alpha=alpha)

        hidden = _gate_up(x16, w_gate, w_up, l)
        ffn = _down(hidden, w_down, l)
        last = l == depth - 1
        outs = _layer_norm(x32, ffn, ln2_g[l], ln2_b[l], alpha=alpha, want_bf16=not last)
        if last:
            x32, = outs
        else:
            x32, x16 = outs
    return x32.reshape(bsz, seq, d_model)
```

```python
import functools

import numpy as np
import jax
import jax.numpy as jnp
from jax import lax
from jax.experimental import pallas as pl
from jax.experimental.pallas import tpu as pltpu

F32 = jnp.float32
BF16 = jnp.bfloat16

V7X_VMEM_BYTES = 64 * 2**20
V7X_LANES = 128
V7X_MXU_DIM = 256
F32_SUBLANES = 8
BF16_SUBLANES = 16

FOX_HEADS = 16
FOX_HEAD_DIM = 128
FOX_WIDTH = FOX_HEADS * FOX_HEAD_DIM
GLA_HEADS = 4
GLA_KEY_DIM = 256
GLA_VAL_DIM = 512
GLA_KEY_WIDTH = GLA_HEADS * GLA_KEY_DIM
GLA_WIDTH = GLA_HEADS * GLA_VAL_DIM
GLA_GATE_RANK = 16
GLA_GATE_NORMALIZER = 16.0
MIX_WIDTH = FOX_WIDTH + GLA_WIDTH
EPS = 1e-5
IN_F_LO = 3 * FOX_WIDTH
IN_G_LO = IN_F_LO + FOX_HEADS
IN_LOW_LO = IN_G_LO + 2 * GLA_KEY_WIDTH + 2 * GLA_WIDTH

LN_ROWS = 256
MM_TM = 1024
MM_TN = 1024
OUT_TN = 512
FFN_TN = 512
FFN_PAD = 512
DOWN_TM = 512
DOWN_TN = 512
GATE_ROWS = 512
FOX_TQ = 512
FOX_TK = 512
FOX_GROUP = 4
FOX_UNROLL = 4
GLA_ROWS = 512
GLA_CHUNK = 64
NEG_BIG = -1e30
LOG2_E = 1.4426950408889634

_NT = (((1,), (1,)), ((), ()))
_TN = (((0,), (0,)), ((), ()))


def _vmem_limit(block_bytes, extra_bytes):
    need = 2 * block_bytes + extra_bytes + (4 << 20)
    return int(min(need, V7X_VMEM_BYTES - (6 << 20)))


def _nbytes(shape, dtype):
    return int(np.prod(shape)) * jnp.dtype(dtype).itemsize


def _log_sigmoid(x):
    return jnp.minimum(x, 0.0) - jnp.log1p(jnp.exp(-jnp.abs(x)))


def _silu(x):
    return x * (1.0 / (1.0 + jnp.exp(-x)))


def _ln_math(t, g, b):
    mu = jnp.mean(t, axis=-1, keepdims=True)
    tc = t - mu
    var = jnp.mean(tc * tc, axis=-1, keepdims=True)
    return tc * lax.rsqrt(var + EPS) * g + b


def _ln_kernel(x_ref, g_ref, b_ref, *outs, want_bf16):
    y = _ln_math(x_ref[...], g_ref[...], b_ref[...])
    outs[0][...] = y
    if want_bf16:
        outs[1][...] = y.astype(BF16)


def _layer_norm(x, g, b, *, want_bf16=True):
    rows, d = x.shape
    assert rows % LN_ROWS == 0
    row_spec = pl.BlockSpec((LN_ROWS, d), lambda i: (i, 0))
    vec_spec = pl.BlockSpec((1, d), lambda i: (0, 0))
    out_shape = [jax.ShapeDtypeStruct((rows, d), F32)] + ([jax.ShapeDtypeStruct((rows, d), BF16)] * want_bf16)
    blk = _nbytes((LN_ROWS, d), F32) * (1 + len(out_shape))
    return pl.pallas_call(
        functools.partial(_ln_kernel, want_bf16=want_bf16),
        grid=(rows // LN_ROWS,),
        in_specs=[row_spec, vec_spec, vec_spec], out_specs=[row_spec] * len(out_shape), out_shape=out_shape,
        compiler_params=pltpu.CompilerParams(
            dimension_semantics=("arbitrary",),
            vmem_limit_bytes=_vmem_limit(blk, 4 * _nbytes((LN_ROWS, d), F32))),
        name="layer_norm",
    )(x, g.reshape(1, d), b.reshape(1, d))


def _ws_kernel(x_ref, *rest, n_w, transposed, epilogue, k_valid, has_res):
    w_refs = rest[:n_w]
    res_ref = rest[n_w] if has_res else None
    o_ref, wb_ref = rest[n_w + has_res:]
    j, i = pl.program_id(0), pl.program_id(1)
    tn = wb_ref.shape[2] // n_w
    ck = w_refs[0].shape[-1] if transposed else w_refs[0].shape[0]

    def convert_chunk(slot):
        rows = pl.ds(pl.multiple_of(i * ck, ck), ck)
        for t, w_ref in enumerate(w_refs):
            chunk = w_ref[0].T if transposed else w_ref[...]
            if k_valid < wb_ref.shape[1]:
                row = i * ck + lax.broadcasted_iota(jnp.int32, chunk.shape, 0)
                chunk = jnp.where(row < k_valid, chunk, 0.0)
            wb_ref[slot, rows, t * tn:(t + 1) * tn] = chunk.astype(BF16)

    @pl.when(j == 0)
    def _():
        convert_chunk(0)

    for parity in (0, 1):
        @pl.when(jnp.logical_and(j > 0, (j & 1) == parity))
        def _():
            convert_chunk(parity)
            acc = jnp.dot(x_ref[...], wb_ref[1 - parity], preferred_element_type=F32)
            o_ref[...] = epilogue(acc, j - 1, res_ref[...] if has_res else None).astype(o_ref.dtype)


def _ws_matmul(x, weights, w_specs, n_tiles, tn_w, tn_out, out_dtype, epilogue, *, transposed, name, tm=MM_TM,
               k_valid=None, res=None):
    m, kdim = x.shape
    n_m = m // tm
    ck = kdim // n_m
    assert m % tm == 0 and kdim % n_m == 0 and ck % (V7X_LANES if transposed else BF16_SUBLANES) == 0
    k_valid = kdim if k_valid is None else k_valid
    n_w = len(weights)
    out_spec = pl.BlockSpec((tm, tn_out), lambda j, i: (jnp.where(j == 0, 0, i), jnp.maximum(j - 1, 0)))
    blk = (_nbytes((tm, kdim), x.dtype) + n_w * _nbytes((ck, tn_w), F32) + _nbytes((tm, tn_out), out_dtype)
           + (_nbytes((tm, tn_out), res.dtype) if res is not None else 0))
    scratch = _nbytes((2, kdim, n_w * tn_w), BF16)
    extra = scratch + 2 * _nbytes((tm, n_w * tn_w), F32) + 3 * n_w * _nbytes((ck, tn_w), F32)
    return pl.pallas_call(
        functools.partial(_ws_kernel, n_w=n_w, transposed=transposed, epilogue=epilogue, k_valid=k_valid,
                          has_res=res is not None),
        grid=(n_tiles + 1, n_m),
        in_specs=([pl.BlockSpec((tm, kdim), lambda j, i: (jnp.where(j == 0, 0, i), 0))] + w_specs(ck)
                  + ([out_spec] if res is not None else [])),
        out_specs=out_spec,
        out_shape=jax.ShapeDtypeStruct((m, n_tiles * tn_out), out_dtype),
        scratch_shapes=[pltpu.VMEM((2, kdim, n_w * tn_w), BF16)],
        compiler_params=pltpu.CompilerParams(
            dimension_semantics=("arbitrary", "arbitrary"),
            vmem_limit_bytes=_vmem_limit(blk, extra)),
        name=name,
    )(x, *weights, *([res] if res is not None else []))


def _residual_epilogue(alpha):
    return lambda acc, tile, res: alpha * res + acc


def _matmul(x, w, layer, n_cols, res, alpha, *, tn=OUT_TN):
    assert n_cols % tn == 0 and w.shape[1] == x.shape[1]
    n_tiles = n_cols // tn

    def w_specs(ck):
        return [pl.BlockSpec((None, ck, tn), lambda j, i: (layer, i, jnp.minimum(j, n_tiles - 1)))]

    return _ws_matmul(x, [w], w_specs, n_tiles, tn, tn, F32, _residual_epilogue(alpha),
                      transposed=False, name="matmul", res=res)


def _matmul_wt(x, w_t, layer, n_tiles, row_start, out_dtype, *, tn=MM_TN, scaled_cols=0, scale=1.0):
    assert w_t.shape[2] == x.shape[1] and scaled_cols % tn == 0
    scaled_tiles = scaled_cols // tn

    def w_specs(ck):
        return [pl.BlockSpec((pl.Element(1), pl.Element(tn), pl.Element(ck)),
                             lambda j, i: (layer, row_start(jnp.minimum(j, n_tiles - 1)), i * ck))]

    def epilogue(acc, tile, res):
        return acc * jnp.where(tile < scaled_tiles, scale, 1.0) if scaled_tiles else acc

    return _ws_matmul(x, [w_t], w_specs, n_tiles, tn, tn, out_dtype, epilogue, transposed=True, name="matmul_wt")


def _gate_up(x, w_gate, w_up, layer, *, tn=FFN_TN):
    d_ff = w_gate.shape[2]
    n_out = pl.cdiv(d_ff, FFN_PAD) * FFN_PAD
    assert n_out % tn == 0 and w_gate.shape[1] == x.shape[1]
    last_w_blk = pl.cdiv(d_ff, tn) - 1

    def w_specs(ck):
        return [pl.BlockSpec((None, ck, tn), lambda j, i: (layer, i, jnp.minimum(j, last_w_blk)))] * 2

    def epilogue(z, tile, res):
        h = _silu(z[:, :tn]) * z[:, tn:]
        col = tile * tn + lax.broadcasted_iota(jnp.int32, h.shape, 1)
        return jnp.where(col < d_ff, h, 0.0)

    return _ws_matmul(x, [w_gate, w_up], w_specs, n_out // tn, tn, tn, BF16, epilogue,
                      transposed=False, name="gate_up")


def _down(h, w_down, layer, res, alpha, *, tm=DOWN_TM, tn=DOWN_TN):
    d_ff, n = w_down.shape[1:]
    assert n % tn == 0 and h.shape[1] >= d_ff
    n_tiles = n // tn

    def w_specs(ck):
        assert h.shape[1] - d_ff < ck
        return [pl.BlockSpec((None, ck, tn), lambda j, i: (layer, i, jnp.minimum(j, n_tiles - 1)))]

    return _ws_matmul(h, [w_down], w_specs, n_tiles, tn, tn, F32, _residual_epilogue(alpha),
                      transposed=False, name="down_proj", tm=tm, k_valid=d_ff, res=res)


def _split3(x):
    a = x.astype(BF16).astype(F32)
    r = x - a
    b = r.astype(BF16).astype(F32)
    return a, b, (r - b).astype(BF16).astype(F32)


def _dot_split2(a, b):
    a_hi = a.astype(BF16)
    b_hi = b.astype(BF16)
    a_lo = (a - a_hi.astype(F32)).astype(BF16)
    b_lo = (b - b_hi.astype(F32)).astype(BF16)
    return (jnp.dot(a_hi, b_hi, preferred_element_type=F32) + jnp.dot(a_hi, b_lo, preferred_element_type=F32)
            + jnp.dot(a_lo, b_hi, preferred_element_type=F32))


def _gates_kernel(x_ref, wf_ref, wl_ref, bias_ref, tri_ref, o_ref, aq_ref, ak_ref, carry_ref):
    @pl.when(pl.program_id(1) == 0)
    def _():
        carry_ref[...] = jnp.zeros_like(carry_ref)

    wf, wl = wf_ref[0], wl_ref[0]
    pad = jnp.zeros((V7X_LANES - wf.shape[0] - wl.shape[0], wf.shape[1]), F32)
    w = jnp.concatenate([wf, wl, pad], axis=0).astype(BF16)
    z = lax.dot_general(x_ref[...], w, _NT, preferred_element_type=F32)
    log_f = _log_sigmoid(z + bias_ref[...])
    parts = jnp.concatenate([p.astype(BF16) for p in _split3(log_f)], axis=1)
    c3 = jnp.dot(tri_ref[...], parts, preferred_element_type=F32)
    nl = log_f.shape[1]
    c = c3[:, :nl] + c3[:, nl:2 * nl] + c3[:, 2 * nl:] + carry_ref[0:1, :]
    rows = c.shape[0]
    carry_ref[...] = jnp.broadcast_to(c[rows - 1:rows, :], carry_ref.shape)
    lane = lax.broadcasted_iota(jnp.int32, c.shape, 1)
    o_ref[...] = jnp.where(lane < FOX_HEADS, c, z)

    c2 = c * LOG2_E
    for h in range(FOX_HEADS):
        c1, c2_, c3 = _split3(jnp.broadcast_to(c2[:, h:h + 1], c.shape))
        aq = jnp.where(lane == 0, c1, jnp.where(lane == 1, c2_, jnp.where(lane == 2, c3,
                       jnp.where(lane < 6, 1.0, 0.0))))
        ak = jnp.where(lane < 3, 1.0, jnp.where(lane == 3, -c1, jnp.where(lane == 4, -c2_,
                       jnp.where(lane == 5, -c3, 0.0))))
        aq_ref[0, h] = aq.astype(BF16)
        ak_ref[0, h] = ak.astype(BF16)


def _gates(x, w_in_t, layer, bias, bsz, seq):
    m, kdim = x.shape
    steps = seq // GATE_ROWS
    tri = jnp.asarray(np.tril(np.ones((GATE_ROWS, GATE_ROWS), np.float32)), BF16)
    aug_shape = jax.ShapeDtypeStruct((bsz, FOX_HEADS, seq, V7X_LANES), BF16)
    aug_spec = pl.BlockSpec((1, FOX_HEADS, GATE_ROWS, V7X_LANES), lambda b, i: (b, 0, i, 0))
    assert IN_F_LO % F32_SUBLANES == 0 and IN_LOW_LO % F32_SUBLANES == 0

    def rows_spec(n_rows, row0):
        return pl.BlockSpec((pl.Element(1), pl.Element(n_rows), pl.Element(kdim)), lambda b, i: (layer, row0, 0))

    blk = (_nbytes((GATE_ROWS, kdim), x.dtype) + _nbytes((V7X_LANES, kdim), F32)
           + _nbytes((GATE_ROWS, GATE_ROWS), F32) + _nbytes((GATE_ROWS, V7X_LANES), F32)
           + 2 * _nbytes((FOX_HEADS, GATE_ROWS, V7X_LANES), BF16))
    return pl.pallas_call(
        _gates_kernel,
        grid=(bsz, steps),
        in_specs=[pl.BlockSpec((GATE_ROWS, kdim), lambda b, i: (b * steps + i, 0)),
                  rows_spec(FOX_HEADS, IN_F_LO), rows_spec(GLA_GATE_RANK, IN_LOW_LO),
                  pl.BlockSpec((1, V7X_LANES), lambda b, i: (0, 0)),
                  pl.BlockSpec((GATE_ROWS, GATE_ROWS), lambda b, i: (0, 0))],
        out_specs=[pl.BlockSpec((GATE_ROWS, V7X_LANES), lambda b, i: (b * steps + i, 0)), aug_spec, aug_spec],
        out_shape=[jax.ShapeDtypeStruct((m, V7X_LANES), F32), aug_shape, aug_shape],
        scratch_shapes=[pltpu.VMEM((8, V7X_LANES), F32)],
        compiler_params=pltpu.CompilerParams(
            dimension_semantics=("arbitrary", "arbitrary"),
            vmem_limit_bytes=_vmem_limit(blk, 16 * _nbytes((GATE_ROWS, V7X_LANES), F32))),
        name="gates",
    )(x, w_in_t, w_in_t, bias, tri)


def _fox_kernel(q_ref, k_ref, v_ref, aq_ref, ak_ref, g_ref, o_ref, kp_ref, vp_ref, m_ref, acc_ref, *, tq, tk):
    qi = pl.program_id(2)
    n_heads = aq_ref.shape[1]
    d = q_ref.shape[2] // n_heads
    dp = acc_ref.shape[2]

    @pl.when(qi == 0)
    def _():
        lane = lax.broadcasted_iota(jnp.int32, (vp_ref.shape[1], V7X_LANES), 1)
        for g in range(n_heads):
            kp_ref[g, :, :d] = k_ref[0, :, g * d:(g + 1) * d]
            kp_ref[g, :, d:] = ak_ref[0, g]
            vp_ref[g, :, :d] = v_ref[0, :, g * d:(g + 1) * d]
            vp_ref[g, :, d:] = jnp.where(lane == 0, 1.0, 0.0).astype(BF16)

    qp = [jnp.concatenate([q_ref[0, :, g * d:(g + 1) * d], aq_ref[0, g]], axis=1) for g in range(n_heads)]
    m_ref[...] = jnp.full_like(m_ref, NEG_BIG)
    acc_ref[...] = jnp.zeros_like(acc_ref)

    def step(j, masked):
        start = pl.multiple_of(j * tk, tk)
        for g in range(n_heads):
            s = lax.dot_general(qp[g], kp_ref[g, pl.ds(start, tk), :], _NT, preferred_element_type=F32)
            if masked:
                row = lax.broadcasted_iota(jnp.int32, (tq, tk), 0)
                col = lax.broadcasted_iota(jnp.int32, (tq, tk), 1)
                s = jnp.where(row >= col, s, NEG_BIG)
            m_prev = m_ref[g]
            m_new = jnp.maximum(m_prev, jnp.max(s, axis=-1, keepdims=True))
            p = jnp.exp2(s - jnp.concatenate([m_new] * (tk // V7X_LANES), axis=1)).astype(BF16)
            alpha = jnp.exp2(m_prev - m_new)
            acc_ref[g] = (jnp.concatenate([alpha] * (dp // V7X_LANES), axis=1) * acc_ref[g]
                          + jnp.dot(p, vp_ref[g, pl.ds(start, tk), :], preferred_element_type=F32))
            m_ref[g] = m_new

    unroll = FOX_UNROLL
    shift = unroll.bit_length() - 1
    assert unroll == 1 << shift

    def body(i, carry):
        for u in range(unroll):
            step(unroll * i + u, False)
        return carry

    lax.fori_loop(0, lax.shift_right_logical(qi, shift), body, 0)

    for rem in range(unroll):
        @pl.when((qi & (unroll - 1)) == rem)
        def _():
            for u in range(rem):
                step(qi - rem + u, False)
            step(qi, True)

    for g in range(n_heads):
        acc = acc_ref[g]
        o = acc[:, :d] / acc[:, d:d + 1]
        o = o * lax.rsqrt(jnp.mean(o * o, axis=-1, keepdims=True) + EPS) * g_ref[g]
        o_ref[0, :, g * d:(g + 1) * d] = o.astype(o_ref.dtype)


def _fox(proj, aq, ak, norm_g, *, tq=FOX_TQ, tk=FOX_TK, group=FOX_GROUP):
    bsz, seq, _ = proj.shape
    d = FOX_HEAD_DIM
    dp = d + V7X_LANES
    gd = group * d
    n_groups = FOX_HEADS // group
    assert tq == tk and seq % tq == 0 and FOX_HEADS % group == 0
    blk = (2 * _nbytes((tq, gd), BF16) + 2 * _nbytes((seq, gd), BF16) + _nbytes((group, tq, V7X_LANES), BF16)
           + _nbytes((group, seq, V7X_LANES), BF16) + _nbytes((group, 8, d), F32))
    scratch = (2 * _nbytes((group, seq, dp), BF16) + _nbytes((group, tq, V7X_LANES), F32)
               + _nbytes((group, tq, dp), F32))
    return pl.pallas_call(
        functools.partial(_fox_kernel, tq=tq, tk=tk),
        grid=(bsz, n_groups, seq // tq),
        in_specs=[pl.BlockSpec((1, tq, gd), lambda b, h, i: (b, i, h)),
                  pl.BlockSpec((1, seq, gd), lambda b, h, i: (b, 0, n_groups + h)),
                  pl.BlockSpec((1, seq, gd), lambda b, h, i: (b, 0, 2 * n_groups + h)),
                  pl.BlockSpec((1, group, tq, V7X_LANES), lambda b, h, i: (b, h, i, 0)),
                  pl.BlockSpec((1, group, seq, V7X_LANES), lambda b, h, i: (b, h, 0, 0)),
                  pl.BlockSpec((group, 1, d), lambda b, h, i: (h, 0, 0))],
        out_specs=pl.BlockSpec((1, tq, gd), lambda b, h, i: (b, i, h)),
        out_shape=jax.ShapeDtypeStruct((bsz, seq, MIX_WIDTH), BF16),
        scratch_shapes=[pltpu.VMEM((group, seq, dp), BF16), pltpu.VMEM((group, seq, dp), BF16),
                        pltpu.VMEM((group, tq, V7X_LANES), F32), pltpu.VMEM((group, tq, dp), F32)],
        compiler_params=pltpu.CompilerParams(
            dimension_semantics=("arbitrary", "arbitrary", "arbitrary"),
            vmem_limit_bytes=_vmem_limit(blk, scratch + 8 * group * _nbytes((tq, tk), F32))),
        name="fox_attention",
    )(proj, proj, proj, aq, ak, norm_g)


def _gla_tables(chunk):
    idx = np.arange(chunk)
    masks = []
    half = chunk // 2
    while half >= 1:
        blk = idx // (2 * half)
        second = (idx // half) % 2 == 1
        masks.append(((blk[:, None] == blk[None, :]) & second[:, None] & ~second[None, :]).astype(np.float32))
        half //= 2
    span_chunks = V7X_MXU_DIM // chunk
    tri = np.kron(np.eye(span_chunks, dtype=np.float32), np.tril(np.ones((chunk, chunk), np.float32)))
    return tri, np.stack(masks, axis=0)


def _level_refs(b):
    chunk, width = b.shape
    refs = []
    half = chunk // 2
    while half >= 4:
        pieces = [jnp.broadcast_to(b[s + half - 1:s + half], (2 * half, width)) for s in range(0, chunk, 2 * half)]
        refs.append(pieces[0] if len(pieces) == 1 else jnp.concatenate(pieces, axis=0))
        half //= 2
    prev1 = pltpu.roll(b, 1, 0)
    prev2 = pltpu.roll(b, 2, 0)
    next1 = pltpu.roll(b, chunk - 1, 0)
    pos = lax.broadcasted_iota(jnp.int32, b.shape, 0) & 3
    refs.append(jnp.where(pos == 0, next1, jnp.where(pos == 1, b, jnp.where(pos == 2, prev1, prev2))))
    refs.append(jnp.where((pos & 1) == 0, b, prev1))
    return refs


def _gla_kernel(q_ref, k_ref, v_ref, gg_ref, gates_ref, wup_ref, bgk_ref, gn_ref, tri_ref, mask_ref,
                mixed_hbm_ref, o_ref, state_ref, la_ref, *, rows, chunk, scale):
    del mixed_hbm_ref
    dk, dv = GLA_KEY_DIM, GLA_VAL_DIM

    @pl.when(pl.program_id(1) == 0)
    def _():
        state_ref[...] = jnp.zeros_like(state_ref)

    pre = _dot_split2(gates_ref[0], wup_ref[...]) + bgk_ref[...]
    log_a = _log_sigmoid(pre) * (LOG2_E / GLA_GATE_NORMALIZER)
    span = tri_ref.shape[0]
    width = log_a.shape[1]
    for r in range(0, rows, span):
        parts = jnp.concatenate([p.astype(BF16) for p in _split3(log_a[r:r + span])], axis=1)
        b3 = jnp.dot(tri_ref[...], parts, preferred_element_type=F32)
        la_ref[r:r + span, :] = b3[:, :width] + b3[:, width:2 * width] + b3[:, 2 * width:]

    n_levels = mask_ref.shape[0]
    eye = (lax.broadcasted_iota(jnp.int32, (chunk, chunk), 0)
           == lax.broadcasted_iota(jnp.int32, (chunk, chunk), 1))
    row_dk = lax.broadcasted_iota(jnp.int32, (chunk, dk), 0)
    row_e = lax.broadcasted_iota(jnp.int32, (BF16_SUBLANES, dk), 0)
    ones_cols = jnp.ones((BF16_SUBLANES, V7X_LANES), BF16)

    def head_chunk(r0, h):
        ks = slice(h * dk, (h + 1) * dk)
        vs = slice(h * dv, (h + 1) * dv)
        b = la_ref[pl.ds(r0, chunk), ks]
        q = q_ref[0, pl.ds(r0, chunk), ks].astype(F32) * scale
        k = k_ref[0, pl.ds(r0, chunk), ks].astype(F32)
        v = v_ref[0, pl.ds(r0, chunk), vs]

        state = state_ref[h]
        o = jnp.dot((q * jnp.exp2(b)).astype(BF16), state.astype(BF16), preferred_element_type=F32)

        scores = jnp.zeros((chunk, chunk), F32)
        half = chunk // 2
        for lvl, ref in enumerate(_level_refs(b)):
            u = (jnp.where((row_dk & half) != 0, q, k) * jnp.exp2(-jnp.abs(b - ref))).astype(BF16)
            scores = scores + mask_ref[lvl] * lax.dot_general(u, u, _NT, preferred_element_type=F32)
            half //= 2
        scores = jnp.where(eye, jnp.sum(q * k, axis=-1, keepdims=True), scores)
        o = o + jnp.dot(scores.astype(BF16), v, preferred_element_type=F32)

        b_last = b[chunk - 1:chunk]
        k_dec = (k * jnp.exp2(b_last - b)).astype(BF16)
        e1, e2, e3 = _split3(jnp.broadcast_to(jnp.exp2(b_last), row_e.shape))
        e_rows = jnp.where(row_e == 0, e1, jnp.where(row_e == 1, e2, jnp.where(row_e == 2, e3, 0.0)))
        dec_col = lax.dot_general(e_rows.astype(BF16), ones_cols, _TN, preferred_element_type=F32)
        decay = jnp.concatenate([dec_col] * (dv // V7X_LANES), axis=1)
        state_ref[h] = state * decay + lax.dot_general(k_dec, v, _TN, preferred_element_type=F32)

        o = o * lax.rsqrt(jnp.mean(o * o, axis=-1, keepdims=True) + EPS) * gn_ref[...]
        gate = _silu(gg_ref[0, pl.ds(r0, chunk), vs].astype(F32))
        o_ref[0, pl.ds(r0, chunk), vs] = (o * gate).astype(o_ref.dtype)

    assert n_levels == chunk.bit_length() - 1

    def chunk_body(c, carry):
        r0 = pl.multiple_of(c * chunk, chunk)
        for h in range(GLA_HEADS):
            head_chunk(r0, h)
        return carry

    lax.fori_loop(0, rows // chunk, chunk_body, 0)


def _gla(proj, col0, gates, w_up_pad, b_gk, norm_g, mixed, *, rows=GLA_ROWS, chunk=GLA_CHUNK):
    bsz, seq, _ = proj.shape
    kw, vw = GLA_KEY_WIDTH, GLA_WIDTH
    assert seq % rows == 0 and rows % chunk == 0 and FOX_WIDTH == vw
    assert col0 % kw == 0 and (col0 + 2 * kw) % vw == 0
    q_blk = col0 // kw
    v_blk = (col0 + 2 * kw) // vw
    tri, masks = _gla_tables(chunk)
    blk = (2 * _nbytes((rows, kw), BF16) + 3 * _nbytes((rows, vw), BF16) + _nbytes((rows, V7X_LANES), F32)
           + _nbytes((V7X_LANES, kw), F32) + _nbytes(masks.shape, F32))
    scratch = _nbytes((GLA_HEADS, GLA_KEY_DIM, GLA_VAL_DIM), F32) + _nbytes((rows, kw), F32)
    return pl.pallas_call(
        functools.partial(_gla_kernel, rows=rows, chunk=chunk, scale=GLA_KEY_DIM ** -0.5),
        grid=(bsz, seq // rows),
        in_specs=[pl.BlockSpec((1, rows, kw), lambda b, t: (b, t, q_blk)),
                  pl.BlockSpec((1, rows, kw), lambda b, t: (b, t, q_blk + 1)),
                  pl.BlockSpec((1, rows, vw), lambda b, t: (b, t, v_blk)),
                  pl.BlockSpec((1, rows, vw), lambda b, t: (b, t, v_blk + 1)),
                  pl.BlockSpec((1, rows, V7X_LANES), lambda b, t: (b, t, 0)),
                  pl.BlockSpec((V7X_LANES, kw), lambda b, t: (0, 0)),
                  pl.BlockSpec((1, kw), lambda b, t: (0, 0)),
                  pl.BlockSpec((1, GLA_VAL_DIM), lambda b, t: (0, 0)),
                  pl.BlockSpec(tri.shape, lambda b, t: (0, 0)),
                  pl.BlockSpec(masks.shape, lambda b, t: (0, 0, 0)),
                  pl.BlockSpec(memory_space=pl.ANY)],
        out_specs=pl.BlockSpec((1, rows, vw), lambda b, t: (b, t, 1)),
        out_shape=jax.ShapeDtypeStruct(mixed.shape, mixed.dtype),
        scratch_shapes=[pltpu.VMEM((GLA_HEADS, GLA_KEY_DIM, GLA_VAL_DIM), F32), pltpu.VMEM((rows, kw), F32)],
        input_output_aliases={10: 0},
        compiler_params=pltpu.CompilerParams(
            dimension_semantics=("arbitrary", "arbitrary"),
            vmem_limit_bytes=_vmem_limit(blk, scratch + (8 << 20))),
        name="gla",
    )(proj, proj, proj, proj, gates, w_up_pad, b_gk.reshape(1, -1), norm_g.reshape(1, -1),
      jnp.asarray(tri, BF16), jnp.asarray(masks), mixed)


def _prep_layer(b_f, w_gk_up):
    gate_bias = jnp.concatenate([b_f, jnp.zeros((V7X_LANES - FOX_HEADS,), F32)]).reshape(1, V7X_LANES)
    w_up_pad = jnp.zeros((V7X_LANES, GLA_KEY_WIDTH), F32).at[FOX_HEADS:FOX_HEADS + GLA_GATE_RANK].set(w_gk_up)
    return gate_bias, w_up_pad


def kernel(x, ln_in_g, ln_in_b, w_in, b_f, w_gk_up, b_gk, fox_norm_g, gla_norm_g, w_out, ln1_g, ln1_b,
           w_gate, w_up, w_down, ln2_g, ln2_b):
    bsz, seq, d_model = x.shape
    depth = w_in.shape[0]
    alpha = (2 * depth) ** 0.25
    rows = bsz * seq

    w_in_t = jnp.swapaxes(w_in, 1, 2)
    n_fox_tiles = IN_F_LO // MM_TN
    n_proj_tiles = n_fox_tiles + (IN_LOW_LO - IN_G_LO) // MM_TN

    def proj_row_start(j):
        return pl.multiple_of(j * MM_TN + jnp.where(j >= n_fox_tiles, IN_G_LO - IN_F_LO, 0), F32_SUBLANES)

    x32, x16 = _layer_norm(x.reshape(rows, d_model), ln_in_g, ln_in_b)
    for l in range(depth):
        gate_bias, w_up_pad = _prep_layer(b_f[l], w_gk_up[l])

        proj = _matmul_wt(x16, w_in_t, l, n_proj_tiles, proj_row_start, BF16, scaled_cols=FOX_WIDTH,
                          scale=FOX_HEAD_DIM ** -0.5 * LOG2_E).reshape(bsz, seq, -1)
        gates, aug_q, aug_k = _gates(x16, w_in_t, l, gate_bias, bsz, seq)
        gates = gates.reshape(bsz, seq, V7X_LANES)

        mixed = _fox(proj, aug_q, aug_k, fox_norm_g[l].reshape(FOX_HEADS, 1, FOX_HEAD_DIM))
        mixed = _gla(proj, IN_F_LO, gates, w_up_pad, b_gk[l], gla_norm_g[l], mixed)

        t = _matmul(mixed.reshape(rows, MIX_WIDTH), w_out, l, d_model, x32, alpha)
        x32, x16 = _layer_norm(t, ln1_g[l], ln1_b[l])

        hidden = _gate_up(x16, w_gate, w_up, l)
        t = _down(hidden, w_down, l, x32, alpha)
        if l == depth - 1:
            x32, = _layer_norm(t, ln2_g[l], ln2_b[l], want_bf16=False)
        else:
            x32, x16 = _layer_norm(t, ln2_g[l], ln2_b[l])
    return x32.reshape(bsz, seq, d_model)
```

```python
import functools

import numpy as np
import jax
import jax.numpy as jnp
from jax import lax
from jax.experimental import pallas as pl
from jax.experimental.pallas import tpu as pltpu

F32 = jnp.float32
BF16 = jnp.bfloat16

V7X_VMEM_BYTES = 64 * 2**20
V7X_LANES = 128
V7X_MXU_DIM = 256
F32_SUBLANES = 8
BF16_SUBLANES = 16

FOX_HEADS = 16
FOX_HEAD_DIM = 128
FOX_WIDTH = FOX_HEADS * FOX_HEAD_DIM
GLA_HEADS = 4
GLA_KEY_DIM = 256
GLA_VAL_DIM = 512
GLA_KEY_WIDTH = GLA_HEADS * GLA_KEY_DIM
GLA_WIDTH = GLA_HEADS * GLA_VAL_DIM
GLA_GATE_RANK = 16
GLA_GATE_NORMALIZER = 16.0
MIX_WIDTH = FOX_WIDTH + GLA_WIDTH
EPS = 1e-5
IN_F_LO = 3 * FOX_WIDTH
IN_G_LO = IN_F_LO + FOX_HEADS
IN_LOW_LO = IN_G_LO + 2 * GLA_KEY_WIDTH + 2 * GLA_WIDTH

LN_ROWS = 256
MM_TM = 1024
MM_TN = 1024
OUT_TN = 512
FFN_TN = 512
FFN_PAD = 512
DOWN_TM = 512
DOWN_TN = 512
GATE_ROWS = 512
FOX_TQ = 512
FOX_TK = 512
FOX_GROUP = 4
FOX_UNROLL = 4
GLA_ROWS = 512
GLA_CHUNK = 64
NEG_BIG = -1e30
LOG2_E = 1.4426950408889634

_NT = (((1,), (1,)), ((), ()))
_TN = (((0,), (0,)), ((), ()))


def _vmem_limit(block_bytes, extra_bytes):
    need = 2 * block_bytes + extra_bytes + (4 << 20)
    return int(min(need, V7X_VMEM_BYTES - (6 << 20)))


def _nbytes(shape, dtype):
    return int(np.prod(shape)) * jnp.dtype(dtype).itemsize


def _log_sigmoid(x):
    return jnp.minimum(x, 0.0) - jnp.log1p(jnp.exp(-jnp.abs(x)))


def _silu(x):
    return x * (1.0 / (1.0 + jnp.exp(-x)))


def _ln_stats(t):
    mu = jnp.mean(t, axis=-1, keepdims=True)
    tc = t - mu
    return mu, lax.rsqrt(jnp.mean(tc * tc, axis=-1, keepdims=True) + EPS)


def _ln_apply(t, mu, rstd, g, b):
    return (t - mu) * rstd * g + b


def _ln_kernel(x_ref, g_ref, b_ref, *outs, final):
    t = x_ref[...]
    mu, rstd = _ln_stats(t)
    y = _ln_apply(t, mu, rstd, g_ref[...], b_ref[...])
    if final:
        outs[0][...] = y
    else:
        outs[0][...] = y.astype(BF16)
        outs[1][...] = jnp.broadcast_to(mu, outs[1].shape)
        outs[2][...] = jnp.broadcast_to(rstd, outs[2].shape)


def _layer_norm(t, g, b, *, final=False):
    rows, d = t.shape
    assert rows % LN_ROWS == 0
    row_spec = pl.BlockSpec((LN_ROWS, d), lambda i: (i, 0))
    vec_spec = pl.BlockSpec((1, d), lambda i: (0, 0))
    stat_spec = pl.BlockSpec((LN_ROWS, V7X_LANES), lambda i: (i, 0))
    if final:
        out_shape, out_specs = [jax.ShapeDtypeStruct((rows, d), F32)], [row_spec]
    else:
        stat_shape = jax.ShapeDtypeStruct((rows, V7X_LANES), F32)
        out_shape, out_specs = [jax.ShapeDtypeStruct((rows, d), BF16), stat_shape, stat_shape], [row_spec, stat_spec, stat_spec]
    blk = 2 * _nbytes((LN_ROWS, d), F32) + 2 * _nbytes((LN_ROWS, V7X_LANES), F32)
    return pl.pallas_call(
        functools.partial(_ln_kernel, final=final),
        grid=(rows // LN_ROWS,),
        in_specs=[row_spec, vec_spec, vec_spec], out_specs=out_specs, out_shape=out_shape,
        compiler_params=pltpu.CompilerParams(
            dimension_semantics=("arbitrary",),
            vmem_limit_bytes=_vmem_limit(blk, 4 * _nbytes((LN_ROWS, d), F32))),
        name="layer_norm",
    )(t, g.reshape(1, d), b.reshape(1, d))


def _ws_kernel(x_ref, *rest, n_w, transposed, epilogue, k_valid, n_res):
    w_refs = rest[:n_w]
    res_refs = rest[n_w:n_w + n_res]
    o_ref, wb_ref = rest[n_w + n_res:]
    j, i = pl.program_id(0), pl.program_id(1)
    tn = wb_ref.shape[2] // n_w
    ck = w_refs[0].shape[-1] if transposed else w_refs[0].shape[0]

    def convert_chunk(slot):
        rows = pl.ds(pl.multiple_of(i * ck, ck), ck)
        for t, w_ref in enumerate(w_refs):
            chunk = w_ref[0].T if transposed else w_ref[...]
            if k_valid < wb_ref.shape[1]:
                row = i * ck + lax.broadcasted_iota(jnp.int32, chunk.shape, 0)
                chunk = jnp.where(row < k_valid, chunk, 0.0)
            wb_ref[slot, rows, t * tn:(t + 1) * tn] = chunk.astype(BF16)

    @pl.when(j == 0)
    def _():
        convert_chunk(0)

    for parity in (0, 1):
        @pl.when(jnp.logical_and(j > 0, (j & 1) == parity))
        def _():
            convert_chunk(parity)
            acc = jnp.dot(x_ref[...], wb_ref[1 - parity], preferred_element_type=F32)
            o_ref[...] = epilogue(acc, j - 1, [r[...] for r in res_refs]).astype(o_ref.dtype)


def _ws_matmul(x, weights, w_specs, n_tiles, tn_w, tn_out, out_dtype, epilogue, *, transposed, name, tm=MM_TM,
               k_valid=None, res=None):
    m, kdim = x.shape
    n_m = m // tm
    ck = kdim // n_m
    assert m % tm == 0 and kdim % n_m == 0 and ck % (V7X_LANES if transposed else BF16_SUBLANES) == 0
    k_valid = kdim if k_valid is None else k_valid
    n_w = len(weights)

    def row_tile(j, i):
        return jnp.where(j == 0, 0, i)

    def col_tile(j, i):
        return jnp.maximum(j - 1, 0)

    out_spec = pl.BlockSpec((tm, tn_out), lambda j, i: (row_tile(j, i), col_tile(j, i)))
    res_arrays, res_specs, res_bytes = [], [], 0
    if res is not None:
        t, mu, rstd, g, b = res
        stat_spec = pl.BlockSpec((tm, V7X_LANES), lambda j, i: (row_tile(j, i), 0))
        vec_spec = pl.BlockSpec((1, tn_out), lambda j, i: (0, col_tile(j, i)))
        res_arrays = [t, mu, rstd, g.reshape(1, -1), b.reshape(1, -1)]
        res_specs = [out_spec, stat_spec, stat_spec, vec_spec, vec_spec]
        res_bytes = _nbytes((tm, tn_out), F32) + 2 * _nbytes((tm, V7X_LANES), F32)
    blk = (_nbytes((tm, kdim), x.dtype) + n_w * _nbytes((ck, tn_w), F32) + _nbytes((tm, tn_out), out_dtype)
           + res_bytes)
    scratch = _nbytes((2, kdim, n_w * tn_w), BF16)
    extra = scratch + 2 * _nbytes((tm, n_w * tn_w), F32) + 3 * n_w * _nbytes((ck, tn_w), F32)
    return pl.pallas_call(
        functools.partial(_ws_kernel, n_w=n_w, transposed=transposed, epilogue=epilogue, k_valid=k_valid,
                          n_res=len(res_arrays)),
        grid=(n_tiles + 1, n_m),
        in_specs=[pl.BlockSpec((tm, kdim), lambda j, i: (row_tile(j, i), 0))] + w_specs(ck) + res_specs,
        out_specs=out_spec,
        out_shape=jax.ShapeDtypeStruct((m, n_tiles * tn_out), out_dtype),
        scratch_shapes=[pltpu.VMEM((2, kdim, n_w * tn_w), BF16)],
        compiler_params=pltpu.CompilerParams(
            dimension_semantics=("arbitrary", "arbitrary"),
            vmem_limit_bytes=_vmem_limit(blk, extra)),
        name=name,
    )(x, *weights, *res_arrays)


def _residual_epilogue(alpha):
    def epilogue(acc, tile, res):
        t, mu, rstd, g, b = res
        reps = t.shape[1] // V7X_LANES
        x = _ln_apply(t, jnp.concatenate([mu] * reps, axis=1), jnp.concatenate([rstd] * reps, axis=1), g, b)
        return alpha * x + acc
    return epilogue


def _matmul(x, w, layer, n_cols, res, alpha, *, tn=OUT_TN):
    assert n_cols % tn == 0 and w.shape[1] == x.shape[1]
    n_tiles = n_cols // tn

    def w_specs(ck):
        return [pl.BlockSpec((None, ck, tn), lambda j, i: (layer, i, jnp.minimum(j, n_tiles - 1)))]

    return _ws_matmul(x, [w], w_specs, n_tiles, tn, tn, F32, _residual_epilogue(alpha),
                      transposed=False, name="matmul", res=res)


def _matmul_wt(x, w_t, layer, n_tiles, row_start, out_dtype, *, tn=MM_TN, scaled_cols=0, scale=1.0):
    assert w_t.shape[2] == x.shape[1] and scaled_cols % tn == 0
    scaled_tiles = scaled_cols // tn

    def w_specs(ck):
        return [pl.BlockSpec((pl.Element(1), pl.Element(tn), pl.Element(ck)),
                             lambda j, i: (layer, row_start(jnp.minimum(j, n_tiles - 1)), i * ck))]

    def epilogue(acc, tile, res):
        return acc * jnp.where(tile < scaled_tiles, scale, 1.0) if scaled_tiles else acc

    return _ws_matmul(x, [w_t], w_specs, n_tiles, tn, tn, out_dtype, epilogue, transposed=True, name="matmul_wt")


def _gate_up(x, w_gate, w_up, layer, *, tn=FFN_TN):
    d_ff = w_gate.shape[2]
    n_out = pl.cdiv(d_ff, FFN_PAD) * FFN_PAD
    assert n_out % tn == 0 and w_gate.shape[1] == x.shape[1]
    last_w_blk = pl.cdiv(d_ff, tn) - 1

    def w_specs(ck):
        return [pl.BlockSpec((None, ck, tn), lambda j, i: (layer, i, jnp.minimum(j, last_w_blk)))] * 2

    def epilogue(z, tile, res):
        h = _silu(z[:, :tn]) * z[:, tn:]
        col = tile * tn + lax.broadcasted_iota(jnp.int32, h.shape, 1)
        return jnp.where(col < d_ff, h, 0.0)

    return _ws_matmul(x, [w_gate, w_up], w_specs, n_out // tn, tn, tn, BF16, epilogue,
                      transposed=False, name="gate_up")


def _down(h, w_down, layer, res, alpha, *, tm=DOWN_TM, tn=DOWN_TN):
    d_ff, n = w_down.shape[1:]
    assert n % tn == 0 and h.shape[1] >= d_ff
    n_tiles = n // tn

    def w_specs(ck):
        assert h.shape[1] - d_ff < ck
        return [pl.BlockSpec((None, ck, tn), lambda j, i: (layer, i, jnp.minimum(j, n_tiles - 1)))]

    return _ws_matmul(h, [w_down], w_specs, n_tiles, tn, tn, F32, _residual_epilogue(alpha),
                      transposed=False, name="down_proj", tm=tm, k_valid=d_ff, res=res)


def _split3(x):
    a = x.astype(BF16).astype(F32)
    r = x - a
    b = r.astype(BF16).astype(F32)
    return a, b, (r - b).astype(BF16).astype(F32)


def _dot_split2(a, b):
    a_hi = a.astype(BF16)
    b_hi = b.astype(BF16)
    a_lo = (a - a_hi.astype(F32)).astype(BF16)
    b_lo = (b - b_hi.astype(F32)).astype(BF16)
    return (jnp.dot(a_hi, b_hi, preferred_element_type=F32) + jnp.dot(a_hi, b_lo, preferred_element_type=F32)
            + jnp.dot(a_lo, b_hi, preferred_element_type=F32))


def _gates_kernel(x_ref, wf_ref, wl_ref, bias_ref, tri_ref, o_ref, aq_ref, ak_ref, carry_ref):
    @pl.when(pl.program_id(1) == 0)
    def _():
        carry_ref[...] = jnp.zeros_like(carry_ref)

    wf, wl = wf_ref[0], wl_ref[0]
    pad = jnp.zeros((V7X_LANES - wf.shape[0] - wl.shape[0], wf.shape[1]), F32)
    w = jnp.concatenate([wf, wl, pad], axis=0).astype(BF16)
    z = lax.dot_general(x_ref[...], w, _NT, preferred_element_type=F32)
    log_f = _log_sigmoid(z + bias_ref[...])
    parts = jnp.concatenate([p.astype(BF16) for p in _split3(log_f)], axis=1)
    c3 = jnp.dot(tri_ref[...], parts, preferred_element_type=F32)
    nl = log_f.shape[1]
    c = c3[:, :nl] + c3[:, nl:2 * nl] + c3[:, 2 * nl:] + carry_ref[0:1, :]
    rows = c.shape[0]
    carry_ref[...] = jnp.broadcast_to(c[rows - 1:rows, :], carry_ref.shape)
    lane = lax.broadcasted_iota(jnp.int32, c.shape, 1)
    o_ref[...] = jnp.where(lane < FOX_HEADS, c, z)

    c2 = c * LOG2_E
    for h in range(FOX_HEADS):
        c1, c2_, c3 = _split3(jnp.broadcast_to(c2[:, h:h + 1], c.shape))
        aq = jnp.where(lane == 0, c1, jnp.where(lane == 1, c2_, jnp.where(lane == 2, c3,
                       jnp.where(lane < 6, 1.0, 0.0))))
        ak = jnp.where(lane < 3, 1.0, jnp.where(lane == 3, -c1, jnp.where(lane == 4, -c2_,
                       jnp.where(lane == 5, -c3, 0.0))))
        aq_ref[0, h] = aq.astype(BF16)
        ak_ref[0, h] = ak.astype(BF16)


def _gates(x, w_in_t, layer, bias, bsz, seq):
    m, kdim = x.shape
    steps = seq // GATE_ROWS
    tri = jnp.asarray(np.tril(np.ones((GATE_ROWS, GATE_ROWS), np.float32)), BF16)
    aug_shape = jax.ShapeDtypeStruct((bsz, FOX_HEADS, seq, V7X_LANES), BF16)
    aug_spec = pl.BlockSpec((1, FOX_HEADS, GATE_ROWS, V7X_LANES), lambda b, i: (b, 0, i, 0))
    assert IN_F_LO % F32_SUBLANES == 0 and IN_LOW_LO % F32_SUBLANES == 0

    def rows_spec(n_rows, row0):
        return pl.BlockSpec((pl.Element(1), pl.Element(n_rows), pl.Element(kdim)), lambda b, i: (layer, row0, 0))

    blk = (_nbytes((GATE_ROWS, kdim), x.dtype) + _nbytes((V7X_LANES, kdim), F32)
           + _nbytes((GATE_ROWS, GATE_ROWS), F32) + _nbytes((GATE_ROWS, V7X_LANES), F32)
           + 2 * _nbytes((FOX_HEADS, GATE_ROWS, V7X_LANES), BF16))
    return pl.pallas_call(
        _gates_kernel,
        grid=(bsz, steps),
        in_specs=[pl.BlockSpec((GATE_ROWS, kdim), lambda b, i: (b * steps + i, 0)),
                  rows_spec(FOX_HEADS, IN_F_LO), rows_spec(GLA_GATE_RANK, IN_LOW_LO),
                  pl.BlockSpec((1, V7X_LANES), lambda b, i: (0, 0)),
                  pl.BlockSpec((GATE_ROWS, GATE_ROWS), lambda b, i: (0, 0))],
        out_specs=[pl.BlockSpec((GATE_ROWS, V7X_LANES), lambda b, i: (b * steps + i, 0)), aug_spec, aug_spec],
        out_shape=[jax.ShapeDtypeStruct((m, V7X_LANES), F32), aug_shape, aug_shape],
        scratch_shapes=[pltpu.VMEM((8, V7X_LANES), F32)],
        compiler_params=pltpu.CompilerParams(
            dimension_semantics=("arbitrary", "arbitrary"),
            vmem_limit_bytes=_vmem_limit(blk, 16 * _nbytes((GATE_ROWS, V7X_LANES), F32))),
        name="gates",
    )(x, w_in_t, w_in_t, bias, tri)


def _fox_kernel(q_ref, k_ref, v_ref, aq_ref, ak_ref, g_ref, o_ref, kp_ref, vp_ref, m_ref, acc_ref, *, tq, tk):
    qi = pl.program_id(2)
    n_heads = aq_ref.shape[1]
    d = q_ref.shape[2] // n_heads
    dp = acc_ref.shape[2]

    @pl.when(qi == 0)
    def _():
        lane = lax.broadcasted_iota(jnp.int32, (vp_ref.shape[1], V7X_LANES), 1)
        for g in range(n_heads):
            kp_ref[g, :, :d] = k_ref[0, :, g * d:(g + 1) * d]
            kp_ref[g, :, d:] = ak_ref[0, g]
            vp_ref[g, :, :d] = v_ref[0, :, g * d:(g + 1) * d]
            vp_ref[g, :, d:] = jnp.where(lane == 0, 1.0, 0.0).astype(BF16)

    qp = [jnp.concatenate([q_ref[0, :, g * d:(g + 1) * d], aq_ref[0, g]], axis=1) for g in range(n_heads)]
    m_ref[...] = jnp.full_like(m_ref, NEG_BIG)
    acc_ref[...] = jnp.zeros_like(acc_ref)

    def step(j, masked):
        start = pl.multiple_of(j * tk, tk)
        for g in range(n_heads):
            s = lax.dot_general(qp[g], kp_ref[g, pl.ds(start, tk), :], _NT, preferred_element_type=F32)
            if masked:
                row = lax.broadcasted_iota(jnp.int32, (tq, tk), 0)
                col = lax.broadcasted_iota(jnp.int32, (tq, tk), 1)
                s = jnp.where(row >= col, s, NEG_BIG)
            m_prev = m_ref[g]
            m_new = jnp.maximum(m_prev, jnp.max(s, axis=-1, keepdims=True))
            p = jnp.exp2(s - jnp.concatenate([m_new] * (tk // V7X_LANES), axis=1)).astype(BF16)
            alpha = jnp.exp2(m_prev - m_new)
            acc_ref[g] = (jnp.concatenate([alpha] * (dp // V7X_LANES), axis=1) * acc_ref[g]
                          + jnp.dot(p, vp_ref[g, pl.ds(start, tk), :], preferred_element_type=F32))
            m_ref[g] = m_new

    unroll = FOX_UNROLL
    shift = unroll.bit_length() - 1
    assert unroll == 1 << shift

    def body(i, carry):
        for u in range(unroll):
            step(unroll * i + u, False)
        return carry

    lax.fori_loop(0, lax.shift_right_logical(qi, shift), body, 0)

    for rem in range(unroll):
        @pl.when((qi & (unroll - 1)) == rem)
        def _():
            for u in range(rem):
                step(qi - rem + u, False)
            step(qi, True)

    for g in range(n_heads):
        acc = acc_ref[g]
        o = acc[:, :d] / acc[:, d:d + 1]
        o = o * lax.rsqrt(jnp.mean(o * o, axis=-1, keepdims=True) + EPS) * g_ref[g]
        o_ref[0, :, g * d:(g + 1) * d] = o.astype(o_ref.dtype)


def _fox(proj, aq, ak, norm_g, *, tq=FOX_TQ, tk=FOX_TK, group=FOX_GROUP):
    bsz, seq, _ = proj.shape
    d = FOX_HEAD_DIM
    dp = d + V7X_LANES
    gd = group * d
    n_groups = FOX_HEADS // group
    assert tq == tk and seq % tq == 0 and FOX_HEADS % group == 0
    blk = (2 * _nbytes((tq, gd), BF16) + 2 * _nbytes((seq, gd), BF16) + _nbytes((group, tq, V7X_LANES), BF16)
           + _nbytes((group, seq, V7X_LANES), BF16) + _nbytes((group, 8, d), F32))
    scratch = (2 * _nbytes((group, seq, dp), BF16) + _nbytes((group, tq, V7X_LANES), F32)
               + _nbytes((group, tq, dp), F32))
    return pl.pallas_call(
        functools.partial(_fox_kernel, tq=tq, tk=tk),
        grid=(bsz, n_groups, seq // tq),
        in_specs=[pl.BlockSpec((1, tq, gd), lambda b, h, i: (b, i, h)),
                  pl.BlockSpec((1, seq, gd), lambda b, h, i: (b, 0, n_groups + h)),
                  pl.BlockSpec((1, seq, gd), lambda b, h, i: (b, 0, 2 * n_groups + h)),
                  pl.BlockSpec((1, group, tq, V7X_LANES), lambda b, h, i: (b, h, i, 0)),
                  pl.BlockSpec((1, group, seq, V7X_LANES), lambda b, h, i: (b, h, 0, 0)),
                  pl.BlockSpec((group, 1, d), lambda b, h, i: (h, 0, 0))],
        out_specs=pl.BlockSpec((1, tq, gd), lambda b, h, i: (b, i, h)),
        out_shape=jax.ShapeDtypeStruct((bsz, seq, MIX_WIDTH), BF16),
        scratch_shapes=[pltpu.VMEM((group, seq, dp), BF16), pltpu.VMEM((group, seq, dp), BF16),
                        pltpu.VMEM((group, tq, V7X_LANES), F32), pltpu.VMEM((group, tq, dp), F32)],
        compiler_params=pltpu.CompilerParams(
            dimension_semantics=("arbitrary", "arbitrary", "arbitrary"),
            vmem_limit_bytes=_vmem_limit(blk, scratch + 8 * group * _nbytes((tq, tk), F32))),
        name="fox_attention",
    )(proj, proj, proj, aq, ak, norm_g)


def _gla_tables(chunk):
    idx = np.arange(chunk)
    masks = []
    half = chunk // 2
    while half >= 1:
        blk = idx // (2 * half)
        second = (idx // half) % 2 == 1
        masks.append(((blk[:, None] == blk[None, :]) & second[:, None] & ~second[None, :]).astype(np.float32))
        half //= 2
    span_chunks = V7X_MXU_DIM // chunk
    tri = np.kron(np.eye(span_chunks, dtype=np.float32), np.tril(np.ones((chunk, chunk), np.float32)))
    return tri, np.stack(masks, axis=0)


def _level_refs(b):
    chunk, width = b.shape
    refs = []
    half = chunk // 2
    while half >= 4:
        pieces = [jnp.broadcast_to(b[s + half - 1:s + half], (2 * half, width)) for s in range(0, chunk, 2 * half)]
        refs.append(pieces[0] if len(pieces) == 1 else jnp.concatenate(pieces, axis=0))
        half //= 2
    prev1 = pltpu.roll(b, 1, 0)
    prev2 = pltpu.roll(b, 2, 0)
    next1 = pltpu.roll(b, chunk - 1, 0)
    pos = lax.broadcasted_iota(jnp.int32, b.shape, 0) & 3
    refs.append(jnp.where(pos == 0, next1, jnp.where(pos == 1, b, jnp.where(pos == 2, prev1, prev2))))
    refs.append(jnp.where((pos & 1) == 0, b, prev1))
    return refs


def _gla_kernel(q_ref, k_ref, v_ref, gg_ref, gates_ref, wup_ref, bgk_ref, gn_ref, tri_ref, mask_ref,
                mixed_hbm_ref, o_ref, state_ref, la_ref, *, rows, chunk, scale):
    del mixed_hbm_ref
    dk, dv = GLA_KEY_DIM, GLA_VAL_DIM

    @pl.when(pl.program_id(1) == 0)
    def _():
        state_ref[...] = jnp.zeros_like(state_ref)

    pre = _dot_split2(gates_ref[0], wup_ref[...]) + bgk_ref[...]
    log_a = _log_sigmoid(pre) * (LOG2_E / GLA_GATE_NORMALIZER)
    span = tri_ref.shape[0]
    width = log_a.shape[1]
    for r in range(0, rows, span):
        parts = jnp.concatenate([p.astype(BF16) for p in _split3(log_a[r:r + span])], axis=1)
        b3 = jnp.dot(tri_ref[...], parts, preferred_element_type=F32)
        la_ref[r:r + span, :] = b3[:, :width] + b3[:, width:2 * width] + b3[:, 2 * width:]

    n_levels = mask_ref.shape[0]
    eye = (lax.broadcasted_iota(jnp.int32, (chunk, chunk), 0)
           == lax.broadcasted_iota(jnp.int32, (chunk, chunk), 1))
    row_dk = lax.broadcasted_iota(jnp.int32, (chunk, dk), 0)
    row_e = lax.broadcasted_iota(jnp.int32, (BF16_SUBLANES, dk), 0)
    ones_cols = jnp.ones((BF16_SUBLANES, V7X_LANES), BF16)

    def head_chunk(r0, h):
        ks = slice(h * dk, (h + 1) * dk)
        vs = slice(h * dv, (h + 1) * dv)
        b = la_ref[pl.ds(r0, chunk), ks]
        q = q_ref[0, pl.ds(r0, chunk), ks].astype(F32) * scale
        k = k_ref[0, pl.ds(r0, chunk), ks].astype(F32)
        v = v_ref[0, pl.ds(r0, chunk), vs]

        state = state_ref[h]
        o = jnp.dot((q * jnp.exp2(b)).astype(BF16), state.astype(BF16), preferred_element_type=F32)

        scores = jnp.zeros((chunk, chunk), F32)
        half = chunk // 2
        for lvl, ref in enumerate(_level_refs(b)):
            u = (jnp.where((row_dk & half) != 0, q, k) * jnp.exp2(-jnp.abs(b - ref))).astype(BF16)
            scores = scores + mask_ref[lvl] * lax.dot_general(u, u, _NT, preferred_element_type=F32)
            half //= 2
        scores = jnp.where(eye, jnp.sum(q * k, axis=-1, keepdims=True), scores)
        o = o + jnp.dot(scores.astype(BF16), v, preferred_element_type=F32)

        b_last = b[chunk - 1:chunk]
        k_dec = (k * jnp.exp2(b_last - b)).astype(BF16)
        e1, e2, e3 = _split3(jnp.broadcast_to(jnp.exp2(b_last), row_e.shape))
        e_rows = jnp.where(row_e == 0, e1, jnp.where(row_e == 1, e2, jnp.where(row_e == 2, e3, 0.0)))
        dec_col = lax.dot_general(e_rows.astype(BF16), ones_cols, _TN, preferred_element_type=F32)
        decay = jnp.concatenate([dec_col] * (dv // V7X_LANES), axis=1)
        state_ref[h] = state * decay + lax.dot_general(k_dec, v, _TN, preferred_element_type=F32)

        o = o * lax.rsqrt(jnp.mean(o * o, axis=-1, keepdims=True) + EPS) * gn_ref[...]
        gate = _silu(gg_ref[0, pl.ds(r0, chunk), vs].astype(F32))
        o_ref[0, pl.ds(r0, chunk), vs] = (o * gate).astype(o_ref.dtype)

    assert n_levels == chunk.bit_length() - 1

    def chunk_body(c, carry):
        r0 = pl.multiple_of(c * chunk, chunk)
        for h in range(GLA_HEADS):
            head_chunk(r0, h)
        return carry

    lax.fori_loop(0, rows // chunk, chunk_body, 0)


def _gla(proj, col0, gates, w_up_pad, b_gk, norm_g, mixed, *, rows=GLA_ROWS, chunk=GLA_CHUNK):
    bsz, seq, _ = proj.shape
    kw, vw = GLA_KEY_WIDTH, GLA_WIDTH
    assert seq % rows == 0 and rows % chunk == 0 and FOX_WIDTH == vw
    assert col0 % kw == 0 and (col0 + 2 * kw) % vw == 0
    q_blk = col0 // kw
    v_blk = (col0 + 2 * kw) // vw
    tri, masks = _gla_tables(chunk)
    blk = (2 * _nbytes((rows, kw), BF16) + 3 * _nbytes((rows, vw), BF16) + _nbytes((rows, V7X_LANES), F32)
           + _nbytes((V7X_LANES, kw), F32) + _nbytes(masks.shape, F32))
    scratch = _nbytes((GLA_HEADS, GLA_KEY_DIM, GLA_VAL_DIM), F32) + _nbytes((rows, kw), F32)
    return pl.pallas_call(
        functools.partial(_gla_kernel, rows=rows, chunk=chunk, scale=GLA_KEY_DIM ** -0.5),
        grid=(bsz, seq // rows),
        in_specs=[pl.BlockSpec((1, rows, kw), lambda b, t: (b, t, q_blk)),
                  pl.BlockSpec((1, rows, kw), lambda b, t: (b, t, q_blk + 1)),
                  pl.BlockSpec((1, rows, vw), lambda b, t: (b, t, v_blk)),
                  pl.BlockSpec((1, rows, vw), lambda b, t: (b, t, v_blk + 1)),
                  pl.BlockSpec((1, rows, V7X_LANES), lambda b, t: (b, t, 0)),
                  pl.BlockSpec((V7X_LANES, kw), lambda b, t: (0, 0)),
                  pl.BlockSpec((1, kw), lambda b, t: (0, 0)),
                  pl.BlockSpec((1, GLA_VAL_DIM), lambda b, t: (0, 0)),
                  pl.BlockSpec(tri.shape, lambda b, t: (0, 0)),
                  pl.BlockSpec(masks.shape, lambda b, t: (0, 0, 0)),
                  pl.BlockSpec(memory_space=pl.ANY)],
        out_specs=pl.BlockSpec((1, rows, vw), lambda b, t: (b, t, 1)),
        out_shape=jax.ShapeDtypeStruct(mixed.shape, mixed.dtype),
        scratch_shapes=[pltpu.VMEM((GLA_HEADS, GLA_KEY_DIM, GLA_VAL_DIM), F32), pltpu.VMEM((rows, kw), F32)],
        input_output_aliases={10: 0},
        compiler_params=pltpu.CompilerParams(
            dimension_semantics=("arbitrary", "arbitrary"),
            vmem_limit_bytes=_vmem_limit(blk, scratch + (8 << 20))),
        name="gla",
    )(proj, proj, proj, proj, gates, w_up_pad, b_gk.reshape(1, -1), norm_g.reshape(1, -1),
      jnp.asarray(tri, BF16), jnp.asarray(masks), mixed)


def _prep_layer(b_f, w_gk_up):
    gate_bias = jnp.concatenate([b_f, jnp.zeros((V7X_LANES - FOX_HEADS,), F32)]).reshape(1, V7X_LANES)
    w_up_pad = jnp.zeros((V7X_LANES, GLA_KEY_WIDTH), F32).at[FOX_HEADS:FOX_HEADS + GLA_GATE_RANK].set(w_gk_up)
    return gate_bias, w_up_pad


def kernel(x, ln_in_g, ln_in_b, w_in, b_f, w_gk_up, b_gk, fox_norm_g, gla_norm_g, w_out, ln1_g, ln1_b,
           w_gate, w_up, w_down, ln2_g, ln2_b):
    bsz, seq, d_model = x.shape
    depth = w_in.shape[0]
    alpha = (2 * depth) ** 0.25
    rows = bsz * seq

    w_in_t = jnp.swapaxes(w_in, 1, 2)
    n_fox_tiles = IN_F_LO // MM_TN
    n_proj_tiles = n_fox_tiles + (IN_LOW_LO - IN_G_LO) // MM_TN

    def proj_row_start(j):
        return pl.multiple_of(j * MM_TN + jnp.where(j >= n_fox_tiles, IN_G_LO - IN_F_LO, 0), F32_SUBLANES)

    t = x.reshape(rows, d_model)
    ln_g, ln_b = ln_in_g, ln_in_b
    x16, mu, rstd = _layer_norm(t, ln_g, ln_b)
    for l in range(depth):
        gate_bias, w_up_pad = _prep_layer(b_f[l], w_gk_up[l])

        proj = _matmul_wt(x16, w_in_t, l, n_proj_tiles, proj_row_start, BF16, scaled_cols=FOX_WIDTH,
                          scale=FOX_HEAD_DIM ** -0.5 * LOG2_E).reshape(bsz, seq, -1)
        gates, aug_q, aug_k = _gates(x16, w_in_t, l, gate_bias, bsz, seq)
        gates = gates.reshape(bsz, seq, V7X_LANES)

        mixed = _fox(proj, aug_q, aug_k, fox_norm_g[l].reshape(FOX_HEADS, 1, FOX_HEAD_DIM))
        mixed = _gla(proj, IN_F_LO, gates, w_up_pad, b_gk[l], gla_norm_g[l], mixed)

        t = _matmul(mixed.reshape(rows, MIX_WIDTH), w_out, l, d_model, (t, mu, rstd, ln_g, ln_b), alpha)
        ln_g, ln_b = ln1_g[l], ln1_b[l]
        x16, mu, rstd = _layer_norm(t, ln_g, ln_b)

        hidden = _gate_up(x16, w_gate, w_up, l)
        t = _down(hidden, w_down, l, (t, mu, rstd, ln_g, ln_b), alpha)
        ln_g, ln_b = ln2_g[l], ln2_b[l]
        if l < depth - 1:
            x16, mu, rstd = _layer_norm(t, ln_g, ln_b)
    out, = _layer_norm(t, ln_g, ln_b, final=True)
    return out.reshape(bsz, seq, d_model)
```

```python
import functools

import numpy as np
import jax
import jax.numpy as jnp
from jax import lax
from jax.experimental import pallas as pl
from jax.experimental.pallas import tpu as pltpu

F32 = jnp.float32
BF16 = jnp.bfloat16

V7X_VMEM_BYTES = 64 * 2**20
V7X_LANES = 128
V7X_MXU_DIM = 256
F32_SUBLANES = 8
BF16_SUBLANES = 16

FOX_HEADS = 16
FOX_HEAD_DIM = 128
FOX_WIDTH = FOX_HEADS * FOX_HEAD_DIM
GLA_HEADS = 4
GLA_KEY_DIM = 256
GLA_VAL_DIM = 512
GLA_KEY_WIDTH = GLA_HEADS * GLA_KEY_DIM
GLA_WIDTH = GLA_HEADS * GLA_VAL_DIM
GLA_GATE_RANK = 16
GLA_GATE_NORMALIZER = 16.0
MIX_WIDTH = FOX_WIDTH + GLA_WIDTH
EPS = 1e-5
IN_F_LO = 3 * FOX_WIDTH
IN_G_LO = IN_F_LO + FOX_HEADS
IN_LOW_LO = IN_G_LO + 2 * GLA_KEY_WIDTH + 2 * GLA_WIDTH

LN_ROWS = 256
LN_SUB_ROWS = 128
MM_TM = 1024
MM_TN = 1024
OUT_TN = 512
FFN_TN = 512
FFN_PAD = 512
DOWN_TM = 512
DOWN_TN = 512
GATE_ROWS = 512
FOX_TQ = 512
FOX_TK = 512
FOX_GROUP = 4
FOX_UNROLL = 4
GLA_ROWS = 512
GLA_CHUNK = 64
NEG_BIG = -1e30
LOG2_E = 1.4426950408889634

_NT = (((1,), (1,)), ((), ()))
_TN = (((0,), (0,)), ((), ()))


def _vmem_limit(block_bytes, extra_bytes):
    need = 2 * block_bytes + extra_bytes + (4 << 20)
    return int(min(need, V7X_VMEM_BYTES - (6 << 20)))


def _nbytes(shape, dtype):
    return int(np.prod(shape)) * jnp.dtype(dtype).itemsize


def _log_sigmoid(x):
    return jnp.minimum(x, 0.0) - jnp.log1p(jnp.exp(-jnp.abs(x)))


def _silu(x):
    return x * (1.0 / (1.0 + jnp.exp(-x)))


def _ln_stats(t):
    mu = jnp.mean(t, axis=-1, keepdims=True)
    tc = t - mu
    return mu, lax.rsqrt(jnp.mean(tc * tc, axis=-1, keepdims=True) + EPS)


def _ln_apply(t, mu, rstd, g, b):
    return (t - mu) * rstd * g + b


def _ln_kernel(x_ref, g_ref, b_ref, *outs, final):
    rows, d = x_ref.shape
    lane_tiles = [slice(c, c + V7X_LANES) for c in range(0, d, V7X_LANES)]
    for r in range(0, rows, LN_SUB_ROWS):
        rs = slice(r, r + LN_SUB_ROWS)
        acc = x_ref[rs, lane_tiles[0]]
        for ls in lane_tiles[1:]:
            acc = acc + x_ref[rs, ls]
        mu = jnp.broadcast_to(jnp.sum(acc, axis=-1, keepdims=True) * (1.0 / d), acc.shape)
        acc = jnp.zeros_like(acc)
        for ls in lane_tiles:
            tc = x_ref[rs, ls] - mu
            acc = acc + tc * tc
        rstd = jnp.broadcast_to(lax.rsqrt(jnp.sum(acc, axis=-1, keepdims=True) * (1.0 / d) + EPS), acc.shape)
        for ls in lane_tiles:
            y = _ln_apply(x_ref[rs, ls], mu, rstd, g_ref[:, ls], b_ref[:, ls])
            outs[0][rs, ls] = y if final else y.astype(BF16)
        if not final:
            outs[1][rs, :] = mu
            outs[2][rs, :] = rstd


def _layer_norm(t, g, b, *, final=False):
    rows, d = t.shape
    assert rows % LN_ROWS == 0
    row_spec = pl.BlockSpec((LN_ROWS, d), lambda i: (i, 0))
    vec_spec = pl.BlockSpec((1, d), lambda i: (0, 0))
    stat_spec = pl.BlockSpec((LN_ROWS, V7X_LANES), lambda i: (i, 0))
    if final:
        out_shape, out_specs = [jax.ShapeDtypeStruct((rows, d), F32)], [row_spec]
    else:
        stat_shape = jax.ShapeDtypeStruct((rows, V7X_LANES), F32)
        out_shape, out_specs = [jax.ShapeDtypeStruct((rows, d), BF16), stat_shape, stat_shape], [row_spec, stat_spec, stat_spec]
    blk = 2 * _nbytes((LN_ROWS, d), F32) + 2 * _nbytes((LN_ROWS, V7X_LANES), F32)
    return pl.pallas_call(
        functools.partial(_ln_kernel, final=final),
        grid=(rows // LN_ROWS,),
        in_specs=[row_spec, vec_spec, vec_spec], out_specs=out_specs, out_shape=out_shape,
        compiler_params=pltpu.CompilerParams(
            dimension_semantics=("arbitrary",),
            vmem_limit_bytes=_vmem_limit(blk, 4 * _nbytes((LN_ROWS, d), F32))),
        name="layer_norm",
    )(t, g.reshape(1, d), b.reshape(1, d))


def _ws_kernel(x_ref, *rest, n_w, transposed, epilogue, k_valid, n_res):
    w_refs = rest[:n_w]
    res_refs = rest[n_w:n_w + n_res]
    o_ref, wb_ref = rest[n_w + n_res:]
    j, i = pl.program_id(0), pl.program_id(1)
    tn = wb_ref.shape[2] // n_w
    ck = w_refs[0].shape[-1] if transposed else w_refs[0].shape[0]

    def convert_chunk(slot):
        rows = pl.ds(pl.multiple_of(i * ck, ck), ck)
        for t, w_ref in enumerate(w_refs):
            chunk = w_ref[0].T if transposed else w_ref[...]
            if k_valid < wb_ref.shape[1]:
                row = i * ck + lax.broadcasted_iota(jnp.int32, chunk.shape, 0)
                chunk = jnp.where(row < k_valid, chunk, 0.0)
            wb_ref[slot, rows, t * tn:(t + 1) * tn] = chunk.astype(BF16)

    @pl.when(j == 0)
    def _():
        convert_chunk(0)

    for parity in (0, 1):
        @pl.when(jnp.logical_and(j > 0, (j & 1) == parity))
        def _():
            convert_chunk(parity)
            acc = jnp.dot(x_ref[...], wb_ref[1 - parity], preferred_element_type=F32)
            o_ref[...] = epilogue(acc, j - 1, [r[...] for r in res_refs]).astype(o_ref.dtype)


def _ws_matmul(x, weights, w_specs, n_tiles, tn_w, tn_out, out_dtype, epilogue, *, transposed, name, tm=MM_TM,
               k_valid=None, res=None):
    m, kdim = x.shape
    n_m = m // tm
    ck = kdim // n_m
    assert m % tm == 0 and kdim % n_m == 0 and ck % (V7X_LANES if transposed else BF16_SUBLANES) == 0
    k_valid = kdim if k_valid is None else k_valid
    n_w = len(weights)

    def row_tile(j, i):
        return jnp.where(j == 0, 0, i)

    def col_tile(j, i):
        return jnp.maximum(j - 1, 0)

    out_spec = pl.BlockSpec((tm, tn_out), lambda j, i: (row_tile(j, i), col_tile(j, i)))
    res_arrays, res_specs, res_bytes = [], [], 0
    if res is not None:
        t, mu, rstd, g, b = res
        stat_spec = pl.BlockSpec((tm, V7X_LANES), lambda j, i: (row_tile(j, i), 0))
        vec_spec = pl.BlockSpec((1, tn_out), lambda j, i: (0, col_tile(j, i)))
        res_arrays = [t, mu, rstd, g.reshape(1, -1), b.reshape(1, -1)]
        res_specs = [out_spec, stat_spec, stat_spec, vec_spec, vec_spec]
        res_bytes = _nbytes((tm, tn_out), F32) + 2 * _nbytes((tm, V7X_LANES), F32)
    blk = (_nbytes((tm, kdim), x.dtype) + n_w * _nbytes((ck, tn_w), F32) + _nbytes((tm, tn_out), out_dtype)
           + res_bytes)
    scratch = _nbytes((2, kdim, n_w * tn_w), BF16)
    extra = scratch + 2 * _nbytes((tm, n_w * tn_w), F32) + 3 * n_w * _nbytes((ck, tn_w), F32)
    return pl.pallas_call(
        functools.partial(_ws_kernel, n_w=n_w, transposed=transposed, epilogue=epilogue, k_valid=k_valid,
                          n_res=len(res_arrays)),
        grid=(n_tiles + 1, n_m),
        in_specs=[pl.BlockSpec((tm, kdim), lambda j, i: (row_tile(j, i), 0))] + w_specs(ck) + res_specs,
        out_specs=out_spec,
        out_shape=jax.ShapeDtypeStruct((m, n_tiles * tn_out), out_dtype),
        scratch_shapes=[pltpu.VMEM((2, kdim, n_w * tn_w), BF16)],
        compiler_params=pltpu.CompilerParams(
            dimension_semantics=("arbitrary", "arbitrary"),
            vmem_limit_bytes=_vmem_limit(blk, extra)),
        name=name,
    )(x, *weights, *res_arrays)


def _residual_epilogue(alpha):
    def epilogue(acc, tile, res):
        t, mu, rstd, g, b = res
        reps = t.shape[1] // V7X_LANES
        x = _ln_apply(t, jnp.concatenate([mu] * reps, axis=1), jnp.concatenate([rstd] * reps, axis=1), g, b)
        return alpha * x + acc
    return epilogue


def _matmul(x, w, layer, n_cols, res, alpha, *, tn=OUT_TN):
    assert n_cols % tn == 0 and w.shape[1] == x.shape[1]
    n_tiles = n_cols // tn

    def w_specs(ck):
        return [pl.BlockSpec((None, ck, tn), lambda j, i: (layer, i, jnp.minimum(j, n_tiles - 1)))]

    return _ws_matmul(x, [w], w_specs, n_tiles, tn, tn, F32, _residual_epilogue(alpha),
                      transposed=False, name="matmul", res=res)


def _matmul_wt(x, w_t, layer, n_tiles, row_start, out_dtype, *, tn=MM_TN, scaled_cols=0, scale=1.0):
    assert w_t.shape[2] == x.shape[1] and scaled_cols % tn == 0
    scaled_tiles = scaled_cols // tn

    def w_specs(ck):
        return [pl.BlockSpec((pl.Element(1), pl.Element(tn), pl.Element(ck)),
                             lambda j, i: (layer, row_start(jnp.minimum(j, n_tiles - 1)), i * ck))]

    def epilogue(acc, tile, res):
        return acc * jnp.where(tile < scaled_tiles, scale, 1.0) if scaled_tiles else acc

    return _ws_matmul(x, [w_t], w_specs, n_tiles, tn, tn, out_dtype, epilogue, transposed=True, name="matmul_wt")


def _gate_up(x, w_gate, w_up, layer, *, tn=FFN_TN):
    d_ff = w_gate.shape[2]
    n_out = pl.cdiv(d_ff, FFN_PAD) * FFN_PAD
    assert n_out % tn == 0 and w_gate.shape[1] == x.shape[1]
    last_w_blk = pl.cdiv(d_ff, tn) - 1

    def w_specs(ck):
        return [pl.BlockSpec((None, ck, tn), lambda j, i: (layer, i, jnp.minimum(j, last_w_blk)))] * 2

    def epilogue(z, tile, res):
        h = _silu(z[:, :tn]) * z[:, tn:]
        col = tile * tn + lax.broadcasted_iota(jnp.int32, h.shape, 1)
        return jnp.where(col < d_ff, h, 0.0)

    return _ws_matmul(x, [w_gate, w_up], w_specs, n_out // tn, tn, tn, BF16, epilogue,
                      transposed=False, name="gate_up")


def _down(h, w_down, layer, res, alpha, *, tm=DOWN_TM, tn=DOWN_TN):
    d_ff, n = w_down.shape[1:]
    assert n % tn == 0 and h.shape[1] >= d_ff
    n_tiles = n // tn

    def w_specs(ck):
        assert h.shape[1] - d_ff < ck
        return [pl.BlockSpec((None, ck, tn), lambda j, i: (layer, i, jnp.minimum(j, n_tiles - 1)))]

    return _ws_matmul(h, [w_down], w_specs, n_tiles, tn, tn, F32, _residual_epilogue(alpha),
                      transposed=False, name="down_proj", tm=tm, k_valid=d_ff, res=res)


def _split3(x):
    a = x.astype(BF16).astype(F32)
    r = x - a
    b = r.astype(BF16).astype(F32)
    return a, b, (r - b).astype(BF16).astype(F32)


def _dot_split2(a, b):
    a_hi = a.astype(BF16)
    b_hi = b.astype(BF16)
    a_lo = (a - a_hi.astype(F32)).astype(BF16)
    b_lo = (b - b_hi.astype(F32)).astype(BF16)
    return (jnp.dot(a_hi, b_hi, preferred_element_type=F32) + jnp.dot(a_hi, b_lo, preferred_element_type=F32)
            + jnp.dot(a_lo, b_hi, preferred_element_type=F32))


def _gates_kernel(x_ref, wf_ref, wl_ref, bias_ref, tri_ref, o_ref, aq_ref, ak_ref, carry_ref):
    @pl.when(pl.program_id(1) == 0)
    def _():
        carry_ref[...] = jnp.zeros_like(carry_ref)

    wf, wl = wf_ref[0], wl_ref[0]
    pad = jnp.zeros((V7X_LANES - wf.shape[0] - wl.shape[0], wf.shape[1]), F32)
    w = jnp.concatenate([wf, wl, pad], axis=0).astype(BF16)
    z = lax.dot_general(x_ref[...], w, _NT, preferred_element_type=F32)
    log_f = _log_sigmoid(z + bias_ref[...])
    parts = jnp.concatenate([p.astype(BF16) for p in _split3(log_f)], axis=1)
    c3 = jnp.dot(tri_ref[...], parts, preferred_element_type=F32)
    nl = log_f.shape[1]
    c = c3[:, :nl] + c3[:, nl:2 * nl] + c3[:, 2 * nl:] + carry_ref[0:1, :]
    rows = c.shape[0]
    carry_ref[...] = jnp.broadcast_to(c[rows - 1:rows, :], carry_ref.shape)
    lane = lax.broadcasted_iota(jnp.int32, c.shape, 1)
    o_ref[...] = jnp.where(lane < FOX_HEADS, c, z)

    c2 = c * LOG2_E
    for h in range(FOX_HEADS):
        c1, c2_, c3 = _split3(jnp.broadcast_to(c2[:, h:h + 1], c.shape))
        aq = jnp.where(lane == 0, c1, jnp.where(lane == 1, c2_, jnp.where(lane == 2, c3,
                       jnp.where(lane < 6, 1.0, 0.0))))
        ak = jnp.where(lane < 3, 1.0, jnp.where(lane == 3, -c1, jnp.where(lane == 4, -c2_,
                       jnp.where(lane == 5, -c3, 0.0))))
        aq_ref[0, h] = aq.astype(BF16)
        ak_ref[0, h] = ak.astype(BF16)


def _gates(x, w_in_t, layer, bias, bsz, seq):
    m, kdim = x.shape
    steps = seq // GATE_ROWS
    tri = jnp.asarray(np.tril(np.ones((GATE_ROWS, GATE_ROWS), np.float32)), BF16)
    aug_shape = jax.ShapeDtypeStruct((bsz, FOX_HEADS, seq, V7X_LANES), BF16)
    aug_spec = pl.BlockSpec((1, FOX_HEADS, GATE_ROWS, V7X_LANES), lambda b, i: (b, 0, i, 0))
    assert IN_F_LO % F32_SUBLANES == 0 and IN_LOW_LO % F32_SUBLANES == 0

    def rows_spec(n_rows, row0):
        return pl.BlockSpec((pl.Element(1), pl.Element(n_rows), pl.Element(kdim)), lambda b, i: (layer, row0, 0))

    blk = (_nbytes((GATE_ROWS, kdim), x.dtype) + _nbytes((V7X_LANES, kdim), F32)
           + _nbytes((GATE_ROWS, GATE_ROWS), F32) + _nbytes((GATE_ROWS, V7X_LANES), F32)
           + 2 * _nbytes((FOX_HEADS, GATE_ROWS, V7X_LANES), BF16))
    return pl.pallas_call(
        _gates_kernel,
        grid=(bsz, steps),
        in_specs=[pl.BlockSpec((GATE_ROWS, kdim), lambda b, i: (b * steps + i, 0)),
                  rows_spec(FOX_HEADS, IN_F_LO), rows_spec(GLA_GATE_RANK, IN_LOW_LO),
                  pl.BlockSpec((1, V7X_LANES), lambda b, i: (0, 0)),
                  pl.BlockSpec((GATE_ROWS, GATE_ROWS), lambda b, i: (0, 0))],
        out_specs=[pl.BlockSpec((GATE_ROWS, V7X_LANES), lambda b, i: (b * steps + i, 0)), aug_spec, aug_spec],
        out_shape=[jax.ShapeDtypeStruct((m, V7X_LANES), F32), aug_shape, aug_shape],
        scratch_shapes=[pltpu.VMEM((8, V7X_LANES), F32)],
        compiler_params=pltpu.CompilerParams(
            dimension_semantics=("arbitrary", "arbitrary"),
            vmem_limit_bytes=_vmem_limit(blk, 16 * _nbytes((GATE_ROWS, V7X_LANES), F32))),
        name="gates",
    )(x, w_in_t, w_in_t, bias, tri)


def _fox_kernel(q_ref, k_ref, v_ref, aq_ref, ak_ref, g_ref, o_ref, kp_ref, vp_ref, m_ref, acc_ref, *, tq, tk):
    qi = pl.program_id(2)
    n_heads = aq_ref.shape[1]
    d = q_ref.shape[2] // n_heads
    dp = acc_ref.shape[2]

    @pl.when(qi == 0)
    def _():
        lane = lax.broadcasted_iota(jnp.int32, (vp_ref.shape[1], V7X_LANES), 1)
        for g in range(n_heads):
            for c in range(0, kp_ref.shape[2], tk):
                kp_ref[g, :d, c:c + tk] = k_ref[0, c:c + tk, g * d:(g + 1) * d].T
                kp_ref[g, d:, c:c + tk] = ak_ref[0, g, c:c + tk, :].T
            vp_ref[g, :, :d] = v_ref[0, :, g * d:(g + 1) * d]
            vp_ref[g, :, d:] = jnp.where(lane == 0, 1.0, 0.0).astype(BF16)

    qp = [jnp.concatenate([q_ref[0, :, g * d:(g + 1) * d], aq_ref[0, g]], axis=1) for g in range(n_heads)]
    m_ref[...] = jnp.full_like(m_ref, NEG_BIG)
    acc_ref[...] = jnp.zeros_like(acc_ref)

    def step(j, masked):
        start = pl.multiple_of(j * tk, tk)
        for g in range(n_heads):
            s = jnp.dot(qp[g], kp_ref[g, :, pl.ds(start, tk)], preferred_element_type=F32)
            if masked:
                row = lax.broadcasted_iota(jnp.int32, (tq, tk), 0)
                col = lax.broadcasted_iota(jnp.int32, (tq, tk), 1)
                s = jnp.where(row >= col, s, NEG_BIG)
            m_prev = m_ref[g]
            m_new = jnp.maximum(m_prev, jnp.max(s, axis=-1, keepdims=True))
            p = jnp.exp2(s - jnp.concatenate([m_new] * (tk // V7X_LANES), axis=1)).astype(BF16)
            alpha = jnp.exp2(m_prev - m_new)
            acc_ref[g] = (jnp.concatenate([alpha] * (dp // V7X_LANES), axis=1) * acc_ref[g]
                          + jnp.dot(p, vp_ref[g, pl.ds(start, tk), :], preferred_element_type=F32))
            m_ref[g] = m_new

    unroll = FOX_UNROLL
    shift = unroll.bit_length() - 1
    assert unroll == 1 << shift

    def body(i, carry):
        for u in range(unroll):
            step(unroll * i + u, False)
        return carry

    lax.fori_loop(0, lax.shift_right_logical(qi, shift), body, 0)

    for rem in range(unroll):
        @pl.when((qi & (unroll - 1)) == rem)
        def _():
            for u in range(rem):
                step(qi - rem + u, False)
            step(qi, True)

    for g in range(n_heads):
        acc = acc_ref[g]
        o = acc[:, :d] / acc[:, d:d + 1]
        o = o * lax.rsqrt(jnp.mean(o * o, axis=-1, keepdims=True) + EPS) * g_ref[g]
        o_ref[0, :, g * d:(g + 1) * d] = o.astype(o_ref.dtype)


def _fox(proj, aq, ak, norm_g, *, tq=FOX_TQ, tk=FOX_TK, group=FOX_GROUP):
    bsz, seq, _ = proj.shape
    d = FOX_HEAD_DIM
    dp = d + V7X_LANES
    gd = group * d
    n_groups = FOX_HEADS // group
    assert tq == tk and seq % tq == 0 and FOX_HEADS % group == 0
    blk = (2 * _nbytes((tq, gd), BF16) + 2 * _nbytes((seq, gd), BF16) + _nbytes((group, tq, V7X_LANES), BF16)
           + _nbytes((group, seq, V7X_LANES), BF16) + _nbytes((group, 8, d), F32))
    scratch = (2 * _nbytes((group, seq, dp), BF16) + _nbytes((group, tq, V7X_LANES), F32)
               + _nbytes((group, tq, dp), F32))
    return pl.pallas_call(
        functools.partial(_fox_kernel, tq=tq, tk=tk),
        grid=(bsz, n_groups, seq // tq),
        in_specs=[pl.BlockSpec((1, tq, gd), lambda b, h, i: (b, i, h)),
                  pl.BlockSpec((1, seq, gd), lambda b, h, i: (b, 0, n_groups + h)),
                  pl.BlockSpec((1, seq, gd), lambda b, h, i: (b, 0, 2 * n_groups + h)),
                  pl.BlockSpec((1, group, tq, V7X_LANES), lambda b, h, i: (b, h, i, 0)),
                  pl.BlockSpec((1, group, seq, V7X_LANES), lambda b, h, i: (b, h, 0, 0)),
                  pl.BlockSpec((group, 1, d), lambda b, h, i: (h, 0, 0))],
        out_specs=pl.BlockSpec((1, tq, gd), lambda b, h, i: (b, i, h)),
        out_shape=jax.ShapeDtypeStruct((bsz, seq, MIX_WIDTH), BF16),
        scratch_shapes=[pltpu.VMEM((group, dp, seq), BF16), pltpu.VMEM((group, seq, dp), BF16),
                        pltpu.VMEM((group, tq, V7X_LANES), F32), pltpu.VMEM((group, tq, dp), F32)],
        compiler_params=pltpu.CompilerParams(
            dimension_semantics=("arbitrary", "arbitrary", "arbitrary"),
            vmem_limit_bytes=_vmem_limit(blk, scratch + 8 * group * _nbytes((tq, tk), F32))),
        name="fox_attention",
    )(proj, proj, proj, aq, ak, norm_g)


def _gla_tables(chunk):
    idx = np.arange(chunk)
    masks = []
    half = chunk // 2
    while half >= 1:
        blk = idx // (2 * half)
        second = (idx // half) % 2 == 1
        masks.append(((blk[:, None] == blk[None, :]) & second[:, None] & ~second[None, :]).astype(np.float32))
        half //= 2
    span_chunks = V7X_MXU_DIM // chunk
    tri = np.kron(np.eye(span_chunks, dtype=np.float32), np.tril(np.ones((chunk, chunk), np.float32)))
    return tri, np.stack(masks, axis=0)


def _level_refs(b):
    chunk, width = b.shape
    refs = []
    half = chunk // 2
    while half >= 4:
        pieces = [jnp.broadcast_to(b[s + half - 1:s + half], (2 * half, width)) for s in range(0, chunk, 2 * half)]
        refs.append(pieces[0] if len(pieces) == 1 else jnp.concatenate(pieces, axis=0))
        half //= 2
    prev1 = pltpu.roll(b, 1, 0)
    prev2 = pltpu.roll(b, 2, 0)
    next1 = pltpu.roll(b, chunk - 1, 0)
    pos = lax.broadcasted_iota(jnp.int32, b.shape, 0) & 3
    refs.append(jnp.where(pos == 0, next1, jnp.where(pos == 1, b, jnp.where(pos == 2, prev1, prev2))))
    refs.append(jnp.where((pos & 1) == 0, b, prev1))
    return refs


def _gla_kernel(q_ref, k_ref, v_ref, gg_ref, gates_ref, wup_ref, bgk_ref, gn_ref, tri_ref, mask_ref,
                mixed_hbm_ref, o_ref, state_ref, la_ref, *, rows, chunk, scale):
    del mixed_hbm_ref
    dk, dv = GLA_KEY_DIM, GLA_VAL_DIM

    @pl.when(pl.program_id(1) == 0)
    def _():
        state_ref[...] = jnp.zeros_like(state_ref)

    pre = _dot_split2(gates_ref[0], wup_ref[...]) + bgk_ref[...]
    log_a = _log_sigmoid(pre) * (LOG2_E / GLA_GATE_NORMALIZER)
    span = tri_ref.shape[0]
    width = log_a.shape[1]
    for r in range(0, rows, span):
        parts = jnp.concatenate([p.astype(BF16) for p in _split3(log_a[r:r + span])], axis=1)
        b3 = jnp.dot(tri_ref[...], parts, preferred_element_type=F32)
        la_ref[r:r + span, :] = b3[:, :width] + b3[:, width:2 * width] + b3[:, 2 * width:]

    n_levels = mask_ref.shape[0]
    eye = (lax.broadcasted_iota(jnp.int32, (chunk, chunk), 0)
           == lax.broadcasted_iota(jnp.int32, (chunk, chunk), 1))
    row_dk = lax.broadcasted_iota(jnp.int32, (chunk, dk), 0)
    row_e = lax.broadcasted_iota(jnp.int32, (BF16_SUBLANES, dk), 0)
    ones_cols = jnp.ones((BF16_SUBLANES, V7X_LANES), BF16)

    def head_chunk(r0, h):
        ks = slice(h * dk, (h + 1) * dk)
        vs = slice(h * dv, (h + 1) * dv)
        b = la_ref[pl.ds(r0, chunk), ks]
        q = q_ref[0, pl.ds(r0, chunk), ks].astype(F32) * scale
        k = k_ref[0, pl.ds(r0, chunk), ks].astype(F32)
        v = v_ref[0, pl.ds(r0, chunk), vs]

        state = state_ref[h]
        o = jnp.dot((q * jnp.exp2(b)).astype(BF16), state.astype(BF16), preferred_element_type=F32)

        scores = jnp.zeros((chunk, chunk), F32)
        half = chunk // 2
        for lvl, ref in enumerate(_level_refs(b)):
            u = (jnp.where((row_dk & half) != 0, q, k) * jnp.exp2(-jnp.abs(b - ref))).astype(BF16)
            scores = scores + mask_ref[lvl] * lax.dot_general(u, u, _NT, preferred_element_type=F32)
            half //= 2
        scores = jnp.where(eye, jnp.sum(q * k, axis=-1, keepdims=True), scores)
        o = o + jnp.dot(scores.astype(BF16), v, preferred_element_type=F32)

        b_last = b[chunk - 1:chunk]
        k_dec = (k * jnp.exp2(b_last - b)).astype(BF16)
        e1, e2, e3 = _split3(jnp.broadcast_to(jnp.exp2(b_last), row_e.shape))
        e_rows = jnp.where(row_e == 0, e1, jnp.where(row_e == 1, e2, jnp.where(row_e == 2, e3, 0.0)))
        dec_col = lax.dot_general(e_rows.astype(BF16), ones_cols, _TN, preferred_element_type=F32)
        decay = jnp.concatenate([dec_col] * (dv // V7X_LANES), axis=1)
        state_ref[h] = state * decay + lax.dot_general(k_dec, v, _TN, preferred_element_type=F32)

        o = o * lax.rsqrt(jnp.mean(o * o, axis=-1, keepdims=True) + EPS) * gn_ref[...]
        gate = _silu(gg_ref[0, pl.ds(r0, chunk), vs].astype(F32))
        o_ref[0, pl.ds(r0, chunk), vs] = (o * gate).astype(o_ref.dtype)

    assert n_levels == chunk.bit_length() - 1

    def chunk_body(c, carry):
        r0 = pl.multiple_of(c * chunk, chunk)
        for h in range(GLA_HEADS):
            head_chunk(r0, h)
        return carry

    lax.fori_loop(0, rows // chunk, chunk_body, 0)


def _gla(proj, col0, gates, w_up_pad, b_gk, norm_g, mixed, *, rows=GLA_ROWS, chunk=GLA_CHUNK):
    bsz, seq, _ = proj.shape
    kw, vw = GLA_KEY_WIDTH, GLA_WIDTH
    assert seq % rows == 0 and rows % chunk == 0 and FOX_WIDTH == vw
    assert col0 % kw == 0 and (col0 + 2 * kw) % vw == 0
    q_blk = col0 // kw
    v_blk = (col0 + 2 * kw) // vw
    tri, masks = _gla_tables(chunk)
    blk = (2 * _nbytes((rows, kw), BF16) + 3 * _nbytes((rows, vw), BF16) + _nbytes((rows, V7X_LANES), F32)
           + _nbytes((V7X_LANES, kw), F32) + _nbytes(masks.shape, F32))
    scratch = _nbytes((GLA_HEADS, GLA_KEY_DIM, GLA_VAL_DIM), F32) + _nbytes((rows, kw), F32)
    return pl.pallas_call(
        functools.partial(_gla_kernel, rows=rows, chunk=chunk, scale=GLA_KEY_DIM ** -0.5),
        grid=(bsz, seq // rows),
        in_specs=[pl.BlockSpec((1, rows, kw), lambda b, t: (b, t, q_blk)),
                  pl.BlockSpec((1, rows, kw), lambda b, t: (b, t, q_blk + 1)),
                  pl.BlockSpec((1, rows, vw), lambda b, t: (b, t, v_blk)),
                  pl.BlockSpec((1, rows, vw), lambda b, t: (b, t, v_blk + 1)),
                  pl.BlockSpec((1, rows, V7X_LANES), lambda b, t: (b, t, 0)),
                  pl.BlockSpec((V7X_LANES, kw), lambda b, t: (0, 0)),
                  pl.BlockSpec((1, kw), lambda b, t: (0, 0)),
                  pl.BlockSpec((1, GLA_VAL_DIM), lambda b, t: (0, 0)),
                  pl.BlockSpec(tri.shape, lambda b, t: (0, 0)),
                  pl.BlockSpec(masks.shape, lambda b, t: (0, 0, 0)),
                  pl.BlockSpec(memory_space=pl.ANY)],
        out_specs=pl.BlockSpec((1, rows, vw), lambda b, t: (b, t, 1)),
        out_shape=jax.ShapeDtypeStruct(mixed.shape, mixed.dtype),
        scratch_shapes=[pltpu.VMEM((GLA_HEADS, GLA_KEY_DIM, GLA_VAL_DIM), F32), pltpu.VMEM((rows, kw), F32)],
        input_output_aliases={10: 0},
        compiler_params=pltpu.CompilerParams(
            dimension_semantics=("arbitrary", "arbitrary"),
            vmem_limit_bytes=_vmem_limit(blk, scratch + (8 << 20))),
        name="gla",
    )(proj, proj, proj, proj, gates, w_up_pad, b_gk.reshape(1, -1), norm_g.reshape(1, -1),
      jnp.asarray(tri, BF16), jnp.asarray(masks), mixed)


def _prep_layer(b_f, w_gk_up):
    gate_bias = jnp.concatenate([b_f, jnp.zeros((V7X_LANES - FOX_HEADS,), F32)]).reshape(1, V7X_LANES)
    w_up_pad = jnp.zeros((V7X_LANES, GLA_KEY_WIDTH), F32).at[FOX_HEADS:FOX_HEADS + GLA_GATE_RANK].set(w_gk_up)
    return gate_bias, w_up_pad


def kernel(x, ln_in_g, ln_in_b, w_in, b_f, w_gk_up, b_gk, fox_norm_g, gla_norm_g, w_out, ln1_g, ln1_b,
           w_gate, w_up, w_down, ln2_g, ln2_b):
    bsz, seq, d_model = x.shape
    depth = w_in.shape[0]
    alpha = (2 * depth) ** 0.25
    rows = bsz * seq

    w_in_t = jnp.swapaxes(w_in, 1, 2)
    n_fox_tiles = IN_F_LO // MM_TN
    n_proj_tiles = n_fox_tiles + (IN_LOW_LO - IN_G_LO) // MM_TN

    def proj_row_start(j):
        return pl.multiple_of(j * MM_TN + jnp.where(j >= n_fox_tiles, IN_G_LO - IN_F_LO, 0), F32_SUBLANES)

    t = x.reshape(rows, d_model)
    ln_g, ln_b = ln_in_g, ln_in_b
    x16, mu, rstd = _layer_norm(t, ln_g, ln_b)
    for l in range(depth):
        gate_bias, w_up_pad = _prep_layer(b_f[l], w_gk_up[l])

        proj = _matmul_wt(x16, w_in_t, l, n_proj_tiles, proj_row_start, BF16, scaled_cols=FOX_WIDTH,
                          scale=FOX_HEAD_DIM ** -0.5 * LOG2_E).reshape(bsz, seq, -1)
        gates, aug_q, aug_k = _gates(x16, w_in_t, l, gate_bias, bsz, seq)
        gates = gates.reshape(bsz, seq, V7X_LANES)

        mixed = _fox(proj, aug_q, aug_k, fox_norm_g[l].reshape(FOX_HEADS, 1, FOX_HEAD_DIM))
        mixed = _gla(proj, IN_F_LO, gates, w_up_pad, b_gk[l], gla_norm_g[l], mixed)

        t = _matmul(mixed.reshape(rows, MIX_WIDTH), w_out, l, d_model, (t, mu, rstd, ln_g, ln_b), alpha)
        ln_g, ln_b = ln1_g[l], ln1_b[l]
        x16, mu, rstd = _layer_norm(t, ln_g, ln_b)

        hidden = _gate_up(x16, w_gate, w_up, l)
        t = _down(hidden, w_down, l, (t, mu, rstd, ln_g, ln_b), alpha)
        ln_g, ln_b = ln2_g[l], ln2_b[l]
        if l < depth - 1:
            x16, mu, rstd = _layer_norm(t, ln_g, ln_b)
    out, = _layer_norm(t, ln_g, ln_b, final=True)
    return out.reshape(bsz, seq, d_model)
```

```python
import functools

import numpy as np
import jax
import jax.numpy as jnp
from jax import lax
from jax.experimental import pallas as pl
from jax.experimental.pallas import tpu as pltpu

F32 = jnp.float32
BF16 = jnp.bfloat16

V7X_VMEM_BYTES = 64 * 2**20
V7X_LANES = 128
V7X_MXU_DIM = 256
F32_SUBLANES = 8
BF16_SUBLANES = 16

FOX_HEADS = 16
FOX_HEAD_DIM = 128
FOX_WIDTH = FOX_HEADS * FOX_HEAD_DIM
GLA_HEADS = 4
GLA_KEY_DIM = 256
GLA_VAL_DIM = 512
GLA_KEY_WIDTH = GLA_HEADS * GLA_KEY_DIM
GLA_WIDTH = GLA_HEADS * GLA_VAL_DIM
GLA_GATE_RANK = 16
GLA_GATE_NORMALIZER = 16.0
MIX_WIDTH = FOX_WIDTH + GLA_WIDTH
EPS = 1e-5
IN_F_LO = 3 * FOX_WIDTH
IN_G_LO = IN_F_LO + FOX_HEADS
IN_LOW_LO = IN_G_LO + 2 * GLA_KEY_WIDTH + 2 * GLA_WIDTH

LN_ROWS = 256
MM_TM = 1024
MM_TN = 1024
OUT_TN = 512
FFN_TN = 512
DOWN_TM = 512
DOWN_TN = 512
GATE_ROWS = 512
FOX_TQ = 512
FOX_TK = 512
FOX_GROUP = 4
FOX_UNROLL = 4
GLA_ROWS = 512
GLA_CHUNK = 64
NEG_BIG = -1e30
LOG2_E = 1.4426950408889634

_NT = (((1,), (1,)), ((), ()))
_TN = (((0,), (0,)), ((), ()))


def _vmem_limit(block_bytes, extra_bytes):
    need = 2 * block_bytes + extra_bytes + (4 << 20)
    return int(min(need, V7X_VMEM_BYTES - (6 << 20)))


def _nbytes(shape, dtype):
    return int(np.prod(shape)) * jnp.dtype(dtype).itemsize


def _log_sigmoid(x):
    return jnp.minimum(x, 0.0) - jnp.log1p(jnp.exp(-jnp.abs(x)))


def _silu(x):
    return x * (1.0 / (1.0 + jnp.exp(-x)))


def _ln_stats(t):
    mu = jnp.mean(t, axis=-1, keepdims=True)
    tc = t - mu
    return mu, lax.rsqrt(jnp.mean(tc * tc, axis=-1, keepdims=True) + EPS)


def _ln_apply(t, mu, rstd, g, b):
    return (t - mu) * rstd * g + b


def _ln_kernel(x_ref, g_ref, b_ref, *outs, final):
    t = x_ref[...]
    mu, rstd = _ln_stats(t)
    y = _ln_apply(t, mu, rstd, g_ref[...], b_ref[...])
    if final:
        outs[0][...] = y
    else:
        outs[0][...] = y.astype(BF16)
        outs[1][...] = jnp.broadcast_to(mu, outs[1].shape)
        outs[2][...] = jnp.broadcast_to(rstd, outs[2].shape)


def _layer_norm(t, g, b, *, final=False):
    rows, d = t.shape
    assert rows % LN_ROWS == 0
    row_spec = pl.BlockSpec((LN_ROWS, d), lambda i: (i, 0))
    vec_spec = pl.BlockSpec((1, d), lambda i: (0, 0))
    stat_spec = pl.BlockSpec((LN_ROWS, V7X_LANES), lambda i: (i, 0))
    if final:
        out_shape, out_specs = [jax.ShapeDtypeStruct((rows, d), F32)], [row_spec]
    else:
        stat_shape = jax.ShapeDtypeStruct((rows, V7X_LANES), F32)
        out_shape, out_specs = [jax.ShapeDtypeStruct((rows, d), BF16), stat_shape, stat_shape], [row_spec, stat_spec, stat_spec]
    blk = 2 * _nbytes((LN_ROWS, d), F32) + 2 * _nbytes((LN_ROWS, V7X_LANES), F32)
    return pl.pallas_call(
        functools.partial(_ln_kernel, final=final),
        grid=(rows // LN_ROWS,),
        in_specs=[row_spec, vec_spec, vec_spec], out_specs=out_specs, out_shape=out_shape,
        compiler_params=pltpu.CompilerParams(
            dimension_semantics=("arbitrary",),
            vmem_limit_bytes=_vmem_limit(blk, 4 * _nbytes((LN_ROWS, d), F32))),
        name="layer_norm",
    )(t, g.reshape(1, d), b.reshape(1, d))


def _ws_kernel(x_ref, *rest, n_w, transposed, epilogue, n_res):
    w_refs = rest[:n_w]
    res_refs = rest[n_w:n_w + n_res]
    o_ref, wb_ref = rest[n_w + n_res:]
    j, i = pl.program_id(0), pl.program_id(1)
    tn = wb_ref.shape[2] // n_w
    ck = w_refs[0].shape[-1] if transposed else w_refs[0].shape[0]

    def convert_chunk(slot):
        rows = pl.ds(pl.multiple_of(i * ck, ck), ck)
        for t, w_ref in enumerate(w_refs):
            chunk = w_ref[0].T if transposed else w_ref[...]
            wb_ref[slot, rows, t * tn:(t + 1) * tn] = chunk.astype(BF16)

    @pl.when(j == 0)
    def _():
        convert_chunk(0)

    for parity in (0, 1):
        @pl.when(jnp.logical_and(j > 0, (j & 1) == parity))
        def _():
            convert_chunk(parity)
            acc = jnp.dot(x_ref[...], wb_ref[1 - parity], preferred_element_type=F32)
            o_ref[...] = epilogue(acc, j - 1, [r[...] for r in res_refs]).astype(o_ref.dtype)


def _ws_matmul(x, weights, w_specs, n_tiles, tn_w, tn_out, out_dtype, epilogue, *, transposed, name, tm=MM_TM,
               res=None, n_out=None):
    m, kdim = x.shape
    n_m = m // tm
    ck = kdim // n_m
    assert m % tm == 0 and kdim % n_m == 0 and ck % (V7X_LANES if transposed else BF16_SUBLANES) == 0
    n_w = len(weights)

    def row_tile(j, i):
        return jnp.where(j == 0, 0, i)

    def col_tile(j, i):
        return jnp.maximum(j - 1, 0)

    out_spec = pl.BlockSpec((tm, tn_out), lambda j, i: (row_tile(j, i), col_tile(j, i)))
    res_arrays, res_specs, res_bytes = [], [], 0
    if res is not None:
        t, mu, rstd, g, b = res
        stat_spec = pl.BlockSpec((tm, V7X_LANES), lambda j, i: (row_tile(j, i), 0))
        vec_spec = pl.BlockSpec((1, tn_out), lambda j, i: (0, col_tile(j, i)))
        res_arrays = [t, mu, rstd, g.reshape(1, -1), b.reshape(1, -1)]
        res_specs = [out_spec, stat_spec, stat_spec, vec_spec, vec_spec]
        res_bytes = _nbytes((tm, tn_out), F32) + 2 * _nbytes((tm, V7X_LANES), F32)
    blk = (_nbytes((tm, kdim), x.dtype) + n_w * _nbytes((ck, tn_w), F32) + _nbytes((tm, tn_out), out_dtype)
           + res_bytes)
    scratch = _nbytes((2, kdim, n_w * tn_w), BF16)
    extra = scratch + 2 * _nbytes((tm, n_w * tn_w), F32) + 3 * n_w * _nbytes((ck, tn_w), F32)
    return pl.pallas_call(
        functools.partial(_ws_kernel, n_w=n_w, transposed=transposed, epilogue=epilogue, n_res=len(res_arrays)),
        grid=(n_tiles + 1, n_m),
        in_specs=[pl.BlockSpec((tm, kdim), lambda j, i: (row_tile(j, i), 0))] + w_specs(ck) + res_specs,
        out_specs=out_spec,
        out_shape=jax.ShapeDtypeStruct((m, n_tiles * tn_out if n_out is None else n_out), out_dtype),
        scratch_shapes=[pltpu.VMEM((2, kdim, n_w * tn_w), BF16)],
        compiler_params=pltpu.CompilerParams(
            dimension_semantics=("arbitrary", "arbitrary"),
            vmem_limit_bytes=_vmem_limit(blk, extra)),
        name=name,
    )(x, *weights, *res_arrays)


def _residual_epilogue(alpha):
    def epilogue(acc, tile, res):
        t, mu, rstd, g, b = res
        reps = t.shape[1] // V7X_LANES
        x = _ln_apply(t, jnp.concatenate([mu] * reps, axis=1), jnp.concatenate([rstd] * reps, axis=1), g, b)
        return alpha * x + acc
    return epilogue


def _matmul(x, w, layer, n_cols, res, alpha, *, tn=OUT_TN):
    assert n_cols % tn == 0 and w.shape[1] == x.shape[1]
    n_tiles = n_cols // tn

    def w_specs(ck):
        return [pl.BlockSpec((None, ck, tn), lambda j, i: (layer, i, jnp.minimum(j, n_tiles - 1)))]

    return _ws_matmul(x, [w], w_specs, n_tiles, tn, tn, F32, _residual_epilogue(alpha),
                      transposed=False, name="matmul", res=res)


def _matmul_wt(x, w_t, layer, n_tiles, row_start, out_dtype, *, tn=MM_TN, scaled_cols=0, scale=1.0):
    assert w_t.shape[2] == x.shape[1] and scaled_cols % tn == 0
    scaled_tiles = scaled_cols // tn

    def w_specs(ck):
        return [pl.BlockSpec((pl.Element(1), pl.Element(tn), pl.Element(ck)),
                             lambda j, i: (layer, row_start(jnp.minimum(j, n_tiles - 1)), i * ck))]

    def epilogue(acc, tile, res):
        return acc * jnp.where(tile < scaled_tiles, scale, 1.0) if scaled_tiles else acc

    return _ws_matmul(x, [w_t], w_specs, n_tiles, tn, tn, out_dtype, epilogue, transposed=True, name="matmul_wt")


def _gate_up(x, w_gate, w_up, layer, *, tn=FFN_TN):
    d_ff = w_gate.shape[2]
    assert w_gate.shape[1] == x.shape[1]
    n_tiles = pl.cdiv(d_ff, tn)

    def w_specs(ck):
        return [pl.BlockSpec((None, ck, tn), lambda j, i: (layer, i, jnp.minimum(j, n_tiles - 1)))] * 2

    def epilogue(z, tile, res):
        return _silu(z[:, :tn]) * z[:, tn:]

    return _ws_matmul(x, [w_gate, w_up], w_specs, n_tiles, tn, tn, BF16, epilogue,
                      transposed=False, name="gate_up", n_out=d_ff)


def _down(h, w_down, layer, res, alpha, *, tm=DOWN_TM, tn=DOWN_TN):
    d_ff, n = w_down.shape[1:]
    assert n % tn == 0 and h.shape[1] == d_ff
    n_tiles = n // tn

    def w_specs(ck):
        return [pl.BlockSpec((None, ck, tn), lambda j, i: (layer, i, jnp.minimum(j, n_tiles - 1)))]

    return _ws_matmul(h, [w_down], w_specs, n_tiles, tn, tn, F32, _residual_epilogue(alpha),
                      transposed=False, name="down_proj", tm=tm, res=res)


def _split3(x):
    a = x.astype(BF16).astype(F32)
    r = x - a
    b = r.astype(BF16).astype(F32)
    return a, b, (r - b).astype(BF16).astype(F32)


def _dot_split2(a, b):
    a_hi = a.astype(BF16)
    b_hi = b.astype(BF16)
    a_lo = (a - a_hi.astype(F32)).astype(BF16)
    b_lo = (b - b_hi.astype(F32)).astype(BF16)
    return (jnp.dot(a_hi, b_hi, preferred_element_type=F32) + jnp.dot(a_hi, b_lo, preferred_element_type=F32)
            + jnp.dot(a_lo, b_hi, preferred_element_type=F32))


def _gates_kernel(x_ref, wf_ref, wl_ref, bias_ref, tri_ref, o_ref, aq_ref, ak_ref, carry_ref):
    @pl.when(pl.program_id(1) == 0)
    def _():
        carry_ref[...] = jnp.zeros_like(carry_ref)

    wf, wl = wf_ref[0], wl_ref[0]
    pad = jnp.zeros((V7X_LANES - wf.shape[0] - wl.shape[0], wf.shape[1]), F32)
    w = jnp.concatenate([wf, wl, pad], axis=0).astype(BF16)
    z = lax.dot_general(x_ref[...], w, _NT, preferred_element_type=F32)
    log_f = _log_sigmoid(z + bias_ref[...])
    parts = jnp.concatenate([p.astype(BF16) for p in _split3(log_f)], axis=1)
    c3 = jnp.dot(tri_ref[...], parts, preferred_element_type=F32)
    nl = log_f.shape[1]
    c = c3[:, :nl] + c3[:, nl:2 * nl] + c3[:, 2 * nl:] + carry_ref[0:1, :]
    rows = c.shape[0]
    carry_ref[...] = jnp.broadcast_to(c[rows - 1:rows, :], carry_ref.shape)
    lane = lax.broadcasted_iota(jnp.int32, c.shape, 1)
    o_ref[...] = jnp.where(lane < FOX_HEADS, c, z)

    c2 = c * LOG2_E
    for h in range(FOX_HEADS):
        c1, c2_, c3 = _split3(jnp.broadcast_to(c2[:, h:h + 1], c.shape))
        aq = jnp.where(lane == 0, c1, jnp.where(lane == 1, c2_, jnp.where(lane == 2, c3,
                       jnp.where(lane < 6, 1.0, 0.0))))
        ak = jnp.where(lane < 3, 1.0, jnp.where(lane == 3, -c1, jnp.where(lane == 4, -c2_,
                       jnp.where(lane == 5, -c3, 0.0))))
        aq_ref[0, h] = aq.astype(BF16)
        ak_ref[0, h] = ak.astype(BF16)


def _gates(x, w_in_t, layer, bias, bsz, seq):
    m, kdim = x.shape
    steps = seq // GATE_ROWS
    tri = jnp.asarray(np.tril(np.ones((GATE_ROWS, GATE_ROWS), np.float32)), BF16)
    aug_shape = jax.ShapeDtypeStruct((bsz, FOX_HEADS, seq, V7X_LANES), BF16)
    aug_spec = pl.BlockSpec((1, FOX_HEADS, GATE_ROWS, V7X_LANES), lambda b, i: (b, 0, i, 0))
    assert IN_F_LO % F32_SUBLANES == 0 and IN_LOW_LO % F32_SUBLANES == 0

    def rows_spec(n_rows, row0):
        return pl.BlockSpec((pl.Element(1), pl.Element(n_rows), pl.Element(kdim)), lambda b, i: (layer, row0, 0))

    blk = (_nbytes((GATE_ROWS, kdim), x.dtype) + _nbytes((V7X_LANES, kdim), F32)
           + _nbytes((GATE_ROWS, GATE_ROWS), F32) + _nbytes((GATE_ROWS, V7X_LANES), F32)
           + 2 * _nbytes((FOX_HEADS, GATE_ROWS, V7X_LANES), BF16))
    return pl.pallas_call(
        _gates_kernel,
        grid=(bsz, steps),
        in_specs=[pl.BlockSpec((GATE_ROWS, kdim), lambda b, i: (b * steps + i, 0)),
                  rows_spec(FOX_HEADS, IN_F_LO), rows_spec(GLA_GATE_RANK, IN_LOW_LO),
                  pl.BlockSpec((1, V7X_LANES), lambda b, i: (0, 0)),
                  pl.BlockSpec((GATE_ROWS, GATE_ROWS), lambda b, i: (0, 0))],
        out_specs=[pl.BlockSpec((GATE_ROWS, V7X_LANES), lambda b, i: (b * steps + i, 0)), aug_spec, aug_spec],
        out_shape=[jax.ShapeDtypeStruct((m, V7X_LANES), F32), aug_shape, aug_shape],
        scratch_shapes=[pltpu.VMEM((8, V7X_LANES), F32)],
        compiler_params=pltpu.CompilerParams(
            dimension_semantics=("arbitrary", "arbitrary"),
            vmem_limit_bytes=_vmem_limit(blk, 16 * _nbytes((GATE_ROWS, V7X_LANES), F32))),
        name="gates",
    )(x, w_in_t, w_in_t, bias, tri)


def _fox_kernel(q_ref, k_ref, v_ref, aq_ref, ak_ref, g_ref, o_ref, kp_ref, vp_ref, m_ref, acc_ref, *, tq, tk):
    qi = pl.program_id(2)
    n_heads = aq_ref.shape[1]
    d = q_ref.shape[2] // n_heads
    dp = acc_ref.shape[2]

    new_rows = pl.ds(pl.multiple_of(qi * tk, tk), tk)
    lane = lax.broadcasted_iota(jnp.int32, (tk, V7X_LANES), 1)
    for g in range(n_heads):
        kp_ref[g, new_rows, :d] = k_ref[0, :, g * d:(g + 1) * d]
        kp_ref[g, new_rows, d:] = ak_ref[0, g]
        vp_ref[g, new_rows, :d] = v_ref[0, :, g * d:(g + 1) * d]
        vp_ref[g, new_rows, d:] = jnp.where(lane == 0, 1.0, 0.0).astype(BF16)

    qp = [jnp.concatenate([q_ref[0, :, g * d:(g + 1) * d], aq_ref[0, g]], axis=1) for g in range(n_heads)]
    m_ref[...] = jnp.full_like(m_ref, NEG_BIG)
    acc_ref[...] = jnp.zeros_like(acc_ref)

    def step(j, masked):
        start = pl.multiple_of(j * tk, tk)
        for g in range(n_heads):
            s = lax.dot_general(qp[g], kp_ref[g, pl.ds(start, tk), :], _NT, preferred_element_type=F32)
            if masked:
                row = lax.broadcasted_iota(jnp.int32, (tq, tk), 0)
                col = lax.broadcasted_iota(jnp.int32, (tq, tk), 1)
                s = jnp.where(row >= col, s, NEG_BIG)
            m_prev = m_ref[g]
            m_new = jnp.maximum(m_prev, jnp.max(s, axis=-1, keepdims=True))
            p = jnp.exp2(s - jnp.concatenate([m_new] * (tk // V7X_LANES), axis=1)).astype(BF16)
            alpha = jnp.exp2(m_prev - m_new)
            acc_ref[g] = (jnp.concatenate([alpha] * (dp // V7X_LANES), axis=1) * acc_ref[g]
                          + jnp.dot(p, vp_ref[g, pl.ds(start, tk), :], preferred_element_type=F32))
            m_ref[g] = m_new

    unroll = FOX_UNROLL
    shift = unroll.bit_length() - 1
    assert unroll == 1 << shift

    def body(i, carry):
        for u in range(unroll):
            step(unroll * i + u, False)
        return carry

    lax.fori_loop(0, lax.shift_right_logical(qi, shift), body, 0)

    for rem in range(unroll):
        @pl.when((qi & (unroll - 1)) == rem)
        def _():
            for u in range(rem):
                step(qi - rem + u, False)
            step(qi, True)

    for g in range(n_heads):
        acc = acc_ref[g]
        o = acc[:, :d] / acc[:, d:d + 1]
        o = o * lax.rsqrt(jnp.mean(o * o, axis=-1, keepdims=True) + EPS) * g_ref[g]
        o_ref[0, :, g * d:(g + 1) * d] = o.astype(o_ref.dtype)


def _fox(proj, aq, ak, norm_g, *, tq=FOX_TQ, tk=FOX_TK, group=FOX_GROUP):
    bsz, seq, _ = proj.shape
    d = FOX_HEAD_DIM
    dp = d + V7X_LANES
    gd = group * d
    n_groups = FOX_HEADS // group
    assert tq == tk and seq % tq == 0 and FOX_HEADS % group == 0
    blk = 4 * _nbytes((tq, gd), BF16) + 2 * _nbytes((group, tq, V7X_LANES), BF16) + _nbytes((group, 8, d), F32)
    scratch = (2 * _nbytes((group, seq, dp), BF16) + _nbytes((group, tq, V7X_LANES), F32)
               + _nbytes((group, tq, dp), F32))
    return pl.pallas_call(
        functools.partial(_fox_kernel, tq=tq, tk=tk),
        grid=(bsz, n_groups, seq // tq),
        in_specs=[pl.BlockSpec((1, tq, gd), lambda b, h, i: (b, i, h)),
                  pl.BlockSpec((1, tk, gd), lambda b, h, i: (b, i, n_groups + h)),
                  pl.BlockSpec((1, tk, gd), lambda b, h, i: (b, i, 2 * n_groups + h)),
                  pl.BlockSpec((1, group, tq, V7X_LANES), lambda b, h, i: (b, h, i, 0)),
                  pl.BlockSpec((1, group, tk, V7X_LANES), lambda b, h, i: (b, h, i, 0)),
                  pl.BlockSpec((group, 1, d), lambda b, h, i: (h, 0, 0))],
        out_specs=pl.BlockSpec((1, tq, gd), lambda b, h, i: (b, i, h)),
        out_shape=jax.ShapeDtypeStruct((bsz, seq, MIX_WIDTH), BF16),
        scratch_shapes=[pltpu.VMEM((group, seq, dp), BF16), pltpu.VMEM((group, seq, dp), BF16),
                        pltpu.VMEM((group, tq, V7X_LANES), F32), pltpu.VMEM((group, tq, dp), F32)],
        compiler_params=pltpu.CompilerParams(
            dimension_semantics=("arbitrary", "arbitrary", "arbitrary"),
            vmem_limit_bytes=_vmem_limit(blk, scratch + 8 * group * _nbytes((tq, tk), F32))),
        name="fox_attention",
    )(proj, proj, proj, aq, ak, norm_g)


def _gla_tables(chunk):
    idx = np.arange(chunk)
    masks = []
    half = chunk // 2
    while half >= 1:
        blk = idx // (2 * half)
        second = (idx // half) % 2 == 1
        masks.append(((blk[:, None] == blk[None, :]) & second[:, None] & ~second[None, :]).astype(np.float32))
        half //= 2
    span_chunks = V7X_MXU_DIM // chunk
    tri = np.kron(np.eye(span_chunks, dtype=np.float32), np.tril(np.ones((chunk, chunk), np.float32)))
    return tri, np.stack(masks, axis=0)


def _level_refs(b):
    chunk, width = b.shape
    refs = []
    half = chunk // 2
    while half >= 4:
        pieces = [jnp.broadcast_to(b[s + half - 1:s + half], (2 * half, width)) for s in range(0, chunk, 2 * half)]
        refs.append(pieces[0] if len(pieces) == 1 else jnp.concatenate(pieces, axis=0))
        half //= 2
    prev1 = pltpu.roll(b, 1, 0)
    prev2 = pltpu.roll(b, 2, 0)
    next1 = pltpu.roll(b, chunk - 1, 0)
    pos = lax.broadcasted_iota(jnp.int32, b.shape, 0) & 3
    refs.append(jnp.where(pos == 0, next1, jnp.where(pos == 1, b, jnp.where(pos == 2, prev1, prev2))))
    refs.append(jnp.where((pos & 1) == 0, b, prev1))
    return refs


def _gla_kernel(q_ref, k_ref, v_ref, gg_ref, gates_ref, wup_ref, bgk_ref, gn_ref, tri_ref, mask_ref,
                mixed_hbm_ref, o_ref, state_ref, la_ref, *, rows, chunk, scale):
    del mixed_hbm_ref
    dk, dv = GLA_KEY_DIM, GLA_VAL_DIM

    @pl.when(pl.program_id(1) == 0)
    def _():
        state_ref[...] = jnp.zeros_like(state_ref)

    pre = _dot_split2(gates_ref[0], wup_ref[...]) + bgk_ref[...]
    log_a = _log_sigmoid(pre) * (LOG2_E / GLA_GATE_NORMALIZER)
    span = tri_ref.shape[0]
    width = log_a.shape[1]
    for r in range(0, rows, span):
        parts = jnp.concatenate([p.astype(BF16) for p in _split3(log_a[r:r + span])], axis=1)
        b3 = jnp.dot(tri_ref[...], parts, preferred_element_type=F32)
        la_ref[r:r + span, :] = b3[:, :width] + b3[:, width:2 * width] + b3[:, 2 * width:]

    n_levels = mask_ref.shape[0]
    eye = (lax.broadcasted_iota(jnp.int32, (chunk, chunk), 0)
           == lax.broadcasted_iota(jnp.int32, (chunk, chunk), 1))
    row_dk = lax.broadcasted_iota(jnp.int32, (chunk, dk), 0)
    row_e = lax.broadcasted_iota(jnp.int32, (BF16_SUBLANES, dk), 0)
    ones_cols = jnp.ones((BF16_SUBLANES, V7X_LANES), BF16)

    def head_chunk(r0, h):
        ks = slice(h * dk, (h + 1) * dk)
        vs = slice(h * dv, (h + 1) * dv)
        b = la_ref[pl.ds(r0, chunk), ks]
        q = q_ref[0, pl.ds(r0, chunk), ks].astype(F32) * scale
        k = k_ref[0, pl.ds(r0, chunk), ks].astype(F32)
        v = v_ref[0, pl.ds(r0, chunk), vs]

        state = state_ref[h]
        o = jnp.dot((q * jnp.exp2(b)).astype(BF16), state.astype(BF16), preferred_element_type=F32)

        scores = jnp.zeros((chunk, chunk), F32)
        half = chunk // 2
        for lvl, ref in enumerate(_level_refs(b)):
            u = (jnp.where((row_dk & half) != 0, q, k) * jnp.exp2(-jnp.abs(b - ref))).astype(BF16)
            scores = scores + mask_ref[lvl] * lax.dot_general(u, u, _NT, preferred_element_type=F32)
            half //= 2
        scores = jnp.where(eye, jnp.sum(q * k, axis=-1, keepdims=True), scores)
        o = o + jnp.dot(scores.astype(BF16), v, preferred_element_type=F32)

        b_last = b[chunk - 1:chunk]
        k_dec = (k * jnp.exp2(b_last - b)).astype(BF16)
        e1, e2, e3 = _split3(jnp.broadcast_to(jnp.exp2(b_last), row_e.shape))
        e_rows = jnp.where(row_e == 0, e1, jnp.where(row_e == 1, e2, jnp.where(row_e == 2, e3, 0.0)))
        dec_col = lax.dot_general(e_rows.astype(BF16), ones_cols, _TN, preferred_element_type=F32)
        decay = jnp.concatenate([dec_col] * (dv // V7X_LANES), axis=1)
        state_ref[h] = state * decay + lax.dot_general(k_dec, v, _TN, preferred_element_type=F32)

        o = o * lax.rsqrt(jnp.mean(o * o, axis=-1, keepdims=True) + EPS) * gn_ref[...]
        gate = _silu(gg_ref[0, pl.ds(r0, chunk), vs].astype(F32))
        o_ref[0, pl.ds(r0, chunk), vs] = (o * gate).astype(o_ref.dtype)

    assert n_levels == chunk.bit_length() - 1

    def chunk_body(c, carry):
        r0 = pl.multiple_of(c * chunk, chunk)
        for h in range(GLA_HEADS):
            head_chunk(r0, h)
        return carry

    lax.fori_loop(0, rows // chunk, chunk_body, 0)


def _gla(proj, col0, gates, w_up_pad, b_gk, norm_g, mixed, *, rows=GLA_ROWS, chunk=GLA_CHUNK):
    bsz, seq, _ = proj.shape
    kw, vw = GLA_KEY_WIDTH, GLA_WIDTH
    assert seq % rows == 0 and rows % chunk == 0 and FOX_WIDTH == vw
    assert col0 % kw == 0 and (col0 + 2 * kw) % vw == 0
    q_blk = col0 // kw
    v_blk = (col0 + 2 * kw) // vw
    tri, masks = _gla_tables(chunk)
    blk = (2 * _nbytes((rows, kw), BF16) + 3 * _nbytes((rows, vw), BF16) + _nbytes((rows, V7X_LANES), F32)
           + _nbytes((V7X_LANES, kw), F32) + _nbytes(masks.shape, F32))
    scratch = _nbytes((GLA_HEADS, GLA_KEY_DIM, GLA_VAL_DIM), F32) + _nbytes((rows, kw), F32)
    return pl.pallas_call(
        functools.partial(_gla_kernel, rows=rows, chunk=chunk, scale=GLA_KEY_DIM ** -0.5),
        grid=(bsz, seq // rows),
        in_specs=[pl.BlockSpec((1, rows, kw), lambda b, t: (b, t, q_blk)),
                  pl.BlockSpec((1, rows, kw), lambda b, t: (b, t, q_blk + 1)),
                  pl.BlockSpec((1, rows, vw), lambda b, t: (b, t, v_blk)),
                  pl.BlockSpec((1, rows, vw), lambda b, t: (b, t, v_blk + 1)),
                  pl.BlockSpec((1, rows, V7X_LANES), lambda b, t: (b, t, 0)),
                  pl.BlockSpec((V7X_LANES, kw), lambda b, t: (0, 0)),
                  pl.BlockSpec((1, kw), lambda b, t: (0, 0)),
                  pl.BlockSpec((1, GLA_VAL_DIM), lambda b, t: (0, 0)),
                  pl.BlockSpec(tri.shape, lambda b, t: (0, 0)),
                  pl.BlockSpec(masks.shape, lambda b, t: (0, 0, 0)),
                  pl.BlockSpec(memory_space=pl.ANY)],
        out_specs=pl.BlockSpec((1, rows, vw), lambda b, t: (b, t, 1)),
        out_shape=jax.ShapeDtypeStruct(mixed.shape, mixed.dtype),
        scratch_shapes=[pltpu.VMEM((GLA_HEADS, GLA_KEY_DIM, GLA_VAL_DIM), F32), pltpu.VMEM((rows, kw), F32)],
        input_output_aliases={10: 0},
        compiler_params=pltpu.CompilerParams(
            dimension_semantics=("arbitrary", "arbitrary"),
            vmem_limit_bytes=_vmem_limit(blk, scratch + (8 << 20))),
        name="gla",
    )(proj, proj, proj, proj, gates, w_up_pad, b_gk.reshape(1, -1), norm_g.reshape(1, -1),
      jnp.asarray(tri, BF16), jnp.asarray(masks), mixed)


def _prep_layer(b_f, w_gk_up):
    gate_bias = jnp.concatenate([b_f, jnp.zeros((V7X_LANES - FOX_HEADS,), F32)]).reshape(1, V7X_LANES)
    w_up_pad = jnp.zeros((V7X_LANES, GLA_KEY_WIDTH), F32).at[FOX_HEADS:FOX_HEADS + GLA_GATE_RANK].set(w_gk_up)
    return gate_bias, w_up_pad


def kernel(x, ln_in_g, ln_in_b, w_in, b_f, w_gk_up, b_gk, fox_norm_g, gla_norm_g, w_out, ln1_g, ln1_b,
           w_gate, w_up, w_down, ln2_g, ln2_b):
    bsz, seq, d_model = x.shape
    depth = w_in.shape[0]
    alpha = (2 * depth) ** 0.25
    rows = bsz * seq

    w_in_t = jnp.swapaxes(w_in, 1, 2)
    n_fox_tiles = IN_F_LO // MM_TN
    n_proj_tiles = n_fox_tiles + (IN_LOW_LO - IN_G_LO) // MM_TN

    def proj_row_start(j):
        return pl.multiple_of(j * MM_TN + jnp.where(j >= n_fox_tiles, IN_G_LO - IN_F_LO, 0), F32_SUBLANES)

    t = x.reshape(rows, d_model)
    ln_g, ln_b = ln_in_g, ln_in_b
    x16, mu, rstd = _layer_norm(t, ln_g, ln_b)
    for l in range(depth):
        gate_bias, w_up_pad = _prep_layer(b_f[l], w_gk_up[l])

        proj = _matmul_wt(x16, w_in_t, l, n_proj_tiles, proj_row_start, BF16, scaled_cols=FOX_WIDTH,
                          scale=FOX_HEAD_DIM ** -0.5 * LOG2_E).reshape(bsz, seq, -1)
        gates, aug_q, aug_k = _gates(x16, w_in_t, l, gate_bias, bsz, seq)
        gates = gates.reshape(bsz, seq, V7X_LANES)

        mixed = _fox(proj, aug_q, aug_k, fox_norm_g[l].reshape(FOX_HEADS, 1, FOX_HEAD_DIM))
        mixed = _gla(proj, IN_F_LO, gates, w_up_pad, b_gk[l], gla_norm_g[l], mixed)

        t = _matmul(mixed.reshape(rows, MIX_WIDTH), w_out, l, d_model, (t, mu, rstd, ln_g, ln_b), alpha)
        ln_g, ln_b = ln1_g[l], ln1_b[l]
        x16, mu, rstd = _layer_norm(t, ln_g, ln_b)

        hidden = _gate_up(x16, w_gate, w_up, l)
        t = _down(hidden, w_down, l, (t, mu, rstd, ln_g, ln_b), alpha)
        ln_g, ln_b = ln2_g[l], ln2_b[l]
        if l < depth - 1:
            x16, mu, rstd = _layer_norm(t, ln_g, ln_b)
    out, = _layer_norm(t, ln_g, ln_b, final=True)
    return out.reshape(bsz, seq, d_model)
```

```python
import functools

import numpy as np
import jax
import jax.numpy as jnp
from jax import lax
from jax.experimental import pallas as pl
from jax.experimental.pallas import tpu as pltpu

F32 = jnp.float32
BF16 = jnp.bfloat16

V7X_VMEM_BYTES = 64 * 2**20
V7X_LANES = 128
V7X_MXU_DIM = 256
F32_SUBLANES = 8
BF16_SUBLANES = 16

FOX_HEADS = 16
FOX_HEAD_DIM = 128
FOX_WIDTH = FOX_HEADS * FOX_HEAD_DIM
GLA_HEADS = 4
GLA_KEY_DIM = 256
GLA_VAL_DIM = 512
GLA_KEY_WIDTH = GLA_HEADS * GLA_KEY_DIM
GLA_WIDTH = GLA_HEADS * GLA_VAL_DIM
GLA_GATE_RANK = 16
GLA_GATE_NORMALIZER = 16.0
MIX_WIDTH = FOX_WIDTH + GLA_WIDTH
EPS = 1e-5
IN_F_LO = 3 * FOX_WIDTH
IN_G_LO = IN_F_LO + FOX_HEADS
IN_LOW_LO = IN_G_LO + 2 * GLA_KEY_WIDTH + 2 * GLA_WIDTH

LN_ROWS = 256
MM_TM = 1024
MM_TN = 1024
OUT_TN = 512
FFN_TN = 512
DOWN_TM = 512
DOWN_TN = 512
GATE_ROWS = 512
FOX_TQ = 512
FOX_TK = 512
FOX_GROUP = 4
FOX_UNROLL = 4
PROJ_TILE = FOX_GROUP * FOX_HEAD_DIM
GLA_ROWS = 512
GLA_CHUNK = 64
NEG_BIG = -1e30
LOG2_E = 1.4426950408889634

_NT = (((1,), (1,)), ((), ()))
_TN = (((0,), (0,)), ((), ()))


def _vmem_limit(block_bytes, extra_bytes):
    need = 2 * block_bytes + extra_bytes + (4 << 20)
    return int(min(need, V7X_VMEM_BYTES - (6 << 20)))


def _nbytes(shape, dtype):
    return int(np.prod(shape)) * jnp.dtype(dtype).itemsize


def _log_sigmoid(x):
    return jnp.minimum(x, 0.0) - jnp.log1p(jnp.exp(-jnp.abs(x)))


def _silu(x):
    return x * (1.0 / (1.0 + jnp.exp(-x)))


def _ln_stats(t):
    mu = jnp.mean(t, axis=-1, keepdims=True)
    tc = t - mu
    return mu, lax.rsqrt(jnp.mean(tc * tc, axis=-1, keepdims=True) + EPS)


def _ln_apply(t, mu, rstd, g, b):
    return (t - mu) * rstd * g + b


def _ln_kernel(x_ref, g_ref, b_ref, *outs, final):
    t = x_ref[...]
    mu, rstd = _ln_stats(t)
    y = _ln_apply(t, mu, rstd, g_ref[...], b_ref[...])
    if final:
        outs[0][...] = y
    else:
        outs[0][...] = y.astype(BF16)
        outs[1][...] = jnp.broadcast_to(mu, outs[1].shape)
        outs[2][...] = jnp.broadcast_to(rstd, outs[2].shape)


def _layer_norm(t, g, b, *, final=False):
    rows, d = t.shape
    assert rows % LN_ROWS == 0
    row_spec = pl.BlockSpec((LN_ROWS, d), lambda i: (i, 0))
    vec_spec = pl.BlockSpec((1, d), lambda i: (0, 0))
    stat_spec = pl.BlockSpec((LN_ROWS, V7X_LANES), lambda i: (i, 0))
    if final:
        out_shape, out_specs = [jax.ShapeDtypeStruct((rows, d), F32)], [row_spec]
    else:
        stat_shape = jax.ShapeDtypeStruct((rows, V7X_LANES), F32)
        out_shape, out_specs = [jax.ShapeDtypeStruct((rows, d), BF16), stat_shape, stat_shape], [row_spec, stat_spec, stat_spec]
    blk = 2 * _nbytes((LN_ROWS, d), F32) + 2 * _nbytes((LN_ROWS, V7X_LANES), F32)
    return pl.pallas_call(
        functools.partial(_ln_kernel, final=final),
        grid=(rows // LN_ROWS,),
        in_specs=[row_spec, vec_spec, vec_spec], out_specs=out_specs, out_shape=out_shape,
        compiler_params=pltpu.CompilerParams(
            dimension_semantics=("arbitrary",),
            vmem_limit_bytes=_vmem_limit(blk, 4 * _nbytes((LN_ROWS, d), F32))),
        name="layer_norm",
    )(t, g.reshape(1, d), b.reshape(1, d))


def _ws_kernel(x_ref, *rest, n_w, transposed, epilogue, n_res):
    w_refs = rest[:n_w]
    res_refs = rest[n_w:n_w + n_res]
    o_ref, wb_ref = rest[n_w + n_res:]
    j, i = pl.program_id(0), pl.program_id(1)
    tn = wb_ref.shape[2] // n_w
    ck = w_refs[0].shape[-1] if transposed else w_refs[0].shape[0]

    def convert_chunk(slot):
        rows = pl.ds(pl.multiple_of(i * ck, ck), ck)
        for t, w_ref in enumerate(w_refs):
            chunk = w_ref[0].T if transposed else w_ref[...]
            wb_ref[slot, rows, t * tn:(t + 1) * tn] = chunk.astype(BF16)

    @pl.when(j == 0)
    def _():
        convert_chunk(0)

    for parity in (0, 1):
        @pl.when(jnp.logical_and(j > 0, (j & 1) == parity))
        def _():
            convert_chunk(parity)
            acc = jnp.dot(x_ref[...], wb_ref[1 - parity], preferred_element_type=F32)
            out = epilogue(acc, j - 1, [r[...] for r in res_refs]).astype(o_ref.dtype)
            if len(o_ref.shape) == 2:
                o_ref[...] = out
            else:
                width = o_ref.shape[2]
                for s in range(o_ref.shape[0]):
                    o_ref[s] = out[:, s * width:(s + 1) * width]


def _ws_matmul(x, weights, w_specs, n_tiles, tn_w, tn_out, out_dtype, epilogue, *, transposed, name, tm=MM_TM,
               res=None, n_out=None, out_split=None):
    m, kdim = x.shape
    n_m = m // tm
    ck = kdim // n_m
    assert m % tm == 0 and kdim % n_m == 0 and ck % (V7X_LANES if transposed else BF16_SUBLANES) == 0
    n_w = len(weights)

    def row_tile(j, i):
        return jnp.where(j == 0, 0, i)

    def col_tile(j, i):
        return jnp.maximum(j - 1, 0)

    if out_split is None:
        out_spec = pl.BlockSpec((tm, tn_out), lambda j, i: (row_tile(j, i), col_tile(j, i)))
        out_shape = jax.ShapeDtypeStruct((m, n_tiles * tn_out if n_out is None else n_out), out_dtype)
    else:
        assert res is None and n_out is None and tn_out % out_split == 0
        out_spec = pl.BlockSpec((out_split, tm, tn_out // out_split), lambda j, i: (col_tile(j, i), row_tile(j, i), 0))
        out_shape = jax.ShapeDtypeStruct((n_tiles * out_split, m, tn_out // out_split), out_dtype)
    res_arrays, res_specs, res_bytes = [], [], 0
    if res is not None:
        t, mu, rstd, g, b = res
        stat_spec = pl.BlockSpec((tm, V7X_LANES), lambda j, i: (row_tile(j, i), 0))
        vec_spec = pl.BlockSpec((1, tn_out), lambda j, i: (0, col_tile(j, i)))
        res_arrays = [t, mu, rstd, g.reshape(1, -1), b.reshape(1, -1)]
        res_specs = [out_spec, stat_spec, stat_spec, vec_spec, vec_spec]
        res_bytes = _nbytes((tm, tn_out), F32) + 2 * _nbytes((tm, V7X_LANES), F32)
    blk = (_nbytes((tm, kdim), x.dtype) + n_w * _nbytes((ck, tn_w), F32) + _nbytes((tm, tn_out), out_dtype)
           + res_bytes)
    scratch = _nbytes((2, kdim, n_w * tn_w), BF16)
    extra = scratch + 2 * _nbytes((tm, n_w * tn_w), F32) + 3 * n_w * _nbytes((ck, tn_w), F32)
    return pl.pallas_call(
        functools.partial(_ws_kernel, n_w=n_w, transposed=transposed, epilogue=epilogue, n_res=len(res_arrays)),
        grid=(n_tiles + 1, n_m),
        in_specs=[pl.BlockSpec((tm, kdim), lambda j, i: (row_tile(j, i), 0))] + w_specs(ck) + res_specs,
        out_specs=out_spec,
        out_shape=out_shape,
        scratch_shapes=[pltpu.VMEM((2, kdim, n_w * tn_w), BF16)],
        compiler_params=pltpu.CompilerParams(
            dimension_semantics=("arbitrary", "arbitrary"),
            vmem_limit_bytes=_vmem_limit(blk, extra)),
        name=name,
    )(x, *weights, *res_arrays)


def _residual_epilogue(alpha):
    def epilogue(acc, tile, res):
        t, mu, rstd, g, b = res
        reps = t.shape[1] // V7X_LANES
        x = _ln_apply(t, jnp.concatenate([mu] * reps, axis=1), jnp.concatenate([rstd] * reps, axis=1), g, b)
        return alpha * x + acc
    return epilogue


def _matmul(x, w, layer, n_cols, res, alpha, *, tn=OUT_TN):
    assert n_cols % tn == 0 and w.shape[1] == x.shape[1]
    n_tiles = n_cols // tn

    def w_specs(ck):
        return [pl.BlockSpec((None, ck, tn), lambda j, i: (layer, i, jnp.minimum(j, n_tiles - 1)))]

    return _ws_matmul(x, [w], w_specs, n_tiles, tn, tn, F32, _residual_epilogue(alpha),
                      transposed=False, name="matmul", res=res)


def _matmul_wt(x, w_t, layer, n_tiles, row_start, out_dtype, *, tn=MM_TN, scaled_cols=0, scale=1.0, out_split=None):
    assert w_t.shape[2] == x.shape[1] and scaled_cols % tn == 0
    scaled_tiles = scaled_cols // tn

    def w_specs(ck):
        return [pl.BlockSpec((pl.Element(1), pl.Element(tn), pl.Element(ck)),
                             lambda j, i: (layer, row_start(jnp.minimum(j, n_tiles - 1)), i * ck))]

    def epilogue(acc, tile, res):
        return acc * jnp.where(tile < scaled_tiles, scale, 1.0) if scaled_tiles else acc

    return _ws_matmul(x, [w_t], w_specs, n_tiles, tn, tn, out_dtype, epilogue, transposed=True, name="matmul_wt",
                      out_split=out_split)


def _gate_up(x, w_gate, w_up, layer, *, tn=FFN_TN):
    d_ff = w_gate.shape[2]
    assert w_gate.shape[1] == x.shape[1]
    n_tiles = pl.cdiv(d_ff, tn)

    def w_specs(ck):
        return [pl.BlockSpec((None, ck, tn), lambda j, i: (layer, i, jnp.minimum(j, n_tiles - 1)))] * 2

    def epilogue(z, tile, res):
        return _silu(z[:, :tn]) * z[:, tn:]

    return _ws_matmul(x, [w_gate, w_up], w_specs, n_tiles, tn, tn, BF16, epilogue,
                      transposed=False, name="gate_up", n_out=d_ff)


def _down(h, w_down, layer, res, alpha, *, tm=DOWN_TM, tn=DOWN_TN):
    d_ff, n = w_down.shape[1:]
    assert n % tn == 0 and h.shape[1] == d_ff
    n_tiles = n // tn

    def w_specs(ck):
        return [pl.BlockSpec((None, ck, tn), lambda j, i: (layer, i, jnp.minimum(j, n_tiles - 1)))]

    return _ws_matmul(h, [w_down], w_specs, n_tiles, tn, tn, F32, _residual_epilogue(alpha),
                      transposed=False, name="down_proj", tm=tm, res=res)


def _split3(x):
    a = x.astype(BF16).astype(F32)
    r = x - a
    b = r.astype(BF16).astype(F32)
    return a, b, (r - b).astype(BF16).astype(F32)


def _dot_split2(a, b):
    a_hi = a.astype(BF16)
    b_hi = b.astype(BF16)
    a_lo = (a - a_hi.astype(F32)).astype(BF16)
    b_lo = (b - b_hi.astype(F32)).astype(BF16)
    return (jnp.dot(a_hi, b_hi, preferred_element_type=F32) + jnp.dot(a_hi, b_lo, preferred_element_type=F32)
            + jnp.dot(a_lo, b_hi, preferred_element_type=F32))


def _gates_kernel(x_ref, wf_ref, wl_ref, bias_ref, tri_ref, o_ref, aq_ref, ak_ref, carry_ref):
    @pl.when(pl.program_id(1) == 0)
    def _():
        carry_ref[...] = jnp.zeros_like(carry_ref)

    wf, wl = wf_ref[0], wl_ref[0]
    pad = jnp.zeros((V7X_LANES - wf.shape[0] - wl.shape[0], wf.shape[1]), F32)
    w = jnp.concatenate([wf, wl, pad], axis=0).astype(BF16)
    z = lax.dot_general(x_ref[...], w, _NT, preferred_element_type=F32)
    log_f = _log_sigmoid(z + bias_ref[...])
    parts = jnp.concatenate([p.astype(BF16) for p in _split3(log_f)], axis=1)
    c3 = jnp.dot(tri_ref[...], parts, preferred_element_type=F32)
    nl = log_f.shape[1]
    c = c3[:, :nl] + c3[:, nl:2 * nl] + c3[:, 2 * nl:] + carry_ref[0:1, :]
    rows = c.shape[0]
    carry_ref[...] = jnp.broadcast_to(c[rows - 1:rows, :], carry_ref.shape)
    lane = lax.broadcasted_iota(jnp.int32, c.shape, 1)
    o_ref[...] = jnp.where(lane < FOX_HEADS, c, z)

    c2 = c * LOG2_E
    for h in range(FOX_HEADS):
        c1, c2_, c3 = _split3(jnp.broadcast_to(c2[:, h:h + 1], c.shape))
        aq = jnp.where(lane == 0, c1, jnp.where(lane == 1, c2_, jnp.where(lane == 2, c3,
                       jnp.where(lane < 6, 1.0, 0.0))))
        ak = jnp.where(lane < 3, 1.0, jnp.where(lane == 3, -c1, jnp.where(lane == 4, -c2_,
                       jnp.where(lane == 5, -c3, 0.0))))
        aq_ref[0, h] = aq.astype(BF16)
        ak_ref[0, h] = ak.astype(BF16)


def _gates(x, w_in_t, layer, bias, bsz, seq):
    m, kdim = x.shape
    steps = seq // GATE_ROWS
    tri = jnp.asarray(np.tril(np.ones((GATE_ROWS, GATE_ROWS), np.float32)), BF16)
    aug_shape = jax.ShapeDtypeStruct((bsz, FOX_HEADS, seq, V7X_LANES), BF16)
    aug_spec = pl.BlockSpec((1, FOX_HEADS, GATE_ROWS, V7X_LANES), lambda b, i: (b, 0, i, 0))
    assert IN_F_LO % F32_SUBLANES == 0 and IN_LOW_LO % F32_SUBLANES == 0

    def rows_spec(n_rows, row0):
        return pl.BlockSpec((pl.Element(1), pl.Element(n_rows), pl.Element(kdim)), lambda b, i: (layer, row0, 0))

    blk = (_nbytes((GATE_ROWS, kdim), x.dtype) + _nbytes((V7X_LANES, kdim), F32)
           + _nbytes((GATE_ROWS, GATE_ROWS), F32) + _nbytes((GATE_ROWS, V7X_LANES), F32)
           + 2 * _nbytes((FOX_HEADS, GATE_ROWS, V7X_LANES), BF16))
    return pl.pallas_call(
        _gates_kernel,
        grid=(bsz, steps),
        in_specs=[pl.BlockSpec((GATE_ROWS, kdim), lambda b, i: (b * steps + i, 0)),
                  rows_spec(FOX_HEADS, IN_F_LO), rows_spec(GLA_GATE_RANK, IN_LOW_LO),
                  pl.BlockSpec((1, V7X_LANES), lambda b, i: (0, 0)),
                  pl.BlockSpec((GATE_ROWS, GATE_ROWS), lambda b, i: (0, 0))],
        out_specs=[pl.BlockSpec((GATE_ROWS, V7X_LANES), lambda b, i: (b * steps + i, 0)), aug_spec, aug_spec],
        out_shape=[jax.ShapeDtypeStruct((m, V7X_LANES), F32), aug_shape, aug_shape],
        scratch_shapes=[pltpu.VMEM((8, V7X_LANES), F32)],
        compiler_params=pltpu.CompilerParams(
            dimension_semantics=("arbitrary", "arbitrary"),
            vmem_limit_bytes=_vmem_limit(blk, 16 * _nbytes((GATE_ROWS, V7X_LANES), F32))),
        name="gates",
    )(x, w_in_t, w_in_t, bias, tri)


def _fox_kernel(q_ref, k_ref, v_ref, aq_ref, ak_ref, g_ref, o_ref, kp_ref, vp_ref, m_ref, acc_ref, *, tq, tk):
    qi = pl.program_id(2)
    n_heads = aq_ref.shape[1]
    d = q_ref.shape[2] // n_heads
    dp = acc_ref.shape[2]

    new_rows = pl.ds(pl.multiple_of(qi * tk, tk), tk)
    lane = lax.broadcasted_iota(jnp.int32, (tk, V7X_LANES), 1)
    for g in range(n_heads):
        kp_ref[g, new_rows, :d] = k_ref[0, :, g * d:(g + 1) * d]
        kp_ref[g, new_rows, d:] = ak_ref[0, g]
        vp_ref[g, new_rows, :d] = v_ref[0, :, g * d:(g + 1) * d]
        vp_ref[g, new_rows, d:] = jnp.where(lane == 0, 1.0, 0.0).astype(BF16)

    qp = [jnp.concatenate([q_ref[0, :, g * d:(g + 1) * d], aq_ref[0, g]], axis=1) for g in range(n_heads)]
    m_ref[...] = jnp.full_like(m_ref, NEG_BIG)
    acc_ref[...] = jnp.zeros_like(acc_ref)

    def step(j, masked):
        start = pl.multiple_of(j * tk, tk)
        for g in range(n_heads):
            s = lax.dot_general(qp[g], kp_ref[g, pl.ds(start, tk), :], _NT, preferred_element_type=F32)
            if masked:
                row = lax.broadcasted_iota(jnp.int32, (tq, tk), 0)
                col = lax.broadcasted_iota(jnp.int32, (tq, tk), 1)
                s = jnp.where(row >= col, s, NEG_BIG)
            m_prev = m_ref[g]
            m_new = jnp.maximum(m_prev, jnp.max(s, axis=-1, keepdims=True))
            p = jnp.exp2(s - jnp.concatenate([m_new] * (tk // V7X_LANES), axis=1)).astype(BF16)
            alpha = jnp.exp2(m_prev - m_new)
            acc_ref[g] = (jnp.concatenate([alpha] * (dp // V7X_LANES), axis=1) * acc_ref[g]
                          + jnp.dot(p, vp_ref[g, pl.ds(start, tk), :], preferred_element_type=F32))
            m_ref[g] = m_new

    unroll = FOX_UNROLL
    shift = unroll.bit_length() - 1
    assert unroll == 1 << shift

    def body(i, carry):
        for u in range(unroll):
            step(unroll * i + u, False)
        return carry

    lax.fori_loop(0, lax.shift_right_logical(qi, shift), body, 0)

    for rem in range(unroll):
        @pl.when((qi & (unroll - 1)) == rem)
        def _():
            for u in range(rem):
                step(qi - rem + u, False)
            step(qi, True)

    for g in range(n_heads):
        acc = acc_ref[g]
        o = acc[:, :d] / acc[:, d:d + 1]
        o = o * lax.rsqrt(jnp.mean(o * o, axis=-1, keepdims=True) + EPS) * g_ref[g]
        o_ref[0, :, g * d:(g + 1) * d] = o.astype(o_ref.dtype)


def _fox(proj, aq, ak, norm_g, *, tq=FOX_TQ, tk=FOX_TK, group=FOX_GROUP):
    _, bsz, seq, gd = proj.shape
    d = FOX_HEAD_DIM
    dp = d + V7X_LANES
    n_groups = FOX_HEADS // group
    assert tq == tk and seq % tq == 0 and FOX_HEADS % group == 0 and gd == group * d
    blk = 4 * _nbytes((tq, gd), BF16) + 2 * _nbytes((group, tq, V7X_LANES), BF16) + _nbytes((group, 8, d), F32)
    scratch = (2 * _nbytes((group, seq, dp), BF16) + _nbytes((group, tq, V7X_LANES), F32)
               + _nbytes((group, tq, dp), F32))
    return pl.pallas_call(
        functools.partial(_fox_kernel, tq=tq, tk=tk),
        grid=(bsz, n_groups, seq // tq),
        in_specs=[pl.BlockSpec((None, 1, tq, gd), lambda b, h, i: (h, b, i, 0)),
                  pl.BlockSpec((None, 1, tk, gd), lambda b, h, i: (n_groups + h, b, i, 0)),
                  pl.BlockSpec((None, 1, tk, gd), lambda b, h, i: (2 * n_groups + h, b, i, 0)),
                  pl.BlockSpec((1, group, tq, V7X_LANES), lambda b, h, i: (b, h, i, 0)),
                  pl.BlockSpec((1, group, tk, V7X_LANES), lambda b, h, i: (b, h, i, 0)),
                  pl.BlockSpec((group, 1, d), lambda b, h, i: (h, 0, 0))],
        out_specs=pl.BlockSpec((1, tq, gd), lambda b, h, i: (b, i, h)),
        out_shape=jax.ShapeDtypeStruct((bsz, seq, MIX_WIDTH), BF16),
        scratch_shapes=[pltpu.VMEM((group, seq, dp), BF16), pltpu.VMEM((group, seq, dp), BF16),
                        pltpu.VMEM((group, tq, V7X_LANES), F32), pltpu.VMEM((group, tq, dp), F32)],
        compiler_params=pltpu.CompilerParams(
            dimension_semantics=("arbitrary", "arbitrary", "arbitrary"),
            vmem_limit_bytes=_vmem_limit(blk, scratch + 8 * group * _nbytes((tq, tk), F32))),
        name="fox_attention",
    )(proj, proj, proj, aq, ak, norm_g)


def _gla_tables(chunk):
    idx = np.arange(chunk)
    masks = []
    half = chunk // 2
    while half >= 1:
        blk = idx // (2 * half)
        second = (idx // half) % 2 == 1
        masks.append(((blk[:, None] == blk[None, :]) & second[:, None] & ~second[None, :]).astype(np.float32))
        half //= 2
    span_chunks = V7X_MXU_DIM // chunk
    tri = np.kron(np.eye(span_chunks, dtype=np.float32), np.tril(np.ones((chunk, chunk), np.float32)))
    return tri, np.stack(masks, axis=0)


def _level_refs(b):
    chunk, width = b.shape
    refs = []
    half = chunk // 2
    while half >= 4:
        pieces = [jnp.broadcast_to(b[s + half - 1:s + half], (2 * half, width)) for s in range(0, chunk, 2 * half)]
        refs.append(pieces[0] if len(pieces) == 1 else jnp.concatenate(pieces, axis=0))
        half //= 2
    prev1 = pltpu.roll(b, 1, 0)
    prev2 = pltpu.roll(b, 2, 0)
    next1 = pltpu.roll(b, chunk - 1, 0)
    pos = lax.broadcasted_iota(jnp.int32, b.shape, 0) & 3
    refs.append(jnp.where(pos == 0, next1, jnp.where(pos == 1, b, jnp.where(pos == 2, prev1, prev2))))
    refs.append(jnp.where((pos & 1) == 0, b, prev1))
    return refs


def _gla_kernel(q_ref, k_ref, v_ref, gg_ref, gates_ref, wup_ref, bgk_ref, gn_ref, tri_ref, mask_ref,
                mixed_hbm_ref, o_ref, state_ref, la_ref, *, rows, chunk, scale):
    del mixed_hbm_ref
    dk, dv = GLA_KEY_DIM, GLA_VAL_DIM

    @pl.when(pl.program_id(1) == 0)
    def _():
        state_ref[...] = jnp.zeros_like(state_ref)

    pre = _dot_split2(gates_ref[0], wup_ref[...]) + bgk_ref[...]
    log_a = _log_sigmoid(pre) * (LOG2_E / GLA_GATE_NORMALIZER)
    span = tri_ref.shape[0]
    width = log_a.shape[1]
    for r in range(0, rows, span):
        parts = jnp.concatenate([p.astype(BF16) for p in _split3(log_a[r:r + span])], axis=1)
        b3 = jnp.dot(tri_ref[...], parts, preferred_element_type=F32)
        la_ref[r:r + span, :] = b3[:, :width] + b3[:, width:2 * width] + b3[:, 2 * width:]

    n_levels = mask_ref.shape[0]
    eye = (lax.broadcasted_iota(jnp.int32, (chunk, chunk), 0)
           == lax.broadcasted_iota(jnp.int32, (chunk, chunk), 1))
    row_dk = lax.broadcasted_iota(jnp.int32, (chunk, dk), 0)
    row_e = lax.broadcasted_iota(jnp.int32, (BF16_SUBLANES, dk), 0)
    ones_cols = jnp.ones((BF16_SUBLANES, V7X_LANES), BF16)

    def head_chunk(r0, h):
        ks = slice(h * dk, (h + 1) * dk)
        vs = slice(h * dv, (h + 1) * dv)
        b = la_ref[pl.ds(r0, chunk), ks]
        kt, kl = divmod(h * dk, q_ref.shape[3])
        vt, vl = divmod(h * dv, v_ref.shape[3])
        q = q_ref[kt, 0, pl.ds(r0, chunk), kl:kl + dk].astype(F32) * scale
        k = k_ref[kt, 0, pl.ds(r0, chunk), kl:kl + dk].astype(F32)
        v = v_ref[vt, 0, pl.ds(r0, chunk), vl:vl + dv]

        state = state_ref[h]
        o = jnp.dot((q * jnp.exp2(b)).astype(BF16), state.astype(BF16), preferred_element_type=F32)

        scores = jnp.zeros((chunk, chunk), F32)
        half = chunk // 2
        for lvl, ref in enumerate(_level_refs(b)):
            u = (jnp.where((row_dk & half) != 0, q, k) * jnp.exp2(-jnp.abs(b - ref))).astype(BF16)
            scores = scores + mask_ref[lvl] * lax.dot_general(u, u, _NT, preferred_element_type=F32)
            half //= 2
        scores = jnp.where(eye, jnp.sum(q * k, axis=-1, keepdims=True), scores)
        o = o + jnp.dot(scores.astype(BF16), v, preferred_element_type=F32)

        b_last = b[chunk - 1:chunk]
        k_dec = (k * jnp.exp2(b_last - b)).astype(BF16)
        e1, e2, e3 = _split3(jnp.broadcast_to(jnp.exp2(b_last), row_e.shape))
        e_rows = jnp.where(row_e == 0, e1, jnp.where(row_e == 1, e2, jnp.where(row_e == 2, e3, 0.0)))
        dec_col = lax.dot_general(e_rows.astype(BF16), ones_cols, _TN, preferred_element_type=F32)
        decay = jnp.concatenate([dec_col] * (dv // V7X_LANES), axis=1)
        state_ref[h] = state * decay + lax.dot_general(k_dec, v, _TN, preferred_element_type=F32)

        o = o * lax.rsqrt(jnp.mean(o * o, axis=-1, keepdims=True) + EPS) * gn_ref[...]
        gate = _silu(gg_ref[vt, 0, pl.ds(r0, chunk), vl:vl + dv].astype(F32))
        o_ref[0, pl.ds(r0, chunk), vs] = (o * gate).astype(o_ref.dtype)

    assert n_levels == chunk.bit_length() - 1

    def chunk_body(c, carry):
        r0 = pl.multiple_of(c * chunk, chunk)
        for h in range(GLA_HEADS):
            head_chunk(r0, h)
        return carry

    lax.fori_loop(0, rows // chunk, chunk_body, 0)


def _gla(proj, col0, gates, w_up_pad, b_gk, norm_g, mixed, *, rows=GLA_ROWS, chunk=GLA_CHUNK):
    _, bsz, seq, tw = proj.shape
    kw, vw = GLA_KEY_WIDTH, GLA_WIDTH
    assert seq % rows == 0 and rows % chunk == 0 and FOX_WIDTH == vw
    assert kw % tw == 0 and col0 % kw == 0 and (col0 + 2 * kw) % vw == 0 and tw % GLA_VAL_DIM == 0
    k_tiles, v_tiles = kw // tw, vw // tw
    q_blk = col0 // kw
    v_blk = (col0 + 2 * kw) // vw
    tri, masks = _gla_tables(chunk)
    blk = (2 * _nbytes((rows, kw), BF16) + 3 * _nbytes((rows, vw), BF16) + _nbytes((rows, V7X_LANES), F32)
           + _nbytes((V7X_LANES, kw), F32) + _nbytes(masks.shape, F32))
    scratch = _nbytes((GLA_HEADS, GLA_KEY_DIM, GLA_VAL_DIM), F32) + _nbytes((rows, kw), F32)
    return pl.pallas_call(
        functools.partial(_gla_kernel, rows=rows, chunk=chunk, scale=GLA_KEY_DIM ** -0.5),
        grid=(bsz, seq // rows),
        in_specs=[pl.BlockSpec((k_tiles, 1, rows, tw), lambda b, t: (q_blk, b, t, 0)),
                  pl.BlockSpec((k_tiles, 1, rows, tw), lambda b, t: (q_blk + 1, b, t, 0)),
                  pl.BlockSpec((v_tiles, 1, rows, tw), lambda b, t: (v_blk, b, t, 0)),
                  pl.BlockSpec((v_tiles, 1, rows, tw), lambda b, t: (v_blk + 1, b, t, 0)),
                  pl.BlockSpec((1, rows, V7X_LANES), lambda b, t: (b, t, 0)),
                  pl.BlockSpec((V7X_LANES, kw), lambda b, t: (0, 0)),
                  pl.BlockSpec((1, kw), lambda b, t: (0, 0)),
                  pl.BlockSpec((1, GLA_VAL_DIM), lambda b, t: (0, 0)),
                  pl.BlockSpec(tri.shape, lambda b, t: (0, 0)),
                  pl.BlockSpec(masks.shape, lambda b, t: (0, 0, 0)),
                  pl.BlockSpec(memory_space=pl.ANY)],
        out_specs=pl.BlockSpec((1, rows, vw), lambda b, t: (b, t, 1)),
        out_shape=jax.ShapeDtypeStruct(mixed.shape, mixed.dtype),
        scratch_shapes=[pltpu.VMEM((GLA_HEADS, GLA_KEY_DIM, GLA_VAL_DIM), F32), pltpu.VMEM((rows, kw), F32)],
        input_output_aliases={10: 0},
        compiler_params=pltpu.CompilerParams(
            dimension_semantics=("arbitrary", "arbitrary"),
            vmem_limit_bytes=_vmem_limit(blk, scratch + (8 << 20))),
        name="gla",
    )(proj, proj, proj, proj, gates, w_up_pad, b_gk.reshape(1, -1), norm_g.reshape(1, -1),
      jnp.asarray(tri, BF16), jnp.asarray(masks), mixed)


def _prep_layer(b_f, w_gk_up):
    gate_bias = jnp.concatenate([b_f, jnp.zeros((V7X_LANES - FOX_HEADS,), F32)]).reshape(1, V7X_LANES)
    w_up_pad = jnp.zeros((V7X_LANES, GLA_KEY_WIDTH), F32).at[FOX_HEADS:FOX_HEADS + GLA_GATE_RANK].set(w_gk_up)
    return gate_bias, w_up_pad


def kernel(x, ln_in_g, ln_in_b, w_in, b_f, w_gk_up, b_gk, fox_norm_g, gla_norm_g, w_out, ln1_g, ln1_b,
           w_gate, w_up, w_down, ln2_g, ln2_b):
    bsz, seq, d_model = x.shape
    depth = w_in.shape[0]
    alpha = (2 * depth) ** 0.25
    rows = bsz * seq

    w_in_t = jnp.swapaxes(w_in, 1, 2)
    n_fox_tiles = IN_F_LO // MM_TN
    n_proj_tiles = n_fox_tiles + (IN_LOW_LO - IN_G_LO) // MM_TN

    def proj_row_start(j):
        return pl.multiple_of(j * MM_TN + jnp.where(j >= n_fox_tiles, IN_G_LO - IN_F_LO, 0), F32_SUBLANES)

    t = x.reshape(rows, d_model)
    ln_g, ln_b = ln_in_g, ln_in_b
    x16, mu, rstd = _layer_norm(t, ln_g, ln_b)
    for l in range(depth):
        gate_bias, w_up_pad = _prep_layer(b_f[l], w_gk_up[l])

        proj = _matmul_wt(x16, w_in_t, l, n_proj_tiles, proj_row_start, BF16, scaled_cols=FOX_WIDTH,
                          scale=FOX_HEAD_DIM ** -0.5 * LOG2_E, out_split=MM_TN // PROJ_TILE)
        proj = proj.reshape(-1, bsz, seq, PROJ_TILE)
        gates, aug_q, aug_k = _gates(x16, w_in_t, l, gate_bias, bsz, seq)
        gates = gates.reshape(bsz, seq, V7X_LANES)

        mixed = _fox(proj, aug_q, aug_k, fox_norm_g[l].reshape(FOX_HEADS, 1, FOX_HEAD_DIM))
        mixed = _gla(proj, IN_F_LO, gates, w_up_pad, b_gk[l], gla_norm_g[l], mixed)

        t = _matmul(mixed.reshape(rows, MIX_WIDTH), w_out, l, d_model, (t, mu, rstd, ln_g, ln_b), alpha)
        ln_g, ln_b = ln1_g[l], ln1_b[l]
        x16, mu, rstd = _layer_norm(t, ln_g, ln_b)

        hidden = _gate_up(x16, w_gate, w_up, l)
        t = _down(hidden, w_down, l, (t, mu, rstd, ln_g, ln_b), alpha)
        ln_g, ln_b = ln2_g[l], ln2_b[l]
        if l < depth - 1:
            x16, mu, rstd = _layer_norm(t, ln_g, ln_b)
    out, = _layer_norm(t, ln_g, ln_b, final=True)
    return out.reshape(bsz, seq, d_model)
```

```python
import functools

import numpy as np
import jax
import jax.numpy as jnp
from jax import lax
from jax.experimental import pallas as pl
from jax.experimental.pallas import tpu as pltpu

F32 = jnp.float32
BF16 = jnp.bfloat16

V7X_VMEM_BYTES = 64 * 2**20
V7X_LANES = 128
V7X_MXU_DIM = 256
F32_SUBLANES = 8
BF16_SUBLANES = 16

FOX_HEADS = 16
FOX_HEAD_DIM = 128
FOX_WIDTH = FOX_HEADS * FOX_HEAD_DIM
GLA_HEADS = 4
GLA_KEY_DIM = 256
GLA_VAL_DIM = 512
GLA_KEY_WIDTH = GLA_HEADS * GLA_KEY_DIM
GLA_WIDTH = GLA_HEADS * GLA_VAL_DIM
GLA_GATE_RANK = 16
GLA_GATE_NORMALIZER = 16.0
MIX_WIDTH = FOX_WIDTH + GLA_WIDTH
EPS = 1e-5
IN_F_LO = 3 * FOX_WIDTH
IN_G_LO = IN_F_LO + FOX_HEADS
IN_LOW_LO = IN_G_LO + 2 * GLA_KEY_WIDTH + 2 * GLA_WIDTH

LN_ROWS = 512
MM_TM = 1024
MM_TN = 1024
OUT_TN = 512
FFN_TN = 512
DOWN_TM = 512
DOWN_TN = 512
GATE_ROWS = 512
FOX_TQ = 512
FOX_TK = 512
FOX_GROUP = 4
FOX_UNROLL = 4
PROJ_TILE = FOX_GROUP * FOX_HEAD_DIM
GLA_ROWS = 512
GLA_CHUNK = 64
NEG_BIG = -1e30
LOG2_E = 1.4426950408889634

_NT = (((1,), (1,)), ((), ()))
_TN = (((0,), (0,)), ((), ()))


def _vmem_limit(block_bytes, extra_bytes):
    need = 2 * block_bytes + extra_bytes + (4 << 20)
    return int(min(need, V7X_VMEM_BYTES - (6 << 20)))


def _nbytes(shape, dtype):
    return int(np.prod(shape)) * jnp.dtype(dtype).itemsize


def _log_sigmoid(x):
    return jnp.minimum(x, 0.0) - jnp.log1p(jnp.exp(-jnp.abs(x)))


def _silu(x):
    return x * (1.0 / (1.0 + jnp.exp(-x)))


def _ln_stats(t):
    mu = jnp.mean(t, axis=-1, keepdims=True)
    tc = t - mu
    return mu, lax.rsqrt(jnp.mean(tc * tc, axis=-1, keepdims=True) + EPS)


def _ln_apply(t, mu, rstd, g, b):
    return (t - mu) * rstd * g + b


def _ln_kernel(x_ref, g_ref, b_ref, *outs, final):
    t = x_ref[...]
    mu, rstd = _ln_stats(t)
    y = _ln_apply(t, mu, rstd, g_ref[...], b_ref[...])
    if final:
        outs[0][...] = y
    else:
        outs[0][...] = y.astype(BF16)
        outs[1][...] = jnp.broadcast_to(mu, outs[1].shape)
        outs[2][...] = jnp.broadcast_to(rstd, outs[2].shape)


def _layer_norm(t, g, b, *, final=False):
    rows, d = t.shape
    assert rows % LN_ROWS == 0
    row_spec = pl.BlockSpec((LN_ROWS, d), lambda i: (i, 0))
    vec_spec = pl.BlockSpec((1, d), lambda i: (0, 0))
    stat_spec = pl.BlockSpec((LN_ROWS, V7X_LANES), lambda i: (i, 0))
    if final:
        out_shape, out_specs = [jax.ShapeDtypeStruct((rows, d), F32)], [row_spec]
    else:
        stat_shape = jax.ShapeDtypeStruct((rows, V7X_LANES), F32)
        out_shape, out_specs = [jax.ShapeDtypeStruct((rows, d), BF16), stat_shape, stat_shape], [row_spec, stat_spec, stat_spec]
    blk = 2 * _nbytes((LN_ROWS, d), F32) + 2 * _nbytes((LN_ROWS, V7X_LANES), F32)
    return pl.pallas_call(
        functools.partial(_ln_kernel, final=final),
        grid=(rows // LN_ROWS,),
        in_specs=[row_spec, vec_spec, vec_spec], out_specs=out_specs, out_shape=out_shape,
        compiler_params=pltpu.CompilerParams(
            dimension_semantics=("arbitrary",),
            vmem_limit_bytes=_vmem_limit(blk, 4 * _nbytes((LN_ROWS, d), F32))),
        name="layer_norm",
    )(t, g.reshape(1, d), b.reshape(1, d))


def _ws_kernel(x_ref, *rest, n_w, transposed, epilogue, n_res):
    w_refs = rest[:n_w]
    res_refs = rest[n_w:n_w + n_res]
    o_ref, wb_ref = rest[n_w + n_res:]
    j, i = pl.program_id(0), pl.program_id(1)
    tn = wb_ref.shape[2] // n_w
    ck = w_refs[0].shape[-1] if transposed else w_refs[0].shape[0]

    def convert_chunk(slot):
        rows = pl.ds(pl.multiple_of(i * ck, ck), ck)
        for t, w_ref in enumerate(w_refs):
            chunk = w_ref[0].T if transposed else w_ref[...]
            wb_ref[slot, rows, t * tn:(t + 1) * tn] = chunk.astype(BF16)

    @pl.when(j == 0)
    def _():
        convert_chunk(0)

    for parity in (0, 1):
        @pl.when(jnp.logical_and(j > 0, (j & 1) == parity))
        def _():
            convert_chunk(parity)
            acc = jnp.dot(x_ref[...], wb_ref[1 - parity], preferred_element_type=F32)
            out = epilogue(acc, j - 1, [r[...] for r in res_refs]).astype(o_ref.dtype)
            if len(o_ref.shape) == 2:
                o_ref[...] = out
            else:
                width = o_ref.shape[2]
                for s in range(o_ref.shape[0]):
                    o_ref[s] = out[:, s * width:(s + 1) * width]


def _ws_matmul(x, weights, w_specs, n_tiles, tn_w, tn_out, out_dtype, epilogue, *, transposed, name, tm=MM_TM,
               res=None, n_out=None, out_split=None):
    m, kdim = x.shape
    n_m = m // tm
    ck = kdim // n_m
    assert m % tm == 0 and kdim % n_m == 0 and ck % (V7X_LANES if transposed else BF16_SUBLANES) == 0
    n_w = len(weights)

    def row_tile(j, i):
        return jnp.where(j == 0, 0, i)

    def col_tile(j, i):
        return jnp.maximum(j - 1, 0)

    if out_split is None:
        out_spec = pl.BlockSpec((tm, tn_out), lambda j, i: (row_tile(j, i), col_tile(j, i)))
        out_shape = jax.ShapeDtypeStruct((m, n_tiles * tn_out if n_out is None else n_out), out_dtype)
    else:
        assert res is None and n_out is None and tn_out % out_split == 0
        out_spec = pl.BlockSpec((out_split, tm, tn_out // out_split), lambda j, i: (col_tile(j, i), row_tile(j, i), 0))
        out_shape = jax.ShapeDtypeStruct((n_tiles * out_split, m, tn_out // out_split), out_dtype)
    res_arrays, res_specs, res_bytes = [], [], 0
    if res is not None:
        t, mu, rstd, g, b = res
        stat_spec = pl.BlockSpec((tm, V7X_LANES), lambda j, i: (row_tile(j, i), 0))
        vec_spec = pl.BlockSpec((1, tn_out), lambda j, i: (0, col_tile(j, i)))
        res_arrays = [t, mu, rstd, g.reshape(1, -1), b.reshape(1, -1)]
        res_specs = [out_spec, stat_spec, stat_spec, vec_spec, vec_spec]
        res_bytes = _nbytes((tm, tn_out), F32) + 2 * _nbytes((tm, V7X_LANES), F32)
    blk = (_nbytes((tm, kdim), x.dtype) + n_w * _nbytes((ck, tn_w), F32) + _nbytes((tm, tn_out), out_dtype)
           + res_bytes)
    scratch = _nbytes((2, kdim, n_w * tn_w), BF16)
    extra = scratch + 2 * _nbytes((tm, n_w * tn_w), F32) + 3 * n_w * _nbytes((ck, tn_w), F32)
    return pl.pallas_call(
        functools.partial(_ws_kernel, n_w=n_w, transposed=transposed, epilogue=epilogue, n_res=len(res_arrays)),
        grid=(n_tiles + 1, n_m),
        in_specs=[pl.BlockSpec((tm, kdim), lambda j, i: (row_tile(j, i), 0))] + w_specs(ck) + res_specs,
        out_specs=out_spec,
        out_shape=out_shape,
        scratch_shapes=[pltpu.VMEM((2, kdim, n_w * tn_w), BF16)],
        compiler_params=pltpu.CompilerParams(
            dimension_semantics=("arbitrary", "arbitrary"),
            vmem_limit_bytes=_vmem_limit(blk, extra)),
        name=name,
    )(x, *weights, *res_arrays)


def _residual_epilogue(alpha):
    def epilogue(acc, tile, res):
        t, mu, rstd, g, b = res
        reps = t.shape[1] // V7X_LANES
        x = _ln_apply(t, jnp.concatenate([mu] * reps, axis=1), jnp.concatenate([rstd] * reps, axis=1), g, b)
        return alpha * x + acc
    return epilogue


def _matmul(x, w, layer, n_cols, res, alpha, *, tn=OUT_TN):
    assert n_cols % tn == 0 and w.shape[1] == x.shape[1]
    n_tiles = n_cols // tn

    def w_specs(ck):
        return [pl.BlockSpec((None, ck, tn), lambda j, i: (layer, i, jnp.minimum(j, n_tiles - 1)))]

    return _ws_matmul(x, [w], w_specs, n_tiles, tn, tn, F32, _residual_epilogue(alpha),
                      transposed=False, name="matmul", res=res)


def _matmul_wt(x, w_t, layer, n_tiles, row_start, out_dtype, *, tn=MM_TN, scaled_cols=0, scale=1.0, out_split=None):
    assert w_t.shape[2] == x.shape[1] and scaled_cols % tn == 0
    scaled_tiles = scaled_cols // tn

    def w_specs(ck):
        return [pl.BlockSpec((pl.Element(1), pl.Element(tn), pl.Element(ck)),
                             lambda j, i: (layer, row_start(jnp.minimum(j, n_tiles - 1)), i * ck))]

    def epilogue(acc, tile, res):
        return acc * jnp.where(tile < scaled_tiles, scale, 1.0) if scaled_tiles else acc

    return _ws_matmul(x, [w_t], w_specs, n_tiles, tn, tn, out_dtype, epilogue, transposed=True, name="matmul_wt",
                      out_split=out_split)


def _gate_up(x, w_gate, w_up, layer, *, tn=FFN_TN):
    d_ff = w_gate.shape[2]
    assert w_gate.shape[1] == x.shape[1]
    n_tiles = pl.cdiv(d_ff, tn)

    def w_specs(ck):
        return [pl.BlockSpec((None, ck, tn), lambda j, i: (layer, i, jnp.minimum(j, n_tiles - 1)))] * 2

    def epilogue(z, tile, res):
        return _silu(z[:, :tn]) * z[:, tn:]

    return _ws_matmul(x, [w_gate, w_up], w_specs, n_tiles, tn, tn, BF16, epilogue,
                      transposed=False, name="gate_up", n_out=d_ff)


def _down(h, w_down, layer, res, alpha, *, tm=DOWN_TM, tn=DOWN_TN):
    d_ff, n = w_down.shape[1:]
    assert n % tn == 0 and h.shape[1] == d_ff
    n_tiles = n // tn

    def w_specs(ck):
        return [pl.BlockSpec((None, ck, tn), lambda j, i: (layer, i, jnp.minimum(j, n_tiles - 1)))]

    return _ws_matmul(h, [w_down], w_specs, n_tiles, tn, tn, F32, _residual_epilogue(alpha),
                      transposed=False, name="down_proj", tm=tm, res=res)


def _split3(x):
    a = x.astype(BF16).astype(F32)
    r = x - a
    b = r.astype(BF16).astype(F32)
    return a, b, (r - b).astype(BF16).astype(F32)


def _dot_split2(a, b):
    a_hi = a.astype(BF16)
    b_hi = b.astype(BF16)
    a_lo = (a - a_hi.astype(F32)).astype(BF16)
    b_lo = (b - b_hi.astype(F32)).astype(BF16)
    return (jnp.dot(a_hi, b_hi, preferred_element_type=F32) + jnp.dot(a_hi, b_lo, preferred_element_type=F32)
            + jnp.dot(a_lo, b_hi, preferred_element_type=F32))


def _gates_kernel(x_ref, wf_ref, wl_ref, bias_ref, tri_ref, o_ref, aq_ref, ak_ref, carry_ref):
    @pl.when(pl.program_id(1) == 0)
    def _():
        carry_ref[...] = jnp.zeros_like(carry_ref)

    wf, wl = wf_ref[0], wl_ref[0]
    pad = jnp.zeros((V7X_LANES - wf.shape[0] - wl.shape[0], wf.shape[1]), F32)
    w = jnp.concatenate([wf, wl, pad], axis=0).astype(BF16)
    z = lax.dot_general(x_ref[...], w, _NT, preferred_element_type=F32)
    log_f = _log_sigmoid(z + bias_ref[...])
    parts = jnp.concatenate([p.astype(BF16) for p in _split3(log_f)], axis=1)
    c3 = jnp.dot(tri_ref[...], parts, preferred_element_type=F32)
    nl = log_f.shape[1]
    c = c3[:, :nl] + c3[:, nl:2 * nl] + c3[:, 2 * nl:] + carry_ref[0:1, :]
    rows = c.shape[0]
    carry_ref[...] = jnp.broadcast_to(c[rows - 1:rows, :], carry_ref.shape)
    lane = lax.broadcasted_iota(jnp.int32, c.shape, 1)
    o_ref[...] = jnp.where(lane < FOX_HEADS, c, z)

    c2 = c * LOG2_E
    for h in range(FOX_HEADS):
        c1, c2_, c3 = _split3(jnp.broadcast_to(c2[:, h:h + 1], c.shape))
        aq = jnp.where(lane == 0, c1, jnp.where(lane == 1, c2_, jnp.where(lane == 2, c3,
                       jnp.where(lane < 6, 1.0, 0.0))))
        ak = jnp.where(lane < 3, 1.0, jnp.where(lane == 3, -c1, jnp.where(lane == 4, -c2_,
                       jnp.where(lane == 5, -c3, 0.0))))
        aq_ref[0, h] = aq.astype(BF16)
        ak_ref[0, h] = ak.astype(BF16)


def _gates(x, w_in_t, layer, bias, bsz, seq):
    m, kdim = x.shape
    steps = seq // GATE_ROWS
    tri = jnp.asarray(np.tril(np.ones((GATE_ROWS, GATE_ROWS), np.float32)), BF16)
    aug_shape = jax.ShapeDtypeStruct((bsz, FOX_HEADS, seq, V7X_LANES), BF16)
    aug_spec = pl.BlockSpec((1, FOX_HEADS, GATE_ROWS, V7X_LANES), lambda b, i: (b, 0, i, 0))
    assert IN_F_LO % F32_SUBLANES == 0 and IN_LOW_LO % F32_SUBLANES == 0

    def rows_spec(n_rows, row0):
        return pl.BlockSpec((pl.Element(1), pl.Element(n_rows), pl.Element(kdim)), lambda b, i: (layer, row0, 0))

    blk = (_nbytes((GATE_ROWS, kdim), x.dtype) + _nbytes((V7X_LANES, kdim), F32)
           + _nbytes((GATE_ROWS, GATE_ROWS), F32) + _nbytes((GATE_ROWS, V7X_LANES), F32)
           + 2 * _nbytes((FOX_HEADS, GATE_ROWS, V7X_LANES), BF16))
    return pl.pallas_call(
        _gates_kernel,
        grid=(bsz, steps),
        in_specs=[pl.BlockSpec((GATE_ROWS, kdim), lambda b, i: (b * steps + i, 0)),
                  rows_spec(FOX_HEADS, IN_F_LO), rows_spec(GLA_GATE_RANK, IN_LOW_LO),
                  pl.BlockSpec((1, V7X_LANES), lambda b, i: (0, 0)),
                  pl.BlockSpec((GATE_ROWS, GATE_ROWS), lambda b, i: (0, 0))],
        out_specs=[pl.BlockSpec((GATE_ROWS, V7X_LANES), lambda b, i: (b * steps + i, 0)), aug_spec, aug_spec],
        out_shape=[jax.ShapeDtypeStruct((m, V7X_LANES), F32), aug_shape, aug_shape],
        scratch_shapes=[pltpu.VMEM((8, V7X_LANES), F32)],
        compiler_params=pltpu.CompilerParams(
            dimension_semantics=("arbitrary", "arbitrary"),
            vmem_limit_bytes=_vmem_limit(blk, 16 * _nbytes((GATE_ROWS, V7X_LANES), F32))),
        name="gates",
    )(x, w_in_t, w_in_t, bias, tri)


def _fox_kernel(q_ref, k_ref, v_ref, aq_ref, ak_ref, g_ref, o_ref, kp_ref, vp_ref, m_ref, acc_ref, *, tq, tk):
    qi = pl.program_id(2)
    n_heads = aq_ref.shape[1]
    d = q_ref.shape[2] // n_heads
    dp = acc_ref.shape[2]

    new_rows = pl.ds(pl.multiple_of(qi * tk, tk), tk)
    lane = lax.broadcasted_iota(jnp.int32, (tk, V7X_LANES), 1)
    for g in range(n_heads):
        kp_ref[g, new_rows, :d] = k_ref[0, :, g * d:(g + 1) * d]
        kp_ref[g, new_rows, d:] = ak_ref[0, g]
        vp_ref[g, new_rows, :d] = v_ref[0, :, g * d:(g + 1) * d]
        vp_ref[g, new_rows, d:] = jnp.where(lane == 0, 1.0, 0.0).astype(BF16)

    qp = [jnp.concatenate([q_ref[0, :, g * d:(g + 1) * d], aq_ref[0, g]], axis=1) for g in range(n_heads)]
    m_ref[...] = jnp.full_like(m_ref, NEG_BIG)
    acc_ref[...] = jnp.zeros_like(acc_ref)

    def step(j, masked):
        start = pl.multiple_of(j * tk, tk)
        for g in range(n_heads):
            s = lax.dot_general(qp[g], kp_ref[g, pl.ds(start, tk), :], _NT, preferred_element_type=F32)
            if masked:
                row = lax.broadcasted_iota(jnp.int32, (tq, tk), 0)
                col = lax.broadcasted_iota(jnp.int32, (tq, tk), 1)
                s = jnp.where(row >= col, s, NEG_BIG)
            m_prev = m_ref[g]
            m_new = jnp.maximum(m_prev, jnp.max(s, axis=-1, keepdims=True))
            p = jnp.exp2(s - jnp.concatenate([m_new] * (tk // V7X_LANES), axis=1)).astype(BF16)
            alpha = jnp.exp2(m_prev - m_new)
            acc_ref[g] = (jnp.concatenate([alpha] * (dp // V7X_LANES), axis=1) * acc_ref[g]
                          + jnp.dot(p, vp_ref[g, pl.ds(start, tk), :], preferred_element_type=F32))
            m_ref[g] = m_new

    unroll = FOX_UNROLL
    shift = unroll.bit_length() - 1
    assert unroll == 1 << shift

    def body(i, carry):
        for u in range(unroll):
            step(unroll * i + u, False)
        return carry

    lax.fori_loop(0, lax.shift_right_logical(qi, shift), body, 0)

    for rem in range(unroll):
        @pl.when((qi & (unroll - 1)) == rem)
        def _():
            for u in range(rem):
                step(qi - rem + u, False)
            step(qi, True)

    for g in range(n_heads):
        acc = acc_ref[g]
        o = acc[:, :d] / acc[:, d:d + 1]
        o = o * lax.rsqrt(jnp.mean(o * o, axis=-1, keepdims=True) + EPS) * g_ref[g]
        o_ref[0, :, g * d:(g + 1) * d] = o.astype(o_ref.dtype)


def _fox(proj, aq, ak, norm_g, *, tq=FOX_TQ, tk=FOX_TK, group=FOX_GROUP):
    _, bsz, seq, gd = proj.shape
    d = FOX_HEAD_DIM
    dp = d + V7X_LANES
    n_groups = FOX_HEADS // group
    assert tq == tk and seq % tq == 0 and FOX_HEADS % group == 0 and gd == group * d
    blk = 4 * _nbytes((tq, gd), BF16) + 2 * _nbytes((group, tq, V7X_LANES), BF16) + _nbytes((group, 8, d), F32)
    scratch = (2 * _nbytes((group, seq, dp), BF16) + _nbytes((group, tq, V7X_LANES), F32)
               + _nbytes((group, tq, dp), F32))
    return pl.pallas_call(
        functools.partial(_fox_kernel, tq=tq, tk=tk),
        grid=(bsz, n_groups, seq // tq),
        in_specs=[pl.BlockSpec((None, 1, tq, gd), lambda b, h, i: (h, b, i, 0)),
                  pl.BlockSpec((None, 1, tk, gd), lambda b, h, i: (n_groups + h, b, i, 0)),
                  pl.BlockSpec((None, 1, tk, gd), lambda b, h, i: (2 * n_groups + h, b, i, 0)),
                  pl.BlockSpec((1, group, tq, V7X_LANES), lambda b, h, i: (b, h, i, 0)),
                  pl.BlockSpec((1, group, tk, V7X_LANES), lambda b, h, i: (b, h, i, 0)),
                  pl.BlockSpec((group, 1, d), lambda b, h, i: (h, 0, 0))],
        out_specs=pl.BlockSpec((1, tq, gd), lambda b, h, i: (b, i, h)),
        out_shape=jax.ShapeDtypeStruct((bsz, seq, MIX_WIDTH), BF16),
        scratch_shapes=[pltpu.VMEM((group, seq, dp), BF16), pltpu.VMEM((group, seq, dp), BF16),
                        pltpu.VMEM((group, tq, V7X_LANES), F32), pltpu.VMEM((group, tq, dp), F32)],
        compiler_params=pltpu.CompilerParams(
            dimension_semantics=("arbitrary", "arbitrary", "arbitrary"),
            vmem_limit_bytes=_vmem_limit(blk, scratch + 8 * group * _nbytes((tq, tk), F32))),
        name="fox_attention",
    )(proj, proj, proj, aq, ak, norm_g)


def _gla_tables(chunk):
    idx = np.arange(chunk)
    masks = []
    half = chunk // 2
    while half >= 1:
        blk = idx // (2 * half)
        second = (idx // half) % 2 == 1
        masks.append(((blk[:, None] == blk[None, :]) & second[:, None] & ~second[None, :]).astype(np.float32))
        half //= 2
    span_chunks = V7X_MXU_DIM // chunk
    tri = np.kron(np.eye(span_chunks, dtype=np.float32), np.tril(np.ones((chunk, chunk), np.float32)))
    return tri, np.stack(masks, axis=0)


def _level_refs(b):
    chunk, width = b.shape
    refs = []
    half = chunk // 2
    while half >= 4:
        pieces = [jnp.broadcast_to(b[s + half - 1:s + half], (2 * half, width)) for s in range(0, chunk, 2 * half)]
        refs.append(pieces[0] if len(pieces) == 1 else jnp.concatenate(pieces, axis=0))
        half //= 2
    prev1 = pltpu.roll(b, 1, 0)
    prev2 = pltpu.roll(b, 2, 0)
    next1 = pltpu.roll(b, chunk - 1, 0)
    pos = lax.broadcasted_iota(jnp.int32, b.shape, 0) & 3
    refs.append(jnp.where(pos == 0, next1, jnp.where(pos == 1, b, jnp.where(pos == 2, prev1, prev2))))
    refs.append(jnp.where((pos & 1) == 0, b, prev1))
    return refs


def _gla_kernel(q_ref, k_ref, v_ref, gg_ref, gates_ref, wup_ref, bgk_ref, gn_ref, tri_ref, mask_ref,
                mixed_hbm_ref, o_ref, state_ref, la_ref, *, rows, chunk, scale):
    del mixed_hbm_ref
    dk, dv = GLA_KEY_DIM, GLA_VAL_DIM

    @pl.when(pl.program_id(1) == 0)
    def _():
        state_ref[...] = jnp.zeros_like(state_ref)

    pre = _dot_split2(gates_ref[0], wup_ref[...]) + bgk_ref[...]
    log_a = _log_sigmoid(pre) * (LOG2_E / GLA_GATE_NORMALIZER)
    span = tri_ref.shape[0]
    width = log_a.shape[1]
    for r in range(0, rows, span):
        parts = jnp.concatenate([p.astype(BF16) for p in _split3(log_a[r:r + span])], axis=1)
        b3 = jnp.dot(tri_ref[...], parts, preferred_element_type=F32)
        la_ref[r:r + span, :] = b3[:, :width] + b3[:, width:2 * width] + b3[:, 2 * width:]

    n_levels = mask_ref.shape[0]
    eye = (lax.broadcasted_iota(jnp.int32, (chunk, chunk), 0)
           == lax.broadcasted_iota(jnp.int32, (chunk, chunk), 1))
    row_dk = lax.broadcasted_iota(jnp.int32, (chunk, dk), 0)
    row_e = lax.broadcasted_iota(jnp.int32, (BF16_SUBLANES, dk), 0)
    ones_cols = jnp.ones((BF16_SUBLANES, V7X_LANES), BF16)

    def head_chunk(r0, h):
        ks = slice(h * dk, (h + 1) * dk)
        vs = slice(h * dv, (h + 1) * dv)
        b = la_ref[pl.ds(r0, chunk), ks]
        kt, kl = divmod(h * dk, q_ref.shape[3])
        vt, vl = divmod(h * dv, v_ref.shape[3])
        q = q_ref[kt, 0, pl.ds(r0, chunk), kl:kl + dk].astype(F32) * scale
        k = k_ref[kt, 0, pl.ds(r0, chunk), kl:kl + dk].astype(F32)
        v = v_ref[vt, 0, pl.ds(r0, chunk), vl:vl + dv]

        state = state_ref[h]
        o = jnp.dot((q * jnp.exp2(b)).astype(BF16), state.astype(BF16), preferred_element_type=F32)

        scores = jnp.zeros((chunk, chunk), F32)
        half = chunk // 2
        for lvl, ref in enumerate(_level_refs(b)):
            u = (jnp.where((row_dk & half) != 0, q, k) * jnp.exp2(-jnp.abs(b - ref))).astype(BF16)
            scores = scores + mask_ref[lvl] * lax.dot_general(u, u, _NT, preferred_element_type=F32)
            half //= 2
        scores = jnp.where(eye, jnp.sum(q * k, axis=-1, keepdims=True), scores)
        o = o + jnp.dot(scores.astype(BF16), v, preferred_element_type=F32)

        b_last = b[chunk - 1:chunk]
        k_dec = (k * jnp.exp2(b_last - b)).astype(BF16)
        e1, e2, e3 = _split3(jnp.broadcast_to(jnp.exp2(b_last), row_e.shape))
        e_rows = jnp.where(row_e == 0, e1, jnp.where(row_e == 1, e2, jnp.where(row_e == 2, e3, 0.0)))
        dec_col = lax.dot_general(e_rows.astype(BF16), ones_cols, _TN, preferred_element_type=F32)
        decay = jnp.concatenate([dec_col] * (dv // V7X_LANES), axis=1)
        state_ref[h] = state * decay + lax.dot_general(k_dec, v, _TN, preferred_element_type=F32)

        o = o * lax.rsqrt(jnp.mean(o * o, axis=-1, keepdims=True) + EPS) * gn_ref[...]
        gate = _silu(gg_ref[vt, 0, pl.ds(r0, chunk), vl:vl + dv].astype(F32))
        o_ref[0, pl.ds(r0, chunk), vs] = (o * gate).astype(o_ref.dtype)

    assert n_levels == chunk.bit_length() - 1

    def chunk_body(c, carry):
        r0 = pl.multiple_of(c * chunk, chunk)
        for h in range(GLA_HEADS):
            head_chunk(r0, h)
        return carry

    lax.fori_loop(0, rows // chunk, chunk_body, 0)


def _gla(proj, col0, gates, w_up_pad, b_gk, norm_g, mixed, *, rows=GLA_ROWS, chunk=GLA_CHUNK):
    _, bsz, seq, tw = proj.shape
    kw, vw = GLA_KEY_WIDTH, GLA_WIDTH
    assert seq % rows == 0 and rows % chunk == 0 and FOX_WIDTH == vw
    assert kw % tw == 0 and col0 % kw == 0 and (col0 + 2 * kw) % vw == 0 and tw % GLA_VAL_DIM == 0
    k_tiles, v_tiles = kw // tw, vw // tw
    q_blk = col0 // kw
    v_blk = (col0 + 2 * kw) // vw
    tri, masks = _gla_tables(chunk)
    blk = (2 * _nbytes((rows, kw), BF16) + 3 * _nbytes((rows, vw), BF16) + _nbytes((rows, V7X_LANES), F32)
           + _nbytes((V7X_LANES, kw), F32) + _nbytes(masks.shape, F32))
    scratch = _nbytes((GLA_HEADS, GLA_KEY_DIM, GLA_VAL_DIM), F32) + _nbytes((rows, kw), F32)
    return pl.pallas_call(
        functools.partial(_gla_kernel, rows=rows, chunk=chunk, scale=GLA_KEY_DIM ** -0.5),
        grid=(bsz, seq // rows),
        in_specs=[pl.BlockSpec((k_tiles, 1, rows, tw), lambda b, t: (q_blk, b, t, 0)),
                  pl.BlockSpec((k_tiles, 1, rows, tw), lambda b, t: (q_blk + 1, b, t, 0)),
                  pl.BlockSpec((v_tiles, 1, rows, tw), lambda b, t: (v_blk, b, t, 0)),
                  pl.BlockSpec((v_tiles, 1, rows, tw), lambda b, t: (v_blk + 1, b, t, 0)),
                  pl.BlockSpec((1, rows, V7X_LANES), lambda b, t: (b, t, 0)),
                  pl.BlockSpec((V7X_LANES, kw), lambda b, t: (0, 0)),
                  pl.BlockSpec((1, kw), lambda b, t: (0, 0)),
                  pl.BlockSpec((1, GLA_VAL_DIM), lambda b, t: (0, 0)),
                  pl.BlockSpec(tri.shape, lambda b, t: (0, 0)),
                  pl.BlockSpec(masks.shape, lambda b, t: (0, 0, 0)),
                  pl.BlockSpec(memory_space=pl.ANY)],
        out_specs=pl.BlockSpec((1, rows, vw), lambda b, t: (b, t, 1)),
        out_shape=jax.ShapeDtypeStruct(mixed.shape, mixed.dtype),
        scratch_shapes=[pltpu.VMEM((GLA_HEADS, GLA_KEY_DIM, GLA_VAL_DIM), F32), pltpu.VMEM((rows, kw), F32)],
        input_output_aliases={10: 0},
        compiler_params=pltpu.CompilerParams(
            dimension_semantics=("arbitrary", "arbitrary"),
            vmem_limit_bytes=_vmem_limit(blk, scratch + (8 << 20))),
        name="gla",
    )(proj, proj, proj, proj, gates, w_up_pad, b_gk.reshape(1, -1), norm_g.reshape(1, -1),
      jnp.asarray(tri, BF16), jnp.asarray(masks), mixed)


def _prep_layer(b_f, w_gk_up):
    gate_bias = jnp.concatenate([b_f, jnp.zeros((V7X_LANES - FOX_HEADS,), F32)]).reshape(1, V7X_LANES)
    w_up_pad = jnp.zeros((V7X_LANES, GLA_KEY_WIDTH), F32).at[FOX_HEADS:FOX_HEADS + GLA_GATE_RANK].set(w_gk_up)
    return gate_bias, w_up_pad


def kernel(x, ln_in_g, ln_in_b, w_in, b_f, w_gk_up, b_gk, fox_norm_g, gla_norm_g, w_out, ln1_g, ln1_b,
           w_gate, w_up, w_down, ln2_g, ln2_b):
    bsz, seq, d_model = x.shape
    depth = w_in.shape[0]
    alpha = (2 * depth) ** 0.25
    rows = bsz * seq

    w_in_t = jnp.swapaxes(w_in, 1, 2)
    n_fox_tiles = IN_F_LO // MM_TN
    n_proj_tiles = n_fox_tiles + (IN_LOW_LO - IN_G_LO) // MM_TN

    def proj_row_start(j):
        return pl.multiple_of(j * MM_TN + jnp.where(j >= n_fox_tiles, IN_G_LO - IN_F_LO, 0), F32_SUBLANES)

    t = x.reshape(rows, d_model)
    ln_g, ln_b = ln_in_g, ln_in_b
    x16, mu, rstd = _layer_norm(t, ln_g, ln_b)
    for l in range(depth):
        gate_bias, w_up_pad = _prep_layer(b_f[l], w_gk_up[l])

        proj = _matmul_wt(x16, w_in_t, l, n_proj_tiles, proj_row_start, BF16, scaled_cols=FOX_WIDTH,
                          scale=FOX_HEAD_DIM ** -0.5 * LOG2_E, out_split=MM_TN // PROJ_TILE)
        proj = proj.reshape(-1, bsz, seq, PROJ_TILE)
        gates, aug_q, aug_k = _gates(x16, w_in_t, l, gate_bias, bsz, seq)
        gates = gates.reshape(bsz, seq, V7X_LANES)

        mixed = _fox(proj, aug_q, aug_k, fox_norm_g[l].reshape(FOX_HEADS, 1, FOX_HEAD_DIM))
        mixed = _gla(proj, IN_F_LO, gates, w_up_pad, b_gk[l], gla_norm_g[l], mixed)

        t = _matmul(mixed.reshape(rows, MIX_WIDTH), w_out, l, d_model, (t, mu, rstd, ln_g, ln_b), alpha)
        ln_g, ln_b = ln1_g[l], ln1_b[l]
        x16, mu, rstd = _layer_norm(t, ln_g, ln_b)

        hidden = _gate_up(x16, w_gate, w_up, l)
        t = _down(hidden, w_down, l, (t, mu, rstd, ln_g, ln_b), alpha)
        ln_g, ln_b = ln2_g[l], ln2_b[l]
        if l < depth - 1:
            x16, mu, rstd = _layer_norm(t, ln_g, ln_b)
    out, = _layer_norm(t, ln_g, ln_b, final=True)
    return out.reshape(bsz, seq, d_model)
```

```python
import functools

import numpy as np
import jax
import jax.numpy as jnp
from jax import lax
from jax.experimental import pallas as pl
from jax.experimental.pallas import tpu as pltpu

F32 = jnp.float32
BF16 = jnp.bfloat16

V7X_VMEM_BYTES = 64 * 2**20
V7X_LANES = 128
V7X_MXU_DIM = 256
F32_SUBLANES = 8
BF16_SUBLANES = 16

FOX_HEADS = 16
FOX_HEAD_DIM = 128
FOX_WIDTH = FOX_HEADS * FOX_HEAD_DIM
GLA_HEADS = 4
GLA_KEY_DIM = 256
GLA_VAL_DIM = 512
GLA_KEY_WIDTH = GLA_HEADS * GLA_KEY_DIM
GLA_WIDTH = GLA_HEADS * GLA_VAL_DIM
GLA_GATE_RANK = 16
GLA_GATE_NORMALIZER = 16.0
EPS = 1e-5
IN_F_LO = 3 * FOX_WIDTH
IN_G_LO = IN_F_LO + FOX_HEADS
IN_LOW_LO = IN_G_LO + 2 * GLA_KEY_WIDTH + 2 * GLA_WIDTH

LN_ROWS = 512
MM_TM = 1024
MM_TN = 1024
OUT_TN = 512
FFN_TN = 512
DOWN_TM = 512
DOWN_TN = 512
GATE_ROWS = 512
FOX_TQ = 512
FOX_TK = 512
FOX_GROUP = 4
FOX_UNROLL = 4
PROJ_TILE = FOX_GROUP * FOX_HEAD_DIM
GLA_ROWS = 512
GLA_CHUNK = 64
NEG_BIG = -1e30
LOG2_E = 1.4426950408889634

_NT = (((1,), (1,)), ((), ()))
_TN = (((0,), (0,)), ((), ()))


def _vmem_limit(block_bytes, extra_bytes):
    need = 2 * block_bytes + extra_bytes + (4 << 20)
    return int(min(need, V7X_VMEM_BYTES - (6 << 20)))


def _nbytes(shape, dtype):
    return int(np.prod(shape)) * jnp.dtype(dtype).itemsize


def _log_sigmoid(x):
    return jnp.minimum(x, 0.0) - jnp.log1p(jnp.exp(-jnp.abs(x)))


def _silu(x):
    return x * (1.0 / (1.0 + jnp.exp(-x)))


def _ln_stats(t):
    mu = jnp.mean(t, axis=-1, keepdims=True)
    tc = t - mu
    return mu, lax.rsqrt(jnp.mean(tc * tc, axis=-1, keepdims=True) + EPS)


def _ln_apply(t, mu, rstd, g, b):
    return (t - mu) * rstd * g + b


def _ln_kernel(x_ref, g_ref, b_ref, *outs, final):
    t = x_ref[...]
    mu, rstd = _ln_stats(t)
    y = _ln_apply(t, mu, rstd, g_ref[...], b_ref[...])
    if final:
        outs[0][...] = y
    else:
        outs[0][...] = y.astype(BF16)
        outs[1][...] = jnp.broadcast_to(mu, outs[1].shape)
        outs[2][...] = jnp.broadcast_to(rstd, outs[2].shape)


def _layer_norm(t, g, b, *, final=False):
    rows, d = t.shape
    assert rows % LN_ROWS == 0
    row_spec = pl.BlockSpec((LN_ROWS, d), lambda i: (i, 0))
    vec_spec = pl.BlockSpec((1, d), lambda i: (0, 0))
    stat_spec = pl.BlockSpec((LN_ROWS, V7X_LANES), lambda i: (i, 0))
    if final:
        out_shape, out_specs = [jax.ShapeDtypeStruct((rows, d), F32)], [row_spec]
    else:
        stat_shape = jax.ShapeDtypeStruct((rows, V7X_LANES), F32)
        out_shape, out_specs = [jax.ShapeDtypeStruct((rows, d), BF16), stat_shape, stat_shape], [row_spec, stat_spec, stat_spec]
    blk = 2 * _nbytes((LN_ROWS, d), F32) + 2 * _nbytes((LN_ROWS, V7X_LANES), F32)
    return pl.pallas_call(
        functools.partial(_ln_kernel, final=final),
        grid=(rows // LN_ROWS,),
        in_specs=[row_spec, vec_spec, vec_spec], out_specs=out_specs, out_shape=out_shape,
        compiler_params=pltpu.CompilerParams(
            dimension_semantics=("arbitrary",),
            vmem_limit_bytes=_vmem_limit(blk, 4 * _nbytes((LN_ROWS, d), F32))),
        name="layer_norm",
    )(t, g.reshape(1, d), b.reshape(1, d))


def _ws_kernel(*refs, n_x, n_w, transposed, epilogue, n_res):
    x_refs, rest = refs[:n_x], refs[n_x:]
    w_refs = rest[:n_w]
    res_refs = rest[n_w:n_w + n_res]
    o_ref, wb_ref = rest[n_w + n_res:]
    j, i = pl.program_id(0), pl.program_id(1)
    tn = wb_ref.shape[2] // n_w
    ck = w_refs[0].shape[-1] if transposed else w_refs[0].shape[0]

    def convert_chunk(slot):
        rows = pl.ds(pl.multiple_of(i * ck, ck), ck)
        for t, w_ref in enumerate(w_refs):
            chunk = w_ref[0].T if transposed else w_ref[...]
            wb_ref[slot, rows, t * tn:(t + 1) * tn] = chunk.astype(BF16)

    @pl.when(j == 0)
    def _():
        convert_chunk(0)

    for parity in (0, 1):
        @pl.when(jnp.logical_and(j > 0, (j & 1) == parity))
        def _():
            convert_chunk(parity)
            acc, k0 = None, 0
            for x_ref in x_refs:
                k1 = k0 + x_ref.shape[1]
                part = jnp.dot(x_ref[...], wb_ref[1 - parity, k0:k1, :], preferred_element_type=F32)
                acc, k0 = (part if acc is None else acc + part), k1
            out = epilogue(acc, j - 1, [r[...] for r in res_refs]).astype(o_ref.dtype)
            if len(o_ref.shape) == 2:
                o_ref[...] = out
            else:
                width = o_ref.shape[2]
                for s in range(o_ref.shape[0]):
                    o_ref[s] = out[:, s * width:(s + 1) * width]


def _ws_matmul(x, weights, w_specs, n_tiles, tn_w, tn_out, out_dtype, epilogue, *, transposed, name, tm=MM_TM,
               res=None, n_out=None, out_split=None):
    xs = list(x) if isinstance(x, (list, tuple)) else [x]
    m = xs[0].shape[0]
    kdim = sum(p.shape[1] for p in xs)
    n_m = m // tm
    ck = kdim // n_m
    assert m % tm == 0 and kdim % n_m == 0 and ck % (V7X_LANES if transposed else BF16_SUBLANES) == 0
    n_w = len(weights)

    def row_tile(j, i):
        return jnp.where(j == 0, 0, i)

    def col_tile(j, i):
        return jnp.maximum(j - 1, 0)

    if out_split is None:
        out_spec = pl.BlockSpec((tm, tn_out), lambda j, i: (row_tile(j, i), col_tile(j, i)))
        out_shape = jax.ShapeDtypeStruct((m, n_tiles * tn_out if n_out is None else n_out), out_dtype)
    else:
        assert res is None and n_out is None and tn_out % out_split == 0
        out_spec = pl.BlockSpec((out_split, tm, tn_out // out_split), lambda j, i: (col_tile(j, i), row_tile(j, i), 0))
        out_shape = jax.ShapeDtypeStruct((n_tiles * out_split, m, tn_out // out_split), out_dtype)
    res_arrays, res_specs, res_bytes = [], [], 0
    if res is not None:
        t, mu, rstd, g, b = res
        stat_spec = pl.BlockSpec((tm, V7X_LANES), lambda j, i: (row_tile(j, i), 0))
        vec_spec = pl.BlockSpec((1, tn_out), lambda j, i: (0, col_tile(j, i)))
        res_arrays = [t, mu, rstd, g.reshape(1, -1), b.reshape(1, -1)]
        res_specs = [out_spec, stat_spec, stat_spec, vec_spec, vec_spec]
        res_bytes = _nbytes((tm, tn_out), F32) + 2 * _nbytes((tm, V7X_LANES), F32)
    blk = (_nbytes((tm, kdim), BF16) + n_w * _nbytes((ck, tn_w), F32) + _nbytes((tm, tn_out), out_dtype)
           + res_bytes)
    scratch = _nbytes((2, kdim, n_w * tn_w), BF16)
    extra = scratch + 2 * _nbytes((tm, n_w * tn_w), F32) + 3 * n_w * _nbytes((ck, tn_w), F32)
    return pl.pallas_call(
        functools.partial(_ws_kernel, n_x=len(xs), n_w=n_w, transposed=transposed, epilogue=epilogue,
                          n_res=len(res_arrays)),
        grid=(n_tiles + 1, n_m),
        in_specs=([pl.BlockSpec((tm, p.shape[1]), lambda j, i: (row_tile(j, i), 0)) for p in xs]
                  + w_specs(ck) + res_specs),
        out_specs=out_spec,
        out_shape=out_shape,
        scratch_shapes=[pltpu.VMEM((2, kdim, n_w * tn_w), BF16)],
        compiler_params=pltpu.CompilerParams(
            dimension_semantics=("arbitrary", "arbitrary"),
            vmem_limit_bytes=_vmem_limit(blk, extra)),
        name=name,
    )(*xs, *weights, *res_arrays)


def _residual_epilogue(alpha):
    def epilogue(acc, tile, res):
        t, mu, rstd, g, b = res
        reps = t.shape[1] // V7X_LANES
        x = _ln_apply(t, jnp.concatenate([mu] * reps, axis=1), jnp.concatenate([rstd] * reps, axis=1), g, b)
        return alpha * x + acc
    return epilogue


def _matmul(x, w, layer, n_cols, res, alpha, *, tn=OUT_TN):
    assert n_cols % tn == 0 and w.shape[1] == sum(p.shape[1] for p in x)
    n_tiles = n_cols // tn

    def w_specs(ck):
        return [pl.BlockSpec((None, ck, tn), lambda j, i: (layer, i, jnp.minimum(j, n_tiles - 1)))]

    return _ws_matmul(x, [w], w_specs, n_tiles, tn, tn, F32, _residual_epilogue(alpha),
                      transposed=False, name="matmul", res=res)


def _matmul_wt(x, w_t, layer, n_tiles, row_start, out_dtype, *, tn=MM_TN, scaled_cols=0, scale=1.0, out_split=None):
    assert w_t.shape[2] == x.shape[1] and scaled_cols % tn == 0
    scaled_tiles = scaled_cols // tn

    def w_specs(ck):
        return [pl.BlockSpec((pl.Element(1), pl.Element(tn), pl.Element(ck)),
                             lambda j, i: (layer, row_start(jnp.minimum(j, n_tiles - 1)), i * ck))]

    def epilogue(acc, tile, res):
        return acc * jnp.where(tile < scaled_tiles, scale, 1.0) if scaled_tiles else acc

    return _ws_matmul(x, [w_t], w_specs, n_tiles, tn, tn, out_dtype, epilogue, transposed=True, name="matmul_wt",
                      out_split=out_split)


def _gate_up(x, w_gate, w_up, layer, *, tn=FFN_TN):
    d_ff = w_gate.shape[2]
    assert w_gate.shape[1] == x.shape[1]
    n_tiles = pl.cdiv(d_ff, tn)

    def w_specs(ck):
        return [pl.BlockSpec((None, ck, tn), lambda j, i: (layer, i, jnp.minimum(j, n_tiles - 1)))] * 2

    def epilogue(z, tile, res):
        return _silu(z[:, :tn]) * z[:, tn:]

    return _ws_matmul(x, [w_gate, w_up], w_specs, n_tiles, tn, tn, BF16, epilogue,
                      transposed=False, name="gate_up", n_out=d_ff)


def _down(h, w_down, layer, res, alpha, *, tm=DOWN_TM, tn=DOWN_TN):
    d_ff, n = w_down.shape[1:]
    assert n % tn == 0 and h.shape[1] == d_ff
    n_tiles = n // tn

    def w_specs(ck):
        return [pl.BlockSpec((None, ck, tn), lambda j, i: (layer, i, jnp.minimum(j, n_tiles - 1)))]

    return _ws_matmul(h, [w_down], w_specs, n_tiles, tn, tn, F32, _residual_epilogue(alpha),
                      transposed=False, name="down_proj", tm=tm, res=res)


def _split3(x):
    a = x.astype(BF16).astype(F32)
    r = x - a
    b = r.astype(BF16).astype(F32)
    return a, b, (r - b).astype(BF16).astype(F32)


def _dot_split2(a, b):
    a_hi = a.astype(BF16)
    b_hi = b.astype(BF16)
    a_lo = (a - a_hi.astype(F32)).astype(BF16)
    b_lo = (b - b_hi.astype(F32)).astype(BF16)
    return (jnp.dot(a_hi, b_hi, preferred_element_type=F32) + jnp.dot(a_hi, b_lo, preferred_element_type=F32)
            + jnp.dot(a_lo, b_hi, preferred_element_type=F32))


def _gates_kernel(x_ref, wf_ref, wl_ref, bias_ref, tri_ref, o_ref, aq_ref, ak_ref, carry_ref):
    @pl.when(pl.program_id(1) == 0)
    def _():
        carry_ref[...] = jnp.zeros_like(carry_ref)

    wf, wl = wf_ref[0], wl_ref[0]
    pad = jnp.zeros((V7X_LANES - wf.shape[0] - wl.shape[0], wf.shape[1]), F32)
    w = jnp.concatenate([wf, wl, pad], axis=0).astype(BF16)
    z = lax.dot_general(x_ref[...], w, _NT, preferred_element_type=F32)
    log_f = _log_sigmoid(z + bias_ref[...])
    parts = jnp.concatenate([p.astype(BF16) for p in _split3(log_f)], axis=1)
    c3 = jnp.dot(tri_ref[...], parts, preferred_element_type=F32)
    nl = log_f.shape[1]
    c = c3[:, :nl] + c3[:, nl:2 * nl] + c3[:, 2 * nl:] + carry_ref[0:1, :]
    rows = c.shape[0]
    carry_ref[...] = jnp.broadcast_to(c[rows - 1:rows, :], carry_ref.shape)
    lane = lax.broadcasted_iota(jnp.int32, c.shape, 1)
    o_ref[...] = jnp.where(lane < FOX_HEADS, c, z)

    c2 = c * LOG2_E
    for h in range(FOX_HEADS):
        c1, c2_, c3 = _split3(jnp.broadcast_to(c2[:, h:h + 1], c.shape))
        aq = jnp.where(lane == 0, c1, jnp.where(lane == 1, c2_, jnp.where(lane == 2, c3,
                       jnp.where(lane < 6, 1.0, 0.0))))
        ak = jnp.where(lane < 3, 1.0, jnp.where(lane == 3, -c1, jnp.where(lane == 4, -c2_,
                       jnp.where(lane == 5, -c3, 0.0))))
        aq_ref[0, h] = aq.astype(BF16)
        ak_ref[0, h] = ak.astype(BF16)


def _gates(x, w_in_t, layer, bias, bsz, seq):
    m, kdim = x.shape
    steps = seq // GATE_ROWS
    tri = jnp.asarray(np.tril(np.ones((GATE_ROWS, GATE_ROWS), np.float32)), BF16)
    aug_shape = jax.ShapeDtypeStruct((bsz, FOX_HEADS, seq, V7X_LANES), BF16)
    aug_spec = pl.BlockSpec((1, FOX_HEADS, GATE_ROWS, V7X_LANES), lambda b, i: (b, 0, i, 0))
    assert IN_F_LO % F32_SUBLANES == 0 and IN_LOW_LO % F32_SUBLANES == 0

    def rows_spec(n_rows, row0):
        return pl.BlockSpec((pl.Element(1), pl.Element(n_rows), pl.Element(kdim)), lambda b, i: (layer, row0, 0))

    blk = (_nbytes((GATE_ROWS, kdim), x.dtype) + _nbytes((V7X_LANES, kdim), F32)
           + _nbytes((GATE_ROWS, GATE_ROWS), F32) + _nbytes((GATE_ROWS, V7X_LANES), F32)
           + 2 * _nbytes((FOX_HEADS, GATE_ROWS, V7X_LANES), BF16))
    return pl.pallas_call(
        _gates_kernel,
        grid=(bsz, steps),
        in_specs=[pl.BlockSpec((GATE_ROWS, kdim), lambda b, i: (b * steps + i, 0)),
                  rows_spec(FOX_HEADS, IN_F_LO), rows_spec(GLA_GATE_RANK, IN_LOW_LO),
                  pl.BlockSpec((1, V7X_LANES), lambda b, i: (0, 0)),
                  pl.BlockSpec((GATE_ROWS, GATE_ROWS), lambda b, i: (0, 0))],
        out_specs=[pl.BlockSpec((GATE_ROWS, V7X_LANES), lambda b, i: (b * steps + i, 0)), aug_spec, aug_spec],
        out_shape=[jax.ShapeDtypeStruct((m, V7X_LANES), F32), aug_shape, aug_shape],
        scratch_shapes=[pltpu.VMEM((8, V7X_LANES), F32)],
        compiler_params=pltpu.CompilerParams(
            dimension_semantics=("arbitrary", "arbitrary"),
            vmem_limit_bytes=_vmem_limit(blk, 16 * _nbytes((GATE_ROWS, V7X_LANES), F32))),
        name="gates",
    )(x, w_in_t, w_in_t, bias, tri)


def _fox_kernel(q_ref, k_ref, v_ref, aq_ref, ak_ref, g_ref, o_ref, kp_ref, vp_ref, m_ref, acc_ref, *, tq, tk):
    qi = pl.program_id(2)
    n_heads = aq_ref.shape[1]
    d = q_ref.shape[2] // n_heads
    dp = acc_ref.shape[2]

    new_rows = pl.ds(pl.multiple_of(qi * tk, tk), tk)
    lane = lax.broadcasted_iota(jnp.int32, (tk, V7X_LANES), 1)
    for g in range(n_heads):
        kp_ref[g, new_rows, :d] = k_ref[0, :, g * d:(g + 1) * d]
        kp_ref[g, new_rows, d:] = ak_ref[0, g]
        vp_ref[g, new_rows, :d] = v_ref[0, :, g * d:(g + 1) * d]
        vp_ref[g, new_rows, d:] = jnp.where(lane == 0, 1.0, 0.0).astype(BF16)

    qp = [jnp.concatenate([q_ref[0, :, g * d:(g + 1) * d], aq_ref[0, g]], axis=1) for g in range(n_heads)]
    m_ref[...] = jnp.full_like(m_ref, NEG_BIG)
    acc_ref[...] = jnp.zeros_like(acc_ref)

    def step(j, masked):
        start = pl.multiple_of(j * tk, tk)
        for g in range(n_heads):
            s = lax.dot_general(qp[g], kp_ref[g, pl.ds(start, tk), :], _NT, preferred_element_type=F32)
            if masked:
                row = lax.broadcasted_iota(jnp.int32, (tq, tk), 0)
                col = lax.broadcasted_iota(jnp.int32, (tq, tk), 1)
                s = jnp.where(row >= col, s, NEG_BIG)
            m_prev = m_ref[g]
            m_new = jnp.maximum(m_prev, jnp.max(s, axis=-1, keepdims=True))
            p = jnp.exp2(s - jnp.concatenate([m_new] * (tk // V7X_LANES), axis=1)).astype(BF16)
            alpha = jnp.exp2(m_prev - m_new)
            acc_ref[g] = (jnp.concatenate([alpha] * (dp // V7X_LANES), axis=1) * acc_ref[g]
                          + jnp.dot(p, vp_ref[g, pl.ds(start, tk), :], preferred_element_type=F32))
            m_ref[g] = m_new

    unroll = FOX_UNROLL
    shift = unroll.bit_length() - 1
    assert unroll == 1 << shift

    def body(i, carry):
        for u in range(unroll):
            step(unroll * i + u, False)
        return carry

    lax.fori_loop(0, lax.shift_right_logical(qi, shift), body, 0)

    for rem in range(unroll):
        @pl.when((qi & (unroll - 1)) == rem)
        def _():
            for u in range(rem):
                step(qi - rem + u, False)
            step(qi, True)

    for g in range(n_heads):
        acc = acc_ref[g]
        o = acc[:, :d] / acc[:, d:d + 1]
        o = o * lax.rsqrt(jnp.mean(o * o, axis=-1, keepdims=True) + EPS) * g_ref[g]
        o_ref[0, :, g * d:(g + 1) * d] = o.astype(o_ref.dtype)


def _fox(proj, aq, ak, norm_g, *, tq=FOX_TQ, tk=FOX_TK, group=FOX_GROUP):
    _, bsz, seq, gd = proj.shape
    d = FOX_HEAD_DIM
    dp = d + V7X_LANES
    n_groups = FOX_HEADS // group
    assert tq == tk and seq % tq == 0 and FOX_HEADS % group == 0 and gd == group * d
    blk = 4 * _nbytes((tq, gd), BF16) + 2 * _nbytes((group, tq, V7X_LANES), BF16) + _nbytes((group, 8, d), F32)
    scratch = (2 * _nbytes((group, seq, dp), BF16) + _nbytes((group, tq, V7X_LANES), F32)
               + _nbytes((group, tq, dp), F32))
    return pl.pallas_call(
        functools.partial(_fox_kernel, tq=tq, tk=tk),
        grid=(bsz, n_groups, seq // tq),
        in_specs=[pl.BlockSpec((None, 1, tq, gd), lambda b, h, i: (h, b, i, 0)),
                  pl.BlockSpec((None, 1, tk, gd), lambda b, h, i: (n_groups + h, b, i, 0)),
                  pl.BlockSpec((None, 1, tk, gd), lambda b, h, i: (2 * n_groups + h, b, i, 0)),
                  pl.BlockSpec((1, group, tq, V7X_LANES), lambda b, h, i: (b, h, i, 0)),
                  pl.BlockSpec((1, group, tk, V7X_LANES), lambda b, h, i: (b, h, i, 0)),
                  pl.BlockSpec((group, 1, d), lambda b, h, i: (h, 0, 0))],
        out_specs=pl.BlockSpec((1, tq, gd), lambda b, h, i: (b, i, h)),
        out_shape=jax.ShapeDtypeStruct((bsz, seq, FOX_WIDTH), BF16),
        scratch_shapes=[pltpu.VMEM((group, seq, dp), BF16), pltpu.VMEM((group, seq, dp), BF16),
                        pltpu.VMEM((group, tq, V7X_LANES), F32), pltpu.VMEM((group, tq, dp), F32)],
        compiler_params=pltpu.CompilerParams(
            dimension_semantics=("arbitrary", "arbitrary", "arbitrary"),
            vmem_limit_bytes=_vmem_limit(blk, scratch + 8 * group * _nbytes((tq, tk), F32))),
        name="fox_attention",
    )(proj, proj, proj, aq, ak, norm_g)


def _gla_tables(chunk):
    idx = np.arange(chunk)
    masks = []
    half = chunk // 2
    while half >= 1:
        blk = idx // (2 * half)
        second = (idx // half) % 2 == 1
        masks.append(((blk[:, None] == blk[None, :]) & second[:, None] & ~second[None, :]).astype(np.float32))
        half //= 2
    span_chunks = V7X_MXU_DIM // chunk
    tri = np.kron(np.eye(span_chunks, dtype=np.float32), np.tril(np.ones((chunk, chunk), np.float32)))
    return tri, np.stack(masks, axis=0)


def _level_refs(b):
    chunk, width = b.shape
    refs = []
    half = chunk // 2
    while half >= 4:
        pieces = [jnp.broadcast_to(b[s + half - 1:s + half], (2 * half, width)) for s in range(0, chunk, 2 * half)]
        refs.append(pieces[0] if len(pieces) == 1 else jnp.concatenate(pieces, axis=0))
        half //= 2
    prev1 = pltpu.roll(b, 1, 0)
    prev2 = pltpu.roll(b, 2, 0)
    next1 = pltpu.roll(b, chunk - 1, 0)
    pos = lax.broadcasted_iota(jnp.int32, b.shape, 0) & 3
    refs.append(jnp.where(pos == 0, next1, jnp.where(pos == 1, b, jnp.where(pos == 2, prev1, prev2))))
    refs.append(jnp.where((pos & 1) == 0, b, prev1))
    return refs


def _gla_kernel(q_ref, k_ref, v_ref, gg_ref, gates_ref, wup_ref, bgk_ref, gn_ref, tri_ref, mask_ref,
                o_ref, state_ref, la_ref, *, rows, chunk, scale):
    dk, dv = GLA_KEY_DIM, GLA_VAL_DIM

    @pl.when(pl.program_id(1) == 0)
    def _():
        state_ref[...] = jnp.zeros_like(state_ref)

    pre = _dot_split2(gates_ref[0], wup_ref[...]) + bgk_ref[...]
    log_a = _log_sigmoid(pre) * (LOG2_E / GLA_GATE_NORMALIZER)
    span = tri_ref.shape[0]
    width = log_a.shape[1]
    for r in range(0, rows, span):
        parts = jnp.concatenate([p.astype(BF16) for p in _split3(log_a[r:r + span])], axis=1)
        b3 = jnp.dot(tri_ref[...], parts, preferred_element_type=F32)
        la_ref[r:r + span, :] = b3[:, :width] + b3[:, width:2 * width] + b3[:, 2 * width:]

    n_levels = mask_ref.shape[0]
    eye = (lax.broadcasted_iota(jnp.int32, (chunk, chunk), 0)
           == lax.broadcasted_iota(jnp.int32, (chunk, chunk), 1))
    row_dk = lax.broadcasted_iota(jnp.int32, (chunk, dk), 0)
    row_e = lax.broadcasted_iota(jnp.int32, (BF16_SUBLANES, dk), 0)
    ones_cols = jnp.ones((BF16_SUBLANES, V7X_LANES), BF16)

    def head_chunk(r0, h):
        ks = slice(h * dk, (h + 1) * dk)
        vs = slice(h * dv, (h + 1) * dv)
        b = la_ref[pl.ds(r0, chunk), ks]
        kt, kl = divmod(h * dk, q_ref.shape[3])
        vt, vl = divmod(h * dv, v_ref.shape[3])
        q = q_ref[kt, 0, pl.ds(r0, chunk), kl:kl + dk].astype(F32) * scale
        k = k_ref[kt, 0, pl.ds(r0, chunk), kl:kl + dk].astype(F32)
        v = v_ref[vt, 0, pl.ds(r0, chunk), vl:vl + dv]

        state = state_ref[h]
        o = jnp.dot((q * jnp.exp2(b)).astype(BF16), state.astype(BF16), preferred_element_type=F32)

        scores = jnp.zeros((chunk, chunk), F32)
        half = chunk // 2
        for lvl, ref in enumerate(_level_refs(b)):
            u = (jnp.where((row_dk & half) != 0, q, k) * jnp.exp2(-jnp.abs(b - ref))).astype(BF16)
            scores = scores + mask_ref[lvl] * lax.dot_general(u, u, _NT, preferred_element_type=F32)
            half //= 2
        scores = jnp.where(eye, jnp.sum(q * k, axis=-1, keepdims=True), scores)
        o = o + jnp.dot(scores.astype(BF16), v, preferred_element_type=F32)

        b_last = b[chunk - 1:chunk]
        k_dec = (k * jnp.exp2(b_last - b)).astype(BF16)
        e1, e2, e3 = _split3(jnp.broadcast_to(jnp.exp2(b_last), row_e.shape))
        e_rows = jnp.where(row_e == 0, e1, jnp.where(row_e == 1, e2, jnp.where(row_e == 2, e3, 0.0)))
        dec_col = lax.dot_general(e_rows.astype(BF16), ones_cols, _TN, preferred_element_type=F32)
        decay = jnp.concatenate([dec_col] * (dv // V7X_LANES), axis=1)
        state_ref[h] = state * decay + lax.dot_general(k_dec, v, _TN, preferred_element_type=F32)

        o = o * lax.rsqrt(jnp.mean(o * o, axis=-1, keepdims=True) + EPS) * gn_ref[...]
        gate = _silu(gg_ref[vt, 0, pl.ds(r0, chunk), vl:vl + dv].astype(F32))
        o_ref[0, pl.ds(r0, chunk), vs] = (o * gate).astype(o_ref.dtype)

    assert n_levels == chunk.bit_length() - 1

    def chunk_body(c, carry):
        r0 = pl.multiple_of(c * chunk, chunk)
        for h in range(GLA_HEADS):
            head_chunk(r0, h)
        return carry

    lax.fori_loop(0, rows // chunk, chunk_body, 0)


def _gla(proj, col0, gates, w_up_pad, b_gk, norm_g, *, rows=GLA_ROWS, chunk=GLA_CHUNK):
    _, bsz, seq, tw = proj.shape
    kw, vw = GLA_KEY_WIDTH, GLA_WIDTH
    assert seq % rows == 0 and rows % chunk == 0
    assert kw % tw == 0 and col0 % kw == 0 and (col0 + 2 * kw) % vw == 0 and tw % GLA_VAL_DIM == 0
    k_tiles, v_tiles = kw // tw, vw // tw
    q_blk = col0 // kw
    v_blk = (col0 + 2 * kw) // vw
    tri, masks = _gla_tables(chunk)
    blk = (2 * _nbytes((rows, kw), BF16) + 3 * _nbytes((rows, vw), BF16) + _nbytes((rows, V7X_LANES), F32)
           + _nbytes((V7X_LANES, kw), F32) + _nbytes(masks.shape, F32))
    scratch = _nbytes((GLA_HEADS, GLA_KEY_DIM, GLA_VAL_DIM), F32) + _nbytes((rows, kw), F32)
    return pl.pallas_call(
        functools.partial(_gla_kernel, rows=rows, chunk=chunk, scale=GLA_KEY_DIM ** -0.5),
        grid=(bsz, seq // rows),
        in_specs=[pl.BlockSpec((k_tiles, 1, rows, tw), lambda b, t: (q_blk, b, t, 0)),
                  pl.BlockSpec((k_tiles, 1, rows, tw), lambda b, t: (q_blk + 1, b, t, 0)),
                  pl.BlockSpec((v_tiles, 1, rows, tw), lambda b, t: (v_blk, b, t, 0)),
                  pl.BlockSpec((v_tiles, 1, rows, tw), lambda b, t: (v_blk + 1, b, t, 0)),
                  pl.BlockSpec((1, rows, V7X_LANES), lambda b, t: (b, t, 0)),
                  pl.BlockSpec((V7X_LANES, kw), lambda b, t: (0, 0)),
                  pl.BlockSpec((1, kw), lambda b, t: (0, 0)),
                  pl.BlockSpec((1, GLA_VAL_DIM), lambda b, t: (0, 0)),
                  pl.BlockSpec(tri.shape, lambda b, t: (0, 0)),
                  pl.BlockSpec(masks.shape, lambda b, t: (0, 0, 0))],
        out_specs=pl.BlockSpec((1, rows, vw), lambda b, t: (b, t, 0)),
        out_shape=jax.ShapeDtypeStruct((bsz, seq, vw), BF16),
        scratch_shapes=[pltpu.VMEM((GLA_HEADS, GLA_KEY_DIM, GLA_VAL_DIM), F32), pltpu.VMEM((rows, kw), F32)],
        compiler_params=pltpu.CompilerParams(
            dimension_semantics=("arbitrary", "arbitrary"),
            vmem_limit_bytes=_vmem_limit(blk, scratch + (8 << 20))),
        name="gla",
    )(proj, proj, proj, proj, gates, w_up_pad, b_gk.reshape(1, -1), norm_g.reshape(1, -1),
      jnp.asarray(tri, BF16), jnp.asarray(masks))


def _prep_layer(b_f, w_gk_up):
    gate_bias = jnp.concatenate([b_f, jnp.zeros((V7X_LANES - FOX_HEADS,), F32)]).reshape(1, V7X_LANES)
    w_up_pad = jnp.zeros((V7X_LANES, GLA_KEY_WIDTH), F32).at[FOX_HEADS:FOX_HEADS + GLA_GATE_RANK].set(w_gk_up)
    return gate_bias, w_up_pad


def kernel(x, ln_in_g, ln_in_b, w_in, b_f, w_gk_up, b_gk, fox_norm_g, gla_norm_g, w_out, ln1_g, ln1_b,
           w_gate, w_up, w_down, ln2_g, ln2_b):
    bsz, seq, d_model = x.shape
    depth = w_in.shape[0]
    alpha = (2 * depth) ** 0.25
    rows = bsz * seq

    w_in_t = jnp.swapaxes(w_in, 1, 2)
    n_fox_tiles = IN_F_LO // MM_TN
    n_proj_tiles = n_fox_tiles + (IN_LOW_LO - IN_G_LO) // MM_TN

    def proj_row_start(j):
        return pl.multiple_of(j * MM_TN + jnp.where(j >= n_fox_tiles, IN_G_LO - IN_F_LO, 0), F32_SUBLANES)

    t = x.reshape(rows, d_model)
    ln_g, ln_b = ln_in_g, ln_in_b
    x16, mu, rstd = _layer_norm(t, ln_g, ln_b)
    for l in range(depth):
        gate_bias, w_up_pad = _prep_layer(b_f[l], w_gk_up[l])

        proj = _matmul_wt(x16, w_in_t, l, n_proj_tiles, proj_row_start, BF16, scaled_cols=FOX_WIDTH,
                          scale=FOX_HEAD_DIM ** -0.5 * LOG2_E, out_split=MM_TN // PROJ_TILE)
        proj = proj.reshape(-1, bsz, seq, PROJ_TILE)
        gates, aug_q, aug_k = _gates(x16, w_in_t, l, gate_bias, bsz, seq)
        gates = gates.reshape(bsz, seq, V7X_LANES)

        o_fox = _fox(proj, aug_q, aug_k, fox_norm_g[l].reshape(FOX_HEADS, 1, FOX_HEAD_DIM))
        o_gla = _gla(proj, IN_F_LO, gates, w_up_pad, b_gk[l], gla_norm_g[l])

        mixed = [o_fox.reshape(rows, FOX_WIDTH), o_gla.reshape(rows, GLA_WIDTH)]
        t = _matmul(mixed, w_out, l, d_model, (t, mu, rstd, ln_g, ln_b), alpha)
        ln_g, ln_b = ln1_g[l], ln1_b[l]
        x16, mu, rstd = _layer_norm(t, ln_g, ln_b)

        hidden = _gate_up(x16, w_gate, w_up, l)
        t = _down(hidden, w_down, l, (t, mu, rstd, ln_g, ln_b), alpha)
        ln_g, ln_b = ln2_g[l], ln2_b[l]
        if l < depth - 1:
            x16, mu, rstd = _layer_norm(t, ln_g, ln_b)
    out, = _layer_norm(t, ln_g, ln_b, final=True)
    return out.reshape(bsz, seq, d_model)
```

```python
import functools

import numpy as np
import jax
import jax.numpy as jnp
from jax import lax
from jax.experimental import pallas as pl
from jax.experimental.pallas import tpu as pltpu

F32 = jnp.float32
BF16 = jnp.bfloat16

V7X_VMEM_BYTES = 64 * 2**20
V7X_LANES = 128
V7X_MXU_DIM = 256
F32_SUBLANES = 8
BF16_SUBLANES = 16

FOX_HEADS = 16
FOX_HEAD_DIM = 128
FOX_WIDTH = FOX_HEADS * FOX_HEAD_DIM
GLA_HEADS = 4
GLA_KEY_DIM = 256
GLA_VAL_DIM = 512
GLA_KEY_WIDTH = GLA_HEADS * GLA_KEY_DIM
GLA_WIDTH = GLA_HEADS * GLA_VAL_DIM
GLA_GATE_RANK = 16
GLA_GATE_NORMALIZER = 16.0
EPS = 1e-5
IN_F_LO = 3 * FOX_WIDTH
IN_G_LO = IN_F_LO + FOX_HEADS
IN_LOW_LO = IN_G_LO + 2 * GLA_KEY_WIDTH + 2 * GLA_WIDTH

LN_ROWS = 512
MM_TM = 1024
MM_TN = 1024
OUT_TN = 512
FFN_TN = 512
DOWN_TM = 512
DOWN_TN = 512
GATE_ROWS = 512
FOX_TQ = 512
FOX_TK = 512
FOX_GROUP = 4
FOX_UNROLL = 4
PROJ_TILE = FOX_GROUP * FOX_HEAD_DIM
GLA_ROWS = 512
GLA_CHUNK = 128
NEG_BIG = -1e30
LOG2_E = 1.4426950408889634

_NT = (((1,), (1,)), ((), ()))
_TN = (((0,), (0,)), ((), ()))


def _vmem_limit(block_bytes, extra_bytes):
    need = 2 * block_bytes + extra_bytes + (4 << 20)
    return int(min(need, V7X_VMEM_BYTES - (6 << 20)))


def _nbytes(shape, dtype):
    return int(np.prod(shape)) * jnp.dtype(dtype).itemsize


def _log_sigmoid(x):
    return jnp.minimum(x, 0.0) - jnp.log1p(jnp.exp(-jnp.abs(x)))


def _silu(x):
    return x * (1.0 / (1.0 + jnp.exp(-x)))


def _ln_stats(t):
    mu = jnp.mean(t, axis=-1, keepdims=True)
    tc = t - mu
    return mu, lax.rsqrt(jnp.mean(tc * tc, axis=-1, keepdims=True) + EPS)


def _ln_apply(t, mu, rstd, g, b):
    return (t - mu) * rstd * g + b


def _ln_kernel(x_ref, g_ref, b_ref, *outs, final):
    t = x_ref[...]
    mu, rstd = _ln_stats(t)
    y = _ln_apply(t, mu, rstd, g_ref[...], b_ref[...])
    if final:
        outs[0][...] = y
    else:
        outs[0][...] = y.astype(BF16)
        outs[1][...] = jnp.broadcast_to(mu, outs[1].shape)
        outs[2][...] = jnp.broadcast_to(rstd, outs[2].shape)


def _layer_norm(t, g, b, *, final=False):
    rows, d = t.shape
    assert rows % LN_ROWS == 0
    row_spec = pl.BlockSpec((LN_ROWS, d), lambda i: (i, 0))
    vec_spec = pl.BlockSpec((1, d), lambda i: (0, 0))
    stat_spec = pl.BlockSpec((LN_ROWS, V7X_LANES), lambda i: (i, 0))
    if final:
        out_shape, out_specs = [jax.ShapeDtypeStruct((rows, d), F32)], [row_spec]
    else:
        stat_shape = jax.ShapeDtypeStruct((rows, V7X_LANES), F32)
        out_shape, out_specs = [jax.ShapeDtypeStruct((rows, d), BF16), stat_shape, stat_shape], [row_spec, stat_spec, stat_spec]
    blk = 2 * _nbytes((LN_ROWS, d), F32) + 2 * _nbytes((LN_ROWS, V7X_LANES), F32)
    return pl.pallas_call(
        functools.partial(_ln_kernel, final=final),
        grid=(rows // LN_ROWS,),
        in_specs=[row_spec, vec_spec, vec_spec], out_specs=out_specs, out_shape=out_shape,
        compiler_params=pltpu.CompilerParams(
            dimension_semantics=("arbitrary",),
            vmem_limit_bytes=_vmem_limit(blk, 4 * _nbytes((LN_ROWS, d), F32))),
        name="layer_norm",
    )(t, g.reshape(1, d), b.reshape(1, d))


def _ws_kernel(*refs, n_x, n_w, transposed, epilogue, n_res):
    x_refs, rest = refs[:n_x], refs[n_x:]
    w_refs = rest[:n_w]
    res_refs = rest[n_w:n_w + n_res]
    o_ref, wb_ref = rest[n_w + n_res:]
    j, i = pl.program_id(0), pl.program_id(1)
    tn = wb_ref.shape[2] // n_w
    ck = w_refs[0].shape[-1] if transposed else w_refs[0].shape[0]

    def convert_chunk(slot):
        rows = pl.ds(pl.multiple_of(i * ck, ck), ck)
        for t, w_ref in enumerate(w_refs):
            chunk = w_ref[0].T if transposed else w_ref[...]
            wb_ref[slot, rows, t * tn:(t + 1) * tn] = chunk.astype(BF16)

    @pl.when(j == 0)
    def _():
        convert_chunk(0)

    for parity in (0, 1):
        @pl.when(jnp.logical_and(j > 0, (j & 1) == parity))
        def _():
            convert_chunk(parity)
            acc, k0 = None, 0
            for x_ref in x_refs:
                k1 = k0 + x_ref.shape[1]
                part = jnp.dot(x_ref[...], wb_ref[1 - parity, k0:k1, :], preferred_element_type=F32)
                acc, k0 = (part if acc is None else acc + part), k1
            out = epilogue(acc, j - 1, [r[...] for r in res_refs]).astype(o_ref.dtype)
            if len(o_ref.shape) == 2:
                o_ref[...] = out
            else:
                width = o_ref.shape[2]
                for s in range(o_ref.shape[0]):
                    o_ref[s] = out[:, s * width:(s + 1) * width]


def _ws_matmul(x, weights, w_specs, n_tiles, tn_w, tn_out, out_dtype, epilogue, *, transposed, name, tm=MM_TM,
               res=None, n_out=None, out_split=None):
    xs = list(x) if isinstance(x, (list, tuple)) else [x]
    m = xs[0].shape[0]
    kdim = sum(p.shape[1] for p in xs)
    n_m = m // tm
    ck = kdim // n_m
    assert m % tm == 0 and kdim % n_m == 0 and ck % (V7X_LANES if transposed else BF16_SUBLANES) == 0
    n_w = len(weights)

    def row_tile(j, i):
        return jnp.where(j == 0, 0, i)

    def col_tile(j, i):
        return jnp.maximum(j - 1, 0)

    if out_split is None:
        out_spec = pl.BlockSpec((tm, tn_out), lambda j, i: (row_tile(j, i), col_tile(j, i)))
        out_shape = jax.ShapeDtypeStruct((m, n_tiles * tn_out if n_out is None else n_out), out_dtype)
    else:
        assert res is None and n_out is None and tn_out % out_split == 0
        out_spec = pl.BlockSpec((out_split, tm, tn_out // out_split), lambda j, i: (col_tile(j, i), row_tile(j, i), 0))
        out_shape = jax.ShapeDtypeStruct((n_tiles * out_split, m, tn_out // out_split), out_dtype)
    res_arrays, res_specs, res_bytes = [], [], 0
    if res is not None:
        t, mu, rstd, g, b = res
        stat_spec = pl.BlockSpec((tm, V7X_LANES), lambda j, i: (row_tile(j, i), 0))
        vec_spec = pl.BlockSpec((1, tn_out), lambda j, i: (0, col_tile(j, i)))
        res_arrays = [t, mu, rstd, g.reshape(1, -1), b.reshape(1, -1)]
        res_specs = [out_spec, stat_spec, stat_spec, vec_spec, vec_spec]
        res_bytes = _nbytes((tm, tn_out), F32) + 2 * _nbytes((tm, V7X_LANES), F32)
    blk = (_nbytes((tm, kdim), BF16) + n_w * _nbytes((ck, tn_w), F32) + _nbytes((tm, tn_out), out_dtype)
           + res_bytes)
    scratch = _nbytes((2, kdim, n_w * tn_w), BF16)
    extra = scratch + 2 * _nbytes((tm, n_w * tn_w), F32) + 3 * n_w * _nbytes((ck, tn_w), F32)
    return pl.pallas_call(
        functools.partial(_ws_kernel, n_x=len(xs), n_w=n_w, transposed=transposed, epilogue=epilogue,
                          n_res=len(res_arrays)),
        grid=(n_tiles + 1, n_m),
        in_specs=([pl.BlockSpec((tm, p.shape[1]), lambda j, i: (row_tile(j, i), 0)) for p in xs]
                  + w_specs(ck) + res_specs),
        out_specs=out_spec,
        out_shape=out_shape,
        scratch_shapes=[pltpu.VMEM((2, kdim, n_w * tn_w), BF16)],
        compiler_params=pltpu.CompilerParams(
            dimension_semantics=("arbitrary", "arbitrary"),
            vmem_limit_bytes=_vmem_limit(blk, extra)),
        name=name,
    )(*xs, *weights, *res_arrays)


def _residual_epilogue(alpha):
    def epilogue(acc, tile, res):
        t, mu, rstd, g, b = res
        reps = t.shape[1] // V7X_LANES
        x = _ln_apply(t, jnp.concatenate([mu] * reps, axis=1), jnp.concatenate([rstd] * reps, axis=1), g, b)
        return alpha * x + acc
    return epilogue


def _matmul(x, w, layer, n_cols, res, alpha, *, tn=OUT_TN):
    assert n_cols % tn == 0 and w.shape[1] == sum(p.shape[1] for p in x)
    n_tiles = n_cols // tn

    def w_specs(ck):
        return [pl.BlockSpec((None, ck, tn), lambda j, i: (layer, i, jnp.minimum(j, n_tiles - 1)))]

    return _ws_matmul(x, [w], w_specs, n_tiles, tn, tn, F32, _residual_epilogue(alpha),
                      transposed=False, name="matmul", res=res)


def _matmul_wt(x, w_t, layer, n_tiles, row_start, out_dtype, *, tn=MM_TN, scaled_cols=0, scale=1.0, out_split=None):
    assert w_t.shape[2] == x.shape[1] and scaled_cols % tn == 0
    scaled_tiles = scaled_cols // tn

    def w_specs(ck):
        return [pl.BlockSpec((pl.Element(1), pl.Element(tn), pl.Element(ck)),
                             lambda j, i: (layer, row_start(jnp.minimum(j, n_tiles - 1)), i * ck))]

    def epilogue(acc, tile, res):
        return acc * jnp.where(tile < scaled_tiles, scale, 1.0) if scaled_tiles else acc

    return _ws_matmul(x, [w_t], w_specs, n_tiles, tn, tn, out_dtype, epilogue, transposed=True, name="matmul_wt",
                      out_split=out_split)


def _gate_up(x, w_gate, w_up, layer, *, tn=FFN_TN):
    d_ff = w_gate.shape[2]
    assert w_gate.shape[1] == x.shape[1]
    n_tiles = pl.cdiv(d_ff, tn)

    def w_specs(ck):
        return [pl.BlockSpec((None, ck, tn), lambda j, i: (layer, i, jnp.minimum(j, n_tiles - 1)))] * 2

    def epilogue(z, tile, res):
        return _silu(z[:, :tn]) * z[:, tn:]

    return _ws_matmul(x, [w_gate, w_up], w_specs, n_tiles, tn, tn, BF16, epilogue,
                      transposed=False, name="gate_up", n_out=d_ff)


def _down(h, w_down, layer, res, alpha, *, tm=DOWN_TM, tn=DOWN_TN):
    d_ff, n = w_down.shape[1:]
    assert n % tn == 0 and h.shape[1] == d_ff
    n_tiles = n // tn

    def w_specs(ck):
        return [pl.BlockSpec((None, ck, tn), lambda j, i: (layer, i, jnp.minimum(j, n_tiles - 1)))]

    return _ws_matmul(h, [w_down], w_specs, n_tiles, tn, tn, F32, _residual_epilogue(alpha),
                      transposed=False, name="down_proj", tm=tm, res=res)


def _split3(x):
    a = x.astype(BF16).astype(F32)
    r = x - a
    b = r.astype(BF16).astype(F32)
    return a, b, (r - b).astype(BF16).astype(F32)


def _dot_split2(a, b):
    a_hi = a.astype(BF16)
    b_hi = b.astype(BF16)
    a_lo = (a - a_hi.astype(F32)).astype(BF16)
    b_lo = (b - b_hi.astype(F32)).astype(BF16)
    return (jnp.dot(a_hi, b_hi, preferred_element_type=F32) + jnp.dot(a_hi, b_lo, preferred_element_type=F32)
            + jnp.dot(a_lo, b_hi, preferred_element_type=F32))


def _gates_kernel(x_ref, wf_ref, wl_ref, bias_ref, tri_ref, o_ref, aq_ref, ak_ref, carry_ref):
    @pl.when(pl.program_id(1) == 0)
    def _():
        carry_ref[...] = jnp.zeros_like(carry_ref)

    wf, wl = wf_ref[0], wl_ref[0]
    pad = jnp.zeros((V7X_LANES - wf.shape[0] - wl.shape[0], wf.shape[1]), F32)
    w = jnp.concatenate([wf, wl, pad], axis=0).astype(BF16)
    z = lax.dot_general(x_ref[...], w, _NT, preferred_element_type=F32)
    log_f = _log_sigmoid(z + bias_ref[...])
    parts = jnp.concatenate([p.astype(BF16) for p in _split3(log_f)], axis=1)
    c3 = jnp.dot(tri_ref[...], parts, preferred_element_type=F32)
    nl = log_f.shape[1]
    c = c3[:, :nl] + c3[:, nl:2 * nl] + c3[:, 2 * nl:] + carry_ref[0:1, :]
    rows = c.shape[0]
    carry_ref[...] = jnp.broadcast_to(c[rows - 1:rows, :], carry_ref.shape)
    lane = lax.broadcasted_iota(jnp.int32, c.shape, 1)
    o_ref[...] = jnp.where(lane < FOX_HEADS, c, z)

    c2 = c * LOG2_E
    for h in range(FOX_HEADS):
        c1, c2_, c3 = _split3(jnp.broadcast_to(c2[:, h:h + 1], c.shape))
        aq = jnp.where(lane == 0, c1, jnp.where(lane == 1, c2_, jnp.where(lane == 2, c3,
                       jnp.where(lane < 6, 1.0, 0.0))))
        ak = jnp.where(lane < 3, 1.0, jnp.where(lane == 3, -c1, jnp.where(lane == 4, -c2_,
                       jnp.where(lane == 5, -c3, 0.0))))
        aq_ref[0, h] = aq.astype(BF16)
        ak_ref[0, h] = ak.astype(BF16)


def _gates(x, w_in_t, layer, bias, bsz, seq):
    m, kdim = x.shape
    steps = seq // GATE_ROWS
    tri = jnp.asarray(np.tril(np.ones((GATE_ROWS, GATE_ROWS), np.float32)), BF16)
    aug_shape = jax.ShapeDtypeStruct((bsz, FOX_HEADS, seq, V7X_LANES), BF16)
    aug_spec = pl.BlockSpec((1, FOX_HEADS, GATE_ROWS, V7X_LANES), lambda b, i: (b, 0, i, 0))
    assert IN_F_LO % F32_SUBLANES == 0 and IN_LOW_LO % F32_SUBLANES == 0

    def rows_spec(n_rows, row0):
        return pl.BlockSpec((pl.Element(1), pl.Element(n_rows), pl.Element(kdim)), lambda b, i: (layer, row0, 0))

    blk = (_nbytes((GATE_ROWS, kdim), x.dtype) + _nbytes((V7X_LANES, kdim), F32)
           + _nbytes((GATE_ROWS, GATE_ROWS), F32) + _nbytes((GATE_ROWS, V7X_LANES), F32)
           + 2 * _nbytes((FOX_HEADS, GATE_ROWS, V7X_LANES), BF16))
    return pl.pallas_call(
        _gates_kernel,
        grid=(bsz, steps),
        in_specs=[pl.BlockSpec((GATE_ROWS, kdim), lambda b, i: (b * steps + i, 0)),
                  rows_spec(FOX_HEADS, IN_F_LO), rows_spec(GLA_GATE_RANK, IN_LOW_LO),
                  pl.BlockSpec((1, V7X_LANES), lambda b, i: (0, 0)),
                  pl.BlockSpec((GATE_ROWS, GATE_ROWS), lambda b, i: (0, 0))],
        out_specs=[pl.BlockSpec((GATE_ROWS, V7X_LANES), lambda b, i: (b * steps + i, 0)), aug_spec, aug_spec],
        out_shape=[jax.ShapeDtypeStruct((m, V7X_LANES), F32), aug_shape, aug_shape],
        scratch_shapes=[pltpu.VMEM((8, V7X_LANES), F32)],
        compiler_params=pltpu.CompilerParams(
            dimension_semantics=("arbitrary", "arbitrary"),
            vmem_limit_bytes=_vmem_limit(blk, 16 * _nbytes((GATE_ROWS, V7X_LANES), F32))),
        name="gates",
    )(x, w_in_t, w_in_t, bias, tri)


def _fox_kernel(q_ref, k_ref, v_ref, aq_ref, ak_ref, g_ref, o_ref, kp_ref, vp_ref, m_ref, acc_ref, *, tq, tk):
    qi = pl.program_id(2)
    n_heads = aq_ref.shape[1]
    d = q_ref.shape[2] // n_heads
    dp = acc_ref.shape[2]

    new_rows = pl.ds(pl.multiple_of(qi * tk, tk), tk)
    lane = lax.broadcasted_iota(jnp.int32, (tk, V7X_LANES), 1)
    for g in range(n_heads):
        kp_ref[g, new_rows, :d] = k_ref[0, :, g * d:(g + 1) * d]
        kp_ref[g, new_rows, d:] = ak_ref[0, g]
        vp_ref[g, new_rows, :d] = v_ref[0, :, g * d:(g + 1) * d]
        vp_ref[g, new_rows, d:] = jnp.where(lane == 0, 1.0, 0.0).astype(BF16)

    qp = [jnp.concatenate([q_ref[0, :, g * d:(g + 1) * d], aq_ref[0, g]], axis=1) for g in range(n_heads)]
    m_ref[...] = jnp.full_like(m_ref, NEG_BIG)
    acc_ref[...] = jnp.zeros_like(acc_ref)

    def step(j, masked):
        start = pl.multiple_of(j * tk, tk)
        for g in range(n_heads):
            s = lax.dot_general(qp[g], kp_ref[g, pl.ds(start, tk), :], _NT, preferred_element_type=F32)
            if masked:
                row = lax.broadcasted_iota(jnp.int32, (tq, tk), 0)
                col = lax.broadcasted_iota(jnp.int32, (tq, tk), 1)
                s = jnp.where(row >= col, s, NEG_BIG)
            m_prev = m_ref[g]
            m_new = jnp.maximum(m_prev, jnp.max(s, axis=-1, keepdims=True))
            p = jnp.exp2(s - jnp.concatenate([m_new] * (tk // V7X_LANES), axis=1)).astype(BF16)
            alpha = jnp.exp2(m_prev - m_new)
            acc_ref[g] = (jnp.concatenate([alpha] * (dp // V7X_LANES), axis=1) * acc_ref[g]
                          + jnp.dot(p, vp_ref[g, pl.ds(start, tk), :], preferred_element_type=F32))
            m_ref[g] = m_new

    unroll = FOX_UNROLL
    shift = unroll.bit_length() - 1
    assert unroll == 1 << shift

    def body(i, carry):
        for u in range(unroll):
            step(unroll * i + u, False)
        return carry

    lax.fori_loop(0, lax.shift_right_logical(qi, shift), body, 0)

    for rem in range(unroll):
        @pl.when((qi & (unroll - 1)) == rem)
        def _():
            for u in range(rem):
                step(qi - rem + u, False)
            step(qi, True)

    for g in range(n_heads):
        acc = acc_ref[g]
        o = acc[:, :d] / acc[:, d:d + 1]
        o = o * lax.rsqrt(jnp.mean(o * o, axis=-1, keepdims=True) + EPS) * g_ref[g]
        o_ref[0, :, g * d:(g + 1) * d] = o.astype(o_ref.dtype)


def _fox(proj, aq, ak, norm_g, *, tq=FOX_TQ, tk=FOX_TK, group=FOX_GROUP):
    _, bsz, seq, gd = proj.shape
    d = FOX_HEAD_DIM
    dp = d + V7X_LANES
    n_groups = FOX_HEADS // group
    assert tq == tk and seq % tq == 0 and FOX_HEADS % group == 0 and gd == group * d
    blk = 4 * _nbytes((tq, gd), BF16) + 2 * _nbytes((group, tq, V7X_LANES), BF16) + _nbytes((group, 8, d), F32)
    scratch = (2 * _nbytes((group, seq, dp), BF16) + _nbytes((group, tq, V7X_LANES), F32)
               + _nbytes((group, tq, dp), F32))
    return pl.pallas_call(
        functools.partial(_fox_kernel, tq=tq, tk=tk),
        grid=(bsz, n_groups, seq // tq),
        in_specs=[pl.BlockSpec((None, 1, tq, gd), lambda b, h, i: (h, b, i, 0)),
                  pl.BlockSpec((None, 1, tk, gd), lambda b, h, i: (n_groups + h, b, i, 0)),
                  pl.BlockSpec((None, 1, tk, gd), lambda b, h, i: (2 * n_groups + h, b, i, 0)),
                  pl.BlockSpec((1, group, tq, V7X_LANES), lambda b, h, i: (b, h, i, 0)),
                  pl.BlockSpec((1, group, tk, V7X_LANES), lambda b, h, i: (b, h, i, 0)),
                  pl.BlockSpec((group, 1, d), lambda b, h, i: (h, 0, 0))],
        out_specs=pl.BlockSpec((1, tq, gd), lambda b, h, i: (b, i, h)),
        out_shape=jax.ShapeDtypeStruct((bsz, seq, FOX_WIDTH), BF16),
        scratch_shapes=[pltpu.VMEM((group, seq, dp), BF16), pltpu.VMEM((group, seq, dp), BF16),
                        pltpu.VMEM((group, tq, V7X_LANES), F32), pltpu.VMEM((group, tq, dp), F32)],
        compiler_params=pltpu.CompilerParams(
            dimension_semantics=("arbitrary", "arbitrary", "arbitrary"),
            vmem_limit_bytes=_vmem_limit(blk, scratch + 8 * group * _nbytes((tq, tk), F32))),
        name="fox_attention",
    )(proj, proj, proj, aq, ak, norm_g)


def _gla_tables(chunk):
    idx = np.arange(chunk)
    masks = []
    half = chunk // 2
    while half >= 1:
        blk = idx // (2 * half)
        second = (idx // half) % 2 == 1
        masks.append(((blk[:, None] == blk[None, :]) & second[:, None] & ~second[None, :]).astype(np.float32))
        half //= 2
    span_chunks = V7X_MXU_DIM // chunk
    tri = np.kron(np.eye(span_chunks, dtype=np.float32), np.tril(np.ones((chunk, chunk), np.float32)))
    return tri, np.stack(masks, axis=0)


def _level_refs(b):
    chunk, width = b.shape
    refs = []
    half = chunk // 2
    while half >= 4:
        pieces = [jnp.broadcast_to(b[s + half - 1:s + half], (2 * half, width)) for s in range(0, chunk, 2 * half)]
        refs.append(pieces[0] if len(pieces) == 1 else jnp.concatenate(pieces, axis=0))
        half //= 2
    prev1 = pltpu.roll(b, 1, 0)
    prev2 = pltpu.roll(b, 2, 0)
    next1 = pltpu.roll(b, chunk - 1, 0)
    pos = lax.broadcasted_iota(jnp.int32, b.shape, 0) & 3
    refs.append(jnp.where(pos == 0, next1, jnp.where(pos == 1, b, jnp.where(pos == 2, prev1, prev2))))
    refs.append(jnp.where((pos & 1) == 0, b, prev1))
    return refs


def _gla_kernel(q_ref, k_ref, v_ref, gg_ref, gates_ref, wup_ref, bgk_ref, gn_ref, tri_ref, mask_ref,
                o_ref, state_ref, la_ref, *, rows, chunk, scale):
    dk, dv = GLA_KEY_DIM, GLA_VAL_DIM

    @pl.when(pl.program_id(1) == 0)
    def _():
        state_ref[...] = jnp.zeros_like(state_ref)

    pre = _dot_split2(gates_ref[0], wup_ref[...]) + bgk_ref[...]
    log_a = _log_sigmoid(pre) * (LOG2_E / GLA_GATE_NORMALIZER)
    span = tri_ref.shape[0]
    width = log_a.shape[1]
    for r in range(0, rows, span):
        parts = jnp.concatenate([p.astype(BF16) for p in _split3(log_a[r:r + span])], axis=1)
        b3 = jnp.dot(tri_ref[...], parts, preferred_element_type=F32)
        la_ref[r:r + span, :] = b3[:, :width] + b3[:, width:2 * width] + b3[:, 2 * width:]

    n_levels = mask_ref.shape[0]
    eye = (lax.broadcasted_iota(jnp.int32, (chunk, chunk), 0)
           == lax.broadcasted_iota(jnp.int32, (chunk, chunk), 1))
    row_dk = lax.broadcasted_iota(jnp.int32, (chunk, dk), 0)
    row_e = lax.broadcasted_iota(jnp.int32, (BF16_SUBLANES, dk), 0)
    ones_cols = jnp.ones((BF16_SUBLANES, V7X_LANES), BF16)

    def head_chunk(r0, h):
        ks = slice(h * dk, (h + 1) * dk)
        vs = slice(h * dv, (h + 1) * dv)
        b = la_ref[pl.ds(r0, chunk), ks]
        kt, kl = divmod(h * dk, q_ref.shape[3])
        vt, vl = divmod(h * dv, v_ref.shape[3])
        q = q_ref[kt, 0, pl.ds(r0, chunk), kl:kl + dk].astype(F32) * scale
        k = k_ref[kt, 0, pl.ds(r0, chunk), kl:kl + dk].astype(F32)
        v = v_ref[vt, 0, pl.ds(r0, chunk), vl:vl + dv]

        state = state_ref[h]
        o = jnp.dot((q * jnp.exp2(b)).astype(BF16), state.astype(BF16), preferred_element_type=F32)

        scores = jnp.zeros((chunk, chunk), F32)
        half = chunk // 2
        for lvl, ref in enumerate(_level_refs(b)):
            u = (jnp.where((row_dk & half) != 0, q, k) * jnp.exp2(-jnp.abs(b - ref))).astype(BF16)
            scores = scores + mask_ref[lvl] * lax.dot_general(u, u, _NT, preferred_element_type=F32)
            half //= 2
        scores = jnp.where(eye, jnp.sum(q * k, axis=-1, keepdims=True), scores)
        o = o + jnp.dot(scores.astype(BF16), v, preferred_element_type=F32)

        b_last = b[chunk - 1:chunk]
        k_dec = (k * jnp.exp2(b_last - b)).astype(BF16)
        e1, e2, e3 = _split3(jnp.broadcast_to(jnp.exp2(b_last), row_e.shape))
        e_rows = jnp.where(row_e == 0, e1, jnp.where(row_e == 1, e2, jnp.where(row_e == 2, e3, 0.0)))
        dec_col = lax.dot_general(e_rows.astype(BF16), ones_cols, _TN, preferred_element_type=F32)
        decay = jnp.concatenate([dec_col] * (dv // V7X_LANES), axis=1)
        state_ref[h] = state * decay + lax.dot_general(k_dec, v, _TN, preferred_element_type=F32)

        o = o * lax.rsqrt(jnp.mean(o * o, axis=-1, keepdims=True) + EPS) * gn_ref[...]
        gate = _silu(gg_ref[vt, 0, pl.ds(r0, chunk), vl:vl + dv].astype(F32))
        o_ref[0, pl.ds(r0, chunk), vs] = (o * gate).astype(o_ref.dtype)

    assert n_levels == chunk.bit_length() - 1

    def chunk_body(c, carry):
        r0 = pl.multiple_of(c * chunk, chunk)
        for h in range(GLA_HEADS):
            head_chunk(r0, h)
        return carry

    lax.fori_loop(0, rows // chunk, chunk_body, 0)


def _gla(proj, col0, gates, w_up_pad, b_gk, norm_g, *, rows=GLA_ROWS, chunk=GLA_CHUNK):
    _, bsz, seq, tw = proj.shape
    kw, vw = GLA_KEY_WIDTH, GLA_WIDTH
    assert seq % rows == 0 and rows % chunk == 0
    assert kw % tw == 0 and col0 % kw == 0 and (col0 + 2 * kw) % vw == 0 and tw % GLA_VAL_DIM == 0
    k_tiles, v_tiles = kw // tw, vw // tw
    q_blk = col0 // kw
    v_blk = (col0 + 2 * kw) // vw
    tri, masks = _gla_tables(chunk)
    blk = (2 * _nbytes((rows, kw), BF16) + 3 * _nbytes((rows, vw), BF16) + _nbytes((rows, V7X_LANES), F32)
           + _nbytes((V7X_LANES, kw), F32) + _nbytes(masks.shape, F32))
    scratch = _nbytes((GLA_HEADS, GLA_KEY_DIM, GLA_VAL_DIM), F32) + _nbytes((rows, kw), F32)
    return pl.pallas_call(
        functools.partial(_gla_kernel, rows=rows, chunk=chunk, scale=GLA_KEY_DIM ** -0.5),
        grid=(bsz, seq // rows),
        in_specs=[pl.BlockSpec((k_tiles, 1, rows, tw), lambda b, t: (q_blk, b, t, 0)),
                  pl.BlockSpec((k_tiles, 1, rows, tw), lambda b, t: (q_blk + 1, b, t, 0)),
                  pl.BlockSpec((v_tiles, 1, rows, tw), lambda b, t: (v_blk, b, t, 0)),
                  pl.BlockSpec((v_tiles, 1, rows, tw), lambda b, t: (v_blk + 1, b, t, 0)),
                  pl.BlockSpec((1, rows, V7X_LANES), lambda b, t: (b, t, 0)),
                  pl.BlockSpec((V7X_LANES, kw), lambda b, t: (0, 0)),
                  pl.BlockSpec((1, kw), lambda b, t: (0, 0)),
                  pl.BlockSpec((1, GLA_VAL_DIM), lambda b, t: (0, 0)),
                  pl.BlockSpec(tri.shape, lambda b, t: (0, 0)),
                  pl.BlockSpec(masks.shape, lambda b, t: (0, 0, 0))],
        out_specs=pl.BlockSpec((1, rows, vw), lambda b, t: (b, t, 0)),
        out_shape=jax.ShapeDtypeStruct((bsz, seq, vw), BF16),
        scratch_shapes=[pltpu.VMEM((GLA_HEADS, GLA_KEY_DIM, GLA_VAL_DIM), F32), pltpu.VMEM((rows, kw), F32)],
        compiler_params=pltpu.CompilerParams(
            dimension_semantics=("arbitrary", "arbitrary"),
            vmem_limit_bytes=_vmem_limit(blk, scratch + (8 << 20))),
        name="gla",
    )(proj, proj, proj, proj, gates, w_up_pad, b_gk.reshape(1, -1), norm_g.reshape(1, -1),
      jnp.asarray(tri, BF16), jnp.asarray(masks))


def _prep_layer(b_f, w_gk_up):
    gate_bias = jnp.concatenate([b_f, jnp.zeros((V7X_LANES - FOX_HEADS,), F32)]).reshape(1, V7X_LANES)
    w_up_pad = jnp.zeros((V7X_LANES, GLA_KEY_WIDTH), F32).at[FOX_HEADS:FOX_HEADS + GLA_GATE_RANK].set(w_gk_up)
    return gate_bias, w_up_pad


def kernel(x, ln_in_g, ln_in_b, w_in, b_f, w_gk_up, b_gk, fox_norm_g, gla_norm_g, w_out, ln1_g, ln1_b,
           w_gate, w_up, w_down, ln2_g, ln2_b):
    bsz, seq, d_model = x.shape
    depth = w_in.shape[0]
    alpha = (2 * depth) ** 0.25
    rows = bsz * seq

    w_in_t = jnp.swapaxes(w_in, 1, 2)
    n_fox_tiles = IN_F_LO // MM_TN
    n_proj_tiles = n_fox_tiles + (IN_LOW_LO - IN_G_LO) // MM_TN

    def proj_row_start(j):
        return pl.multiple_of(j * MM_TN + jnp.where(j >= n_fox_tiles, IN_G_LO - IN_F_LO, 0), F32_SUBLANES)

    t = x.reshape(rows, d_model)
    ln_g, ln_b = ln_in_g, ln_in_b
    x16, mu, rstd = _layer_norm(t, ln_g, ln_b)
    for l in range(depth):
        gate_bias, w_up_pad = _prep_layer(b_f[l], w_gk_up[l])

        proj = _matmul_wt(x16, w_in_t, l, n_proj_tiles, proj_row_start, BF16, scaled_cols=FOX_WIDTH,
                          scale=FOX_HEAD_DIM ** -0.5 * LOG2_E, out_split=MM_TN // PROJ_TILE)
        proj = proj.reshape(-1, bsz, seq, PROJ_TILE)
        gates, aug_q, aug_k = _gates(x16, w_in_t, l, gate_bias, bsz, seq)
        gates = gates.reshape(bsz, seq, V7X_LANES)

        o_fox = _fox(proj, aug_q, aug_k, fox_norm_g[l].reshape(FOX_HEADS, 1, FOX_HEAD_DIM))
        o_gla = _gla(proj, IN_F_LO, gates, w_up_pad, b_gk[l], gla_norm_g[l])

        mixed = [o_fox.reshape(rows, FOX_WIDTH), o_gla.reshape(rows, GLA_WIDTH)]
        t = _matmul(mixed, w_out, l, d_model, (t, mu, rstd, ln_g, ln_b), alpha)
        ln_g, ln_b = ln1_g[l], ln1_b[l]
        x16, mu, rstd = _layer_norm(t, ln_g, ln_b)

        hidden = _gate_up(x16, w_gate, w_up, l)
        t = _down(hidden, w_down, l, (t, mu, rstd, ln_g, ln_b), alpha)
        ln_g, ln_b = ln2_g[l], ln2_b[l]
        if l < depth - 1:
            x16, mu, rstd = _layer_norm(t, ln_g, ln_b)
    out, = _layer_norm(t, ln_g, ln_b, final=True)
    return out.reshape(bsz, seq, d_model)
```
